```python
import math
import jax
import jax.numpy as jnp
from jax import lax
import numpy as np

D_MODEL = 4096
BATCH = 1
SEQ = 8192
DEPTH = 1
DEC_BATCH = 128
DEC_SEQ = 4
PAST_LEN = 8192
PAGE_SIZE = 128

GDN_HEADS = D_MODEL // 256
GDN_DK = 128
GDN_DV = 128
GDN_QK_WIDTH = GDN_HEADS * GDN_DK
GDN_WIDTH = GDN_HEADS * GDN_DV
CONV_W = 4
CONV_CH = 2 * GDN_QK_WIDTH + GDN_WIDTH
GDN_CHUNK = 64
SWA_HEADS = D_MODEL // 128
SWA_KV_HEADS = 8
SWA_HD = 64
SWA_GROUP = SWA_HEADS // SWA_KV_HEADS
SWA_WIDTH = SWA_HEADS * SWA_HD
SWA_KV_WIDTH = SWA_KV_HEADS * SWA_HD
WINDOW = 128
N_BUCKETS = 32
MAX_DISTANCE = WINDOW
N_GROUPS = 8
EXPERTS_PER_GROUP = 8
N_EXPERTS = N_GROUPS * EXPERTS_PER_GROUP
TOP_K = 2
D_EXPERT = D_MODEL // 4
MOE_BLOCK = 128
IN_WIDTH = 2 * GDN_QK_WIDTH + 2 * GDN_WIDTH + 2 * GDN_HEADS + SWA_WIDTH + 2 * SWA_KV_WIDTH
EPS = 1e-6

kernel_name = 'hymba_gdn_swa_hmoe_step'


def _rmsnorm(x, g):
    xf = x.astype(jnp.float32)
    y = xf * lax.rsqrt(jnp.mean(xf * xf, axis=-1, keepdims=True) + EPS)
    return (y * g.astype(jnp.float32)).astype(x.dtype)


def _l2norm(x):
    xf = x.astype(jnp.float32)
    return xf * lax.rsqrt(jnp.sum(xf * xf, axis=-1, keepdims=True) + EPS)


def _short_conv(x_raw, buf, conv_w):
    L = x_raw.shape[1]
    xp = jnp.concatenate([buf.astype(x_raw.dtype), x_raw], axis=1)
    y = xp[:, 0:L] * conv_w[0]
    for i in range(1, CONV_W):
        y = y + xp[:, i:i + L] * conv_w[i]
    return jax.nn.silu(y), xp[:, L:]


def _gated_delta_chunked(q, k, v, beta, g, S0):
    B, L, H, DK = q.shape
    DV = v.shape[-1]
    C = min(GDN_CHUNK, L)
    n = -(-L // C)
    pad = n * C - L

    def blocks(a):
        a = jnp.pad(a, [(0, 0), (0, pad)] + [(0, 0)] * (a.ndim - 2))
        a = a.reshape((B, n, C) + a.shape[2:])
        return jnp.moveaxis(jnp.moveaxis(a, 3, 2), 1, 0)

    qc, kc, vc, bc = blocks(q), blocks(k), blocks(v), blocks(beta)
    gcum = jnp.cumsum(blocks(g), axis=-1)
    tri = jnp.tril(jnp.ones((C, C), dtype=bool))
    strict = jnp.tril(jnp.ones((C, C), dtype=bool), k=-1)
    decay = jnp.exp(jnp.where(tri, gcum[..., :, None] - gcum[..., None, :], -jnp.inf))
    kk = jnp.einsum('nbhcd,nbhsd->nbhcs', kc, kc)
    lower = jnp.where(strict, bc[..., :, None] * kk * decay, 0.0)
    a_mat = jnp.eye(C, dtype=jnp.float32) + lower
    rhs = jnp.concatenate([vc * bc[..., None], kc * (bc * jnp.exp(gcum))[..., None]], axis=-1)
    sol = lax.linalg.triangular_solve(a_mat, rhs, left_side=True, lower=True, unit_diagonal=True)
    u_base, k_cumdecay = sol[..., :DV], sol[..., DV:]
    qk = jnp.einsum('nbhcd,nbhsd->nbhcs', qc, kc) * decay

    def step(S, xs):
        q_i, k_i, ub_i, kcd_i, qk_i, g_i = xs
        u = ub_i - jnp.einsum('bhcd,bhdv->bhcv', kcd_i, S)
        o = (jnp.einsum('bhcd,bhdv->bhcv', q_i * jnp.exp(g_i)[..., None], S)
             + jnp.einsum('bhcs,bhsv->bhcv', qk_i, u))
        g_last = g_i[..., -1]
        S = (S * jnp.exp(g_last)[..., None, None]
             + jnp.einsum('bhcd,bhcv->bhdv', k_i * jnp.exp(g_last[..., None] - g_i)[..., None], u))
        return S, o

    S_fin, o = lax.scan(step, S0, (qc, kc, u_base, k_cumdecay, qk, gcum))
    o = jnp.moveaxis(jnp.moveaxis(o, 0, 1), 2, 3).reshape(B, n * C, H, DV)[:, :L]
    return o, S_fin


def _t5_bucket(dist):
    d = jnp.maximum(dist, 0)
    max_exact = N_BUCKETS // 2
    large = max_exact + (jnp.log(jnp.maximum(d, 1).astype(jnp.float32) / max_exact)
                         / math.log(MAX_DISTANCE / max_exact) * (N_BUCKETS - max_exact)).astype(jnp.int32)
    large = jnp.minimum(large, N_BUCKETS - 1)
    return jnp.where(d < max_exact, d, large)


def _rel_bias(dist, rel_bias):
    Q, S = dist.shape
    b = rel_bias.astype(jnp.float32)[_t5_bucket(dist)]
    return jnp.transpose(b.reshape(Q, S, SWA_KV_HEADS, SWA_GROUP), (2, 3, 0, 1))


def _sink_softmax(logits, valid, sinks):
    s = sinks.astype(jnp.float32).reshape(SWA_KV_HEADS, SWA_GROUP, 1, 1)
    logits = jnp.where(valid, logits, -jnp.inf)
    m = jnp.maximum(jnp.max(logits, axis=-1, keepdims=True), s)
    p = jnp.exp(logits - m)
    return p / (jnp.sum(p, axis=-1, keepdims=True) + jnp.exp(s - m))


def _swa_banded(q, k, v, rel_bias, sinks):
    B, L = q.shape[:2]
    nb = -(-L // WINDOW)
    pad = nb * WINDOW - L
    padr = lambda a: jnp.pad(a, ((0, 0), (0, pad), (0, 0), (0, 0)))
    q, k, v = padr(q), padr(k), padr(v)
    Lp = nb * WINDOW
    qb = q.reshape(B, nb, WINDOW, SWA_KV_HEADS, SWA_GROUP, SWA_HD)

    def band(a):
        prev = jnp.pad(a, ((0, 0), (WINDOW, 0), (0, 0), (0, 0)))[:, :Lp]
        return jnp.concatenate([prev.reshape(B, nb, WINDOW, SWA_KV_HEADS, SWA_HD),
                                a.reshape(B, nb, WINDOW, SWA_KV_HEADS, SWA_HD)], axis=2)

    kb, vb = band(k), band(v)
    logits = jnp.einsum('bnqkgd,bnskd->bnkgqs', qb, kb, preferred_element_type=jnp.float32) * (SWA_HD ** -0.5)
    qi = jnp.arange(WINDOW, dtype=jnp.int32)
    sj = jnp.arange(2 * WINDOW, dtype=jnp.int32)
    dist = (WINDOW + qi)[:, None] - sj[None, :]
    kpos = (jnp.arange(nb, dtype=jnp.int32) * WINDOW - WINDOW)[:, None] + sj[None, :]
    valid = ((dist >= 0) & (dist < WINDOW))[None, :, :] & (kpos >= 0)[:, None, :]
    probs = _sink_softmax(logits + _rel_bias(dist, rel_bias), valid[None, :, None, None], sinks)
    o = jnp.einsum('bnkgqs,bnskd->bnqkgd', probs.astype(v.dtype), vb)
    return o.reshape(B, Lp, SWA_WIDTH)[:, :L]


def _swa_cached(q, k, v, k_buf, v_buf, rel_bias, sinks):
    B, S = q.shape[:2]
    Wb = k_buf.shape[1]
    kk = jnp.concatenate([k_buf.astype(k.dtype), k], axis=1)
    vv = jnp.concatenate([v_buf.astype(v.dtype), v], axis=1)
    qg = q.reshape(B, S, SWA_KV_HEADS, SWA_GROUP, SWA_HD)
    logits = jnp.einsum('bqkgd,bskd->bkgqs', qg, kk, preferred_element_type=jnp.float32) * (SWA_HD ** -0.5)
    dist = (Wb + jnp.arange(S, dtype=jnp.int32))[:, None] - jnp.arange(Wb + S, dtype=jnp.int32)[None, :]
    valid = (dist >= 0) & (dist < WINDOW)
    probs = _sink_softmax(logits + _rel_bias(dist, rel_bias), valid, sinks)
    o = jnp.einsum('bkgqs,bskd->bqkgd', probs.astype(v.dtype), vv).reshape(B, S, SWA_WIDTH)
    return o, kk[:, S:], vv[:, S:]


def _moe_dispatch(t, expert_idx, gates, w_gate, w_up, w_down):
    T, D = t.shape
    A = T * TOP_K
    flat_e = expert_idx.reshape(A)
    flat_tok = jnp.repeat(jnp.arange(T, dtype=jnp.int32), TOP_K)
    flat_w = gates.reshape(A)
    order = jnp.argsort(flat_e)
    sorted_e = flat_e[order]
    counts = jnp.zeros((N_EXPERTS,), jnp.int32).at[flat_e].add(1)
    padded = (counts + MOE_BLOCK - 1) // MOE_BLOCK * MOE_BLOCK
    pad_end = jnp.cumsum(padded)
    pad_start = pad_end - padded
    start = jnp.cumsum(counts) - counts
    dest = pad_start[sorted_e] + jnp.arange(A, dtype=jnp.int32) - start[sorted_e]
    n_blocks = -(-A // MOE_BLOCK) + N_EXPERTS
    P = n_blocks * MOE_BLOCK
    row_tok = jnp.full((P,), T, jnp.int32).at[dest].set(flat_tok[order])
    row_w = jnp.zeros((P,), t.dtype).at[dest].set(flat_w[order].astype(t.dtype))
    block_e = jnp.minimum(jnp.searchsorted(pad_end, jnp.arange(n_blocks, dtype=jnp.int32) * MOE_BLOCK, side='right'),
                          N_EXPERTS - 1).astype(jnp.int32)
    t_pad = jnp.concatenate([t, jnp.zeros((1, D), t.dtype)], axis=0)
    xb = t_pad[row_tok].reshape(n_blocks, MOE_BLOCK, D)

    def expert_block(args):
        xblk, e = args
        hid = jax.nn.silu(xblk @ w_gate[e]) * (xblk @ w_up[e])
        return hid @ w_down[e]

    yb = lax.map(expert_block, (xb, block_e)).reshape(P, D)
    out = jnp.zeros((T + 1, D), t.dtype).at[row_tok].add(yb * row_w[:, None])
    return out[:T]


def _hier_moe(t, w_rg, b_rg, w_re, b_re, w_gate, w_up, w_down):
    T = t.shape[0]
    lg = (t @ w_rg).astype(jnp.float32) + b_rg.astype(jnp.float32)
    pg = jax.nn.softmax(lg, axis=-1)
    grp = jnp.argmax(lg, axis=-1).astype(jnp.int32)
    p_grp = jnp.take_along_axis(pg, grp[:, None], axis=-1)
    le = ((t @ w_re).astype(jnp.float32) + b_re.astype(jnp.float32)).reshape(T, N_GROUPS, EXPERTS_PER_GROUP)
    le_sel = jnp.take_along_axis(le, grp[:, None, None], axis=1)[:, 0]
    pe = jax.nn.softmax(le_sel, axis=-1)
    top_p, top_i = lax.top_k(pe, TOP_K)
    gates = p_grp * top_p / jnp.sum(top_p, axis=-1, keepdims=True)
    expert_idx = grp[:, None] * EXPERTS_PER_GROUP + top_i.astype(jnp.int32)
    return _moe_dispatch(t, expert_idx, gates, w_gate, w_up, w_down)


def _layer(x, c, conv_buf, S0, k_buf, v_buf, w_ada, b_ada, norm_mix, w_in, conv_w, a_log, dt_bias, gdn_norm,
           swa_sinks, rel_bias, w_out, norm_moe, w_rg, b_rg, w_re, b_re, w_gate, w_up, w_down):
    B, L, D = x.shape
    mod = jax.nn.silu(c) @ w_ada + b_ada
    sh1, sc1, gt1, sh2, sc2, gt2 = [m[:, None, :] for m in jnp.split(mod, 6, axis=-1)]
    h = _rmsnorm(x, norm_mix) * (1 + sc1) + sh1
    proj = h @ w_in
    sizes = [GDN_QK_WIDTH, GDN_QK_WIDTH, GDN_WIDTH, GDN_WIDTH, GDN_HEADS, GDN_HEADS,
             SWA_WIDTH, SWA_KV_WIDTH, SWA_KV_WIDTH]
    offsets = [int(o) for o in np.cumsum(sizes)[:-1]]
    gq, gk, gv, gz, gb, ga, sq, sk, sv = jnp.split(proj, offsets, axis=-1)
    qkv, conv_new = _short_conv(jnp.concatenate([gq, gk, gv], axis=-1), conv_buf, conv_w)
    cq, ck, cv = jnp.split(qkv, [GDN_QK_WIDTH, 2 * GDN_QK_WIDTH], axis=-1)
    q = _l2norm(cq.reshape(B, L, GDN_HEADS, GDN_DK)) * (GDN_DK ** -0.5)
    k = _l2norm(ck.reshape(B, L, GDN_HEADS, GDN_DK))
    v = cv.reshape(B, L, GDN_HEADS, GDN_DV).astype(jnp.float32)
    beta = jax.nn.sigmoid(gb.astype(jnp.float32))
    g = -jnp.exp(a_log.astype(jnp.float32)) * jax.nn.softplus(ga.astype(jnp.float32) + dt_bias.astype(jnp.float32))
    o_g, S_new = _gated_delta_chunked(q, k, v, beta, g, S0.astype(jnp.float32))
    o_g = _rmsnorm(o_g, gdn_norm) * jax.nn.silu(gz.reshape(B, L, GDN_HEADS, GDN_DV).astype(jnp.float32))
    o_g = o_g.astype(x.dtype).reshape(B, L, GDN_WIDTH)
    qs = sq.reshape(B, L, SWA_HEADS, SWA_HD)
    ks = sk.reshape(B, L, SWA_KV_HEADS, SWA_HD)
    vs = sv.reshape(B, L, SWA_KV_HEADS, SWA_HD)
    if k_buf is None:
        o_s = _swa_banded(qs, ks, vs, rel_bias, swa_sinks)
        w = min(WINDOW, L)
        k_new, v_new = ks[:, L - w:], vs[:, L - w:]
    else:
        o_s, k_new, v_new = _swa_cached(qs, ks, vs, k_buf, v_buf, rel_bias, swa_sinks)
    mix = jnp.concatenate([o_g, o_s], axis=-1) @ w_out
    x = x + gt1 * mix
    h2 = _rmsnorm(x, norm_moe) * (1 + sc2) + sh2
    x = x + gt2 * _hier_moe(h2.reshape(B * L, D), w_rg, b_rg, w_re, b_re, w_gate, w_up, w_down).reshape(B, L, D)
    return x, S_new, conv_new, k_new, v_new


def setup_inputs(seed: int = 0) -> dict:
    key = jax.random.key(seed)
    ks = jax.random.split(key, 28)
    f32 = jnp.float32
    nrm = lambda k, shape, s: jax.random.normal(k, shape, f32) * s
    D = D_MODEL
    dt = jnp.exp(jax.random.uniform(ks[14], (DEPTH, GDN_HEADS), f32, math.log(1e-3), math.log(1e-1)))
    return {
        'x_prompt': nrm(ks[0], (BATCH, SEQ, D), 1.0),
        'x_sample': nrm(ks[1], (DEC_BATCH, DEC_SEQ, D), 1.0),
        'state_gdn': nrm(ks[2], (DEPTH, DEC_BATCH, GDN_HEADS, GDN_DK, GDN_DV), 0.1),
        'state_conv': nrm(ks[3], (DEPTH, DEC_BATCH, CONV_W - 1, CONV_CH), 1.0),
        'cache_swa_k': nrm(ks[4], (DEPTH, DEC_BATCH, min(WINDOW, PAST_LEN), SWA_KV_HEADS, SWA_HD), 1.0),
        'cache_swa_v': nrm(ks[5], (DEPTH, DEC_BATCH, min(WINDOW, PAST_LEN), SWA_KV_HEADS, SWA_HD), 1.0),
        'c_prompt': nrm(ks[6], (BATCH, D), 1.0),
        'c_sample': nrm(ks[7], (DEC_BATCH, D), 1.0),
        'w_ada': nrm(ks[8], (DEPTH, D, 6 * D), 0.5 * D ** -0.5),
        'b_ada': nrm(ks[9], (DEPTH, 6 * D), 0.02),
        'norm_mix': 1.0 + nrm(ks[10], (DEPTH, D), 0.02),
        'w_in': nrm(ks[11], (DEPTH, D, IN_WIDTH), D ** -0.5),
        'conv_w': nrm(ks[12], (DEPTH, CONV_W, CONV_CH), CONV_W ** -0.5),
        'a_log': jnp.log(jax.random.uniform(ks[13], (DEPTH, GDN_HEADS), f32, 1.0, 16.0)),
        'dt_bias': dt + jnp.log(-jnp.expm1(-dt)),
        'gdn_norm': 1.0 + nrm(ks[15], (DEPTH, GDN_DV), 0.02),
        'swa_sinks': nrm(ks[16], (DEPTH, SWA_HEADS), 0.5),
        'rel_bias': nrm(ks[17], (N_BUCKETS, SWA_HEADS), 0.5),
        'w_out': nrm(ks[18], (DEPTH, GDN_WIDTH + SWA_WIDTH, D), (GDN_WIDTH + SWA_WIDTH) ** -0.5),
        'norm_moe': 1.0 + nrm(ks[19], (DEPTH, D), 0.02),
        'router_group': nrm(ks[20], (DEPTH, D, N_GROUPS), D ** -0.5),
        'router_group_bias': nrm(ks[21], (DEPTH, N_GROUPS), 0.01),
        'router_expert': nrm(ks[22], (DEPTH, D, N_EXPERTS), D ** -0.5),
        'router_expert_bias': nrm(ks[23], (DEPTH, N_EXPERTS), 0.01),
        'w_gate': nrm(ks[24], (DEPTH, N_EXPERTS, D, D_EXPERT), D ** -0.5),
        'w_up': nrm(ks[25], (DEPTH, N_EXPERTS, D, D_EXPERT), D ** -0.5),
        'w_down': nrm(ks[26], (DEPTH, N_EXPERTS, D_EXPERT, D), D_EXPERT ** -0.5),
        'norm_final': 1.0 + nrm(ks[27], (D,), 0.02),
    }


def reference(x_prompt, x_sample, state_gdn, state_conv, cache_swa_k, cache_swa_v, c_prompt, c_sample,
              w_ada, b_ada, norm_mix, w_in, conv_w, a_log, dt_bias, gdn_norm, swa_sinks, rel_bias, w_out,
              norm_moe, router_group, router_group_bias, router_expert, router_expert_bias,
              w_gate, w_up, w_down, norm_final):
    B = x_prompt.shape[0]
    xp, xs = x_prompt, x_sample
    gdn_p, gdn_s, conv_p, conv_s, kp, ksm, vp, vsm = [], [], [], [], [], [], [], []
    for l in range(DEPTH):
        lw = (w_ada[l], b_ada[l], norm_mix[l], w_in[l], conv_w[l], a_log[l], dt_bias[l], gdn_norm[l],
              swa_sinks[l], rel_bias, w_out[l], norm_moe[l], router_group[l], router_group_bias[l],
              router_expert[l], router_expert_bias[l], w_gate[l], w_up[l], w_down[l])
        conv0 = jnp.zeros((B, CONV_W - 1, CONV_CH), x_prompt.dtype)
        s0 = jnp.zeros((B, GDN_HEADS, GDN_DK, GDN_DV), jnp.float32)
        xp, S_p, cv_p, k_p, v_p = _layer(xp, c_prompt, conv0, s0, None, None, *lw)
        xs, S_s, cv_s, k_s, v_s = _layer(xs, c_sample, state_conv[l], state_gdn[l], cache_swa_k[l], cache_swa_v[l], *lw)
        gdn_p.append(S_p.astype(state_gdn.dtype))
        gdn_s.append(S_s.astype(state_gdn.dtype))
        conv_p.append(cv_p.astype(state_conv.dtype))
        conv_s.append(cv_s.astype(state_conv.dtype))
        kp.append(k_p.astype(cache_swa_k.dtype))
        ksm.append(k_s.astype(cache_swa_k.dtype))
        vp.append(v_p.astype(cache_swa_v.dtype))
        vsm.append(v_s.astype(cache_swa_v.dtype))
    y_prompt = _rmsnorm(xp, norm_final)
    y_sample = _rmsnorm(xs, norm_final)
    return (y_prompt, y_sample, jnp.stack(gdn_p), jnp.stack(gdn_s), jnp.stack(conv_p), jnp.stack(conv_s),
            jnp.stack(kp), jnp.stack(ksm), jnp.stack(vp), jnp.stack(vsm))
```

```python
import functools
import math

import jax
import jax.numpy as jnp
from jax import lax
from jax.experimental import pallas as pl
from jax.experimental.pallas import tpu as pltpu

F32 = jnp.float32
BF16 = jnp.bfloat16
EPS = 1e-6
NEG = -1e30

LANE = 128
GDN_HEADS = 16
GDN_D = 128
GDN_W = GDN_HEADS * GDN_D
CONV_W = 4
SWA_HEADS = 32
SWA_KV = 8
SWA_HD = 64
SWA_W = SWA_HEADS * SWA_HD
SWA_KVW = SWA_KV * SWA_HD
WINDOW = 128
N_BUCKETS = 32
N_GROUPS = 8
EPG = 8
N_EXPERTS = 64
D_EXPERT = 1024
MOE_ROWS = 384
MOE_SUB = 128
MOE_NC = 4
VMEM_LIMIT = 56 * 1024 * 1024


def _cparams(sem):
    return pltpu.CompilerParams(dimension_semantics=sem, vmem_limit_bytes=VMEM_LIMIT)


def _sigmoid(x):
    return 1.0 / (1.0 + jnp.exp(-x))


def _silu(x):
    return x * _sigmoid(x)


def _dot(a, b):
    return jnp.dot(a, b, preferred_element_type=F32)


def _dot_nt(a, b):
    return lax.dot_general(a, b, (((1,), (1,)), ((), ())), preferred_element_type=F32)


def _dot_tn(a, b):
    return lax.dot_general(a, b, (((0,), (0,)), ((), ())), preferred_element_type=F32)


def _ada_kernel(c_ref, w_ref, b_ref, o_ref):
    a = _silu(c_ref[...]).astype(BF16)
    o_ref[...] = _dot(a, w_ref[...].astype(BF16)) + b_ref[...]


def _adaln(c, w, b, tn=512):
    m, d = c.shape
    n = w.shape[1]
    return pl.pallas_call(
        _ada_kernel,
        grid=(n // tn,),
        in_specs=[pl.BlockSpec((m, d), lambda j: (0, 0)),
                  pl.BlockSpec((d, tn), lambda j: (0, j)),
                  pl.BlockSpec((1, tn), lambda j: (0, j))],
        out_specs=pl.BlockSpec((m, tn), lambda j: (0, j)),
        out_shape=jax.ShapeDtypeStruct((m, n), F32),
        compiler_params=_cparams(("arbitrary",)),
        name="adaln",
    )(c, w, b.reshape(1, n))


def _norm_kernel(x_ref, g_ref, sc_ref, sh_ref, o_ref):
    x = x_ref[...]
    y = x * lax.rsqrt(jnp.mean(x * x, axis=-1, keepdims=True) + EPS) * g_ref[...]
    o_ref[...] = (y * (1.0 + sc_ref[...]) + sh_ref[...]).astype(o_ref.dtype)


def _mod_spec(mod, tm, d):
    if mod.shape[0] == 1:
        return pl.BlockSpec((1, d), lambda i: (0, 0))
    return pl.BlockSpec((tm, d), lambda i: (i, 0))


def _norm_mod(x, g, sc, sh, tm=256):
    t, d = x.shape
    tm = min(tm, t)
    return pl.pallas_call(
        _norm_kernel,
        grid=(t // tm,),
        in_specs=[pl.BlockSpec((tm, d), lambda i: (i, 0)),
                  pl.BlockSpec((1, d), lambda i: (0, 0)),
                  _mod_spec(sc, tm, d), _mod_spec(sh, tm, d)],
        out_specs=pl.BlockSpec((tm, d), lambda i: (i, 0)),
        out_shape=jax.ShapeDtypeStruct((t, d), BF16),
        compiler_params=_cparams(("arbitrary",)),
        name="norm_mod",
    )(x, g.reshape(1, d), sc, sh)


def _split3(x):
    hi = x.astype(BF16)
    r1 = x - hi.astype(F32)
    mid = r1.astype(BF16)
    lo = (r1 - mid.astype(F32)).astype(BF16)
    return hi, mid, lo


def _norm_router_kernel(x_ref, g_ref, sc_ref, sh_ref, wr_ref, br_ref, o_ref, e_ref, p_ref):
    x = x_ref[...]
    y = x * lax.rsqrt(jnp.mean(x * x, axis=-1, keepdims=True) + EPS) * g_ref[...]
    t = y * (1.0 + sc_ref[...]) + sh_ref[...]
    o_ref[...] = t.astype(o_ref.dtype)
    t_hi, t_mid, _ = _split3(t)
    w_hi = wr_ref[0]
    w_mid = wr_ref[1]
    lg = _dot(t_hi, w_hi) + (_dot(t_mid, w_hi) + _dot(t_hi, w_mid)) + br_ref[...]
    lane = lax.broadcasted_iota(jnp.int32, lg.shape, 1)
    big = jnp.int32(4 * LANE)
    lgrp = jnp.where(lane < N_GROUPS, lg, NEG)
    mg = jnp.max(lgrp, axis=-1, keepdims=True)
    grp = jnp.min(jnp.where(lgrp == mg, lane, big), axis=-1, keepdims=True)
    p_grp = 1.0 / jnp.sum(jnp.exp(lgrp - mg), axis=-1, keepdims=True)
    lo = N_GROUPS + grp * EPG
    emask = jnp.logical_and(lane >= lo, lane < lo + EPG)
    le = jnp.where(emask, lg, NEG)
    m1 = jnp.max(le, axis=-1, keepdims=True)
    i1 = jnp.min(jnp.where(le == m1, lane, big), axis=-1, keepdims=True)
    le2 = jnp.where(lane == i1, NEG, le)
    m2 = jnp.max(le2, axis=-1, keepdims=True)
    i2 = jnp.min(jnp.where(le2 == m2, lane, big), axis=-1, keepdims=True)
    e2 = jnp.exp(m2 - m1)
    w1 = p_grp / (1.0 + e2)
    w2 = p_grp * e2 / (1.0 + e2)
    e_ref[...] = jnp.where(lane == 0, i1 - N_GROUPS, jnp.where(lane == 1, i2 - N_GROUPS, 0))
    p_ref[...] = jnp.where(lane == 0, w1, jnp.where(lane == 1, w2, 0.0))


def _norm_router(x, g, sc, sh, wr, br, tm=256):
    t, d = x.shape
    tm = min(tm, t)
    return pl.pallas_call(
        _norm_router_kernel,
        grid=(t // tm,),
        in_specs=[pl.BlockSpec((tm, d), lambda i: (i, 0)),
                  pl.BlockSpec((1, d), lambda i: (0, 0)),
                  _mod_spec(sc, tm, d), _mod_spec(sh, tm, d),
                  pl.BlockSpec((2, d, LANE), lambda i: (0, 0, 0)),
                  pl.BlockSpec((1, LANE), lambda i: (0, 0))],
        out_specs=[pl.BlockSpec((tm, d), lambda i: (i, 0)),
                   pl.BlockSpec((tm, LANE), lambda i: (i, 0)),
                   pl.BlockSpec((tm, LANE), lambda i: (i, 0))],
        out_shape=[jax.ShapeDtypeStruct((t, d), BF16),
                   jax.ShapeDtypeStruct((t, LANE), jnp.int32),
                   jax.ShapeDtypeStruct((t, LANE), F32)],
        compiler_params=_cparams(("arbitrary",)),
        name="norm_router",
    )(x, g.reshape(1, d), sc, sh, wr, br)


def _mm_kernel(x_ref, w_ref, o_ref):
    o_ref[...] = _dot(x_ref[...], w_ref[...]).astype(o_ref.dtype)


def _matmul(x, w, tm, tn, out_dtype=F32, name="matmul"):
    m, k = x.shape
    n = w.shape[1]
    tm = min(tm, m)
    tn = min(tn, n)
    return pl.pallas_call(
        _mm_kernel,
        grid=(m // tm, n // tn),
        in_specs=[pl.BlockSpec((tm, k), lambda i, j: (i, 0)),
                  pl.BlockSpec((k, tn), lambda i, j: (0, j))],
        out_specs=pl.BlockSpec((tm, tn), lambda i, j: (i, j)),
        out_shape=jax.ShapeDtypeStruct((m, n), out_dtype),
        compiler_params=_cparams(("arbitrary", "arbitrary")),
        name=name,
    )(x, w)


def _outproj_kernel(a_ref, b_ref, wa_ref, wb_ref, x_ref, g_ref, o_ref):
    mix = _dot(a_ref[...], wa_ref[...]) + _dot(b_ref[...], wb_ref[...])
    o_ref[...] = x_ref[...] + g_ref[...] * mix


def _outproj(a, b, w, x, gate, tm=1024, tn=512):
    m, ka = a.shape
    kb = b.shape[1]
    n = w.shape[1]
    tm = min(tm, m)
    if gate.shape[0] == 1:
        gspec = pl.BlockSpec((1, tn), lambda i, j: (0, j))
    else:
        gspec = pl.BlockSpec((tm, tn), lambda i, j: (i, j))
    return pl.pallas_call(
        _outproj_kernel,
        grid=(m // tm, n // tn),
        in_specs=[pl.BlockSpec((tm, ka), lambda i, j: (i, 0)),
                  pl.BlockSpec((tm, kb), lambda i, j: (i, 0)),
                  pl.BlockSpec((ka, tn), lambda i, j: (0, j)),
                  pl.BlockSpec((kb, tn), lambda i, j: (1, j)),
                  pl.BlockSpec((tm, tn), lambda i, j: (i, j)),
                  gspec],
        out_specs=pl.BlockSpec((tm, tn), lambda i, j: (i, j)),
        out_shape=jax.ShapeDtypeStruct((m, n), F32),
        compiler_params=_cparams(("arbitrary", "arbitrary")),
        name="outproj",
    )(a, b, w, w, x, gate)


GDN_HU = 4


def _rowpad(a, rows):
    if a.shape[0] == rows:
        return a
    return jnp.concatenate([a, jnp.zeros((rows - a.shape[0], a.shape[1]), a.dtype)], axis=0)


def _gdn_head(qn, kn, vh, bcol, gcol, s_prev, lvl_ref, c):
    cp = LANE
    lane = lax.broadcasted_iota(jnp.int32, (c, cp), 1)
    row = lax.broadcasted_iota(jnp.int32, (c, cp), 0)
    g_hi = gcol.astype(BF16).astype(F32)
    r1 = gcol - g_hi
    g_mid = r1.astype(BF16).astype(F32)
    g_lo = r1 - g_mid
    a_mat = jnp.where(lane == 0, g_hi, jnp.where(lane == 1, g_mid, jnp.where(lane == 2, g_lo,
                      jnp.where(lane < 6, 1.0, 0.0))))
    b_mat = jnp.where(lane < 3, 1.0, jnp.where(lane == 3, -g_hi, jnp.where(lane == 4, -g_mid,
                      jnp.where(lane == 5, -g_lo, 0.0))))
    diff = _dot_nt(a_mat.astype(BF16), _rowpad(b_mat, cp).astype(BF16))
    tri = lane <= row
    decay = jnp.where(tri, jnp.exp(jnp.minimum(diff, 0.0)), 0.0)
    eg = jnp.exp(gcol)
    kn16 = kn.astype(BF16)
    qk_kk = _dot_nt(jnp.concatenate([qn.astype(BF16), kn16], axis=0), _rowpad(kn16, cp))
    qk = qk_kk[:c]
    kk = qk_kk[c:]
    lmat = (bcol * kk) * decay
    n_mat = -(lmat * lvl_ref[0])
    for lv in range(1, lvl_ref.shape[0]):
        b = lmat * lvl_ref[lv]
        w_mat = b + _dot(n_mat.astype(BF16), _rowpad(b.astype(BF16), cp))
        n_mat = n_mat - w_mat - _dot(w_mat.astype(BF16), _rowpad(n_mat.astype(BF16), cp))
    rhs = jnp.concatenate([vh * bcol, kn * (bcol * eg)], axis=1)
    sol = rhs + _dot(n_mat.astype(BF16), _rowpad(rhs.astype(BF16), cp))
    u_base = sol[:, :GDN_D]
    kcd = sol[:, GDN_D:]
    kq_s = _dot(jnp.concatenate([kcd.astype(BF16), (qn * eg).astype(BF16)], axis=0), s_prev.astype(BF16))
    u = u_base - kq_s[:c]
    o = kq_s[c:] + _dot((qk * decay).astype(BF16), _rowpad(u.astype(BF16), cp))
    g_last = gcol[c - 1:c, :]
    kd = kn * jnp.exp(g_last - gcol)
    s_new = s_prev * jnp.exp(g_last) + _dot_tn(_rowpad(kd.astype(BF16), cp), _rowpad(u.astype(BF16), cp))
    return o, s_new


def _gdn_kernel(*refs, c, l_valid, prompt, n_chunks):
    if prompt:
        (q_ref, k_ref, v_ref, z_ref, ba_ref, cw_ref, hp_ref, gn_ref, lvl_ref,
         o_ref, sout_ref, cout_ref, xe_ref, cb_ref, s_ref) = refs
    else:
        (q_ref, k_ref, v_ref, z_ref, ba_ref, cw_ref, hp_ref, gn_ref, lvl_ref, cin_ref, sin_ref,
         o_ref, sout_ref, cout_ref, xe_ref, cb_ref) = refs
    i = pl.program_id(0)
    w = GDN_W
    if prompt:
        @pl.when(i == 0)
        def _():
            xe_ref[0:8, :] = jnp.zeros((8, 3 * w), F32)
            s_ref[...] = jnp.zeros(s_ref.shape, F32)
        xe_ref[8:8 + c, 0:w] = q_ref[...]
        xe_ref[8:8 + c, w:2 * w] = k_ref[...]
        xe_ref[8:8 + c, 2 * w:3 * w] = v_ref[...]
    else:
        xe_ref[0:8, :] = cin_ref[0]
        xe_ref[8:8 + c, 0:w] = q_ref[0]
        xe_ref[8:8 + c, w:2 * w] = k_ref[0]
        xe_ref[8:8 + c, 2 * w:3 * w] = v_ref[0]
    rowc = lax.broadcasted_iota(jnp.int32, (c, 512), 0)
    for cb in range(3 * w // 512):
        cs = slice(cb * 512, (cb + 1) * 512)
        acc = xe_ref[8:8 + c, cs] * cw_ref[CONV_W - 1:CONV_W, cs]
        for j in range(CONV_W - 1):
            acc = acc + xe_ref[8 - (CONV_W - 1) + j:8 - (CONV_W - 1) + j + c, cs] * cw_ref[j:j + 1, cs]
        y = _silu(acc)
        if l_valid < c:
            y = jnp.where(rowc < l_valid, y, 0.0)
        cb_ref[:, cs] = y
    new_conv = xe_ref[l_valid:l_valid + 8, :]
    if prompt:
        cout_ref[...] = new_conv
        xe_ref[0:8, :] = new_conv
    else:
        cout_ref[0] = new_conv

    ba = ba_ref[...] if prompt else ba_ref[0]
    row = lax.broadcasted_iota(jnp.int32, (c, LANE), 0)
    lane = lax.broadcasted_iota(jnp.int32, (c, LANE), 1)
    beta_all = _sigmoid(ba)
    xg = ba + hp_ref[1:2, :]
    softplus = jnp.maximum(xg, 0.0) + jnp.log1p(jnp.exp(-jnp.abs(xg)))
    g_all = -jnp.exp(hp_ref[0:1, :]) * softplus
    if l_valid < c:
        beta_all = jnp.where(row < l_valid, beta_all, 0.0)
        g_all = jnp.where(row < l_valid, g_all, 0.0)
    gcum = g_all
    s = 1
    while s < c:
        gcum = gcum + jnp.where(row >= s, pltpu.roll(gcum, s, 0), 0.0)
        s *= 2
    gn = gn_ref[...]

    def hbody(hh, carry):
        for uu in range(GDN_HU):
            h = hh * GDN_HU + uu
            off = pl.multiple_of(h * GDN_D, GDN_D)
            qh = cb_ref[:, pl.ds(off, GDN_D)]
            kh = cb_ref[:, pl.ds(pl.multiple_of(w + h * GDN_D, GDN_D), GDN_D)]
            vh = cb_ref[:, pl.ds(pl.multiple_of(2 * w + h * GDN_D, GDN_D), GDN_D)]
            qn = qh * lax.rsqrt(jnp.sum(qh * qh, axis=-1, keepdims=True) + EPS) * (GDN_D ** -0.5)
            kn = kh * lax.rsqrt(jnp.sum(kh * kh, axis=-1, keepdims=True) + EPS)
            bcol = jnp.sum(jnp.where(lane == h, beta_all, 0.0), axis=-1, keepdims=True)
            gcol = jnp.sum(jnp.where(lane == h + GDN_HEADS, gcum, 0.0), axis=-1, keepdims=True)
            s_prev = s_ref[h] if prompt else sin_ref[0, h]
            o, s_new = _gdn_head(qn, kn, vh, bcol, gcol, s_prev, lvl_ref, c)
            if prompt:
                s_ref[h] = s_new
            else:
                sout_ref[0, h] = s_new
            zh = z_ref[:, pl.ds(off, GDN_D)] if prompt else z_ref[0, :, pl.ds(off, GDN_D)]
            on = o * lax.rsqrt(jnp.mean(o * o, axis=-1, keepdims=True) + EPS) * gn
            res = (on * _silu(zh)).astype(o_ref.dtype)
            if prompt:
                o_ref[:, pl.ds(off, GDN_D)] = res
            else:
                o_ref[0, :, pl.ds(off, GDN_D)] = res
        return carry

    lax.fori_loop(0, GDN_HEADS // GDN_HU, hbody, 0)
    if prompt:
        @pl.when(i == n_chunks - 1)
        def _():
            sout_ref[...] = s_ref[...]


def _level_masks(c, l_valid):
    i = jnp.arange(c, dtype=jnp.int32)[:, None]
    j = jnp.arange(LANE, dtype=jnp.int32)[None, :]
    masks = []
    s = 1
    while s < l_valid:
        masks.append((i // (2 * s) == j // (2 * s)) & (i % (2 * s) >= s) & (j % (2 * s) < s))
        s *= 2
    return jnp.stack(masks).astype(F32)


def _gdn_prompt(proj, ba, conv_w, hp, gn, c=128):
    t = proj.shape[0]
    n_chunks = t // c
    w = GDN_W
    lvl = _level_masks(c, c)
    kern = functools.partial(_gdn_kernel, c=c, l_valid=c, prompt=True, n_chunks=n_chunks)
    return pl.pallas_call(
        kern,
        grid=(n_chunks,),
        in_specs=[pl.BlockSpec((c, w), lambda i: (i, 0)),
                  pl.BlockSpec((c, w), lambda i: (i, 1)),
                  pl.BlockSpec((c, w), lambda i: (i, 2)),
                  pl.BlockSpec((c, w), lambda i: (i, 3)),
                  pl.BlockSpec((c, LANE), lambda i: (i, 0)),
                  pl.BlockSpec((CONV_W, 3 * w), lambda i: (0, 0)),
                  pl.BlockSpec((8, LANE), lambda i: (0, 0)),
                  pl.BlockSpec((1, GDN_D), lambda i: (0, 0)),
                  pl.BlockSpec(lvl.shape, lambda i: (0, 0, 0))],
        out_specs=[pl.BlockSpec((c, w), lambda i: (i, 0)),
                   pl.BlockSpec((GDN_HEADS, GDN_D, GDN_D), lambda i: (0, 0, 0)),
                   pl.BlockSpec((8, 3 * w), lambda i: (0, 0))],
        out_shape=[jax.ShapeDtypeStruct((t, w), BF16),
                   jax.ShapeDtypeStruct((GDN_HEADS, GDN_D, GDN_D), F32),
                   jax.ShapeDtypeStruct((8, 3 * w), F32)],
        scratch_shapes=[pltpu.VMEM((8 + c, 3 * w), F32),
                        pltpu.VMEM((c, 3 * w), F32),
                        pltpu.VMEM((GDN_HEADS, GDN_D, GDN_D), F32)],
        compiler_params=_cparams(("arbitrary",)),
        name="gdn_prompt",
    )(proj, proj, proj, proj, ba, conv_w, hp, gn, lvl)


def _gdn_sample(proj3, ba3, conv_w, hp, gn, conv_in, s_in, l):
    b, c, _ = proj3.shape
    w = GDN_W
    lvl = _level_masks(c, l)
    kern = functools.partial(_gdn_kernel, c=c, l_valid=l, prompt=False, n_chunks=1)
    return pl.pallas_call(
        kern,
        grid=(b,),
        in_specs=[pl.BlockSpec((1, c, w), lambda i: (i, 0, 0)),
                  pl.BlockSpec((1, c, w), lambda i: (i, 0, 1)),
                  pl.BlockSpec((1, c, w), lambda i: (i, 0, 2)),
                  pl.BlockSpec((1, c, w), lambda i: (i, 0, 3)),
                  pl.BlockSpec((1, c, LANE), lambda i: (i, 0, 0)),
                  pl.BlockSpec((CONV_W, 3 * w), lambda i: (0, 0)),
                  pl.BlockSpec((8, LANE), lambda i: (0, 0)),
                  pl.BlockSpec((1, GDN_D), lambda i: (0, 0)),
                  pl.BlockSpec(lvl.shape, lambda i: (0, 0, 0)),
                  pl.BlockSpec((1, 8, 3 * w), lambda i: (i, 0, 0)),
                  pl.BlockSpec((1, GDN_HEADS, GDN_D, GDN_D), lambda i: (i, 0, 0, 0))],
        out_specs=[pl.BlockSpec((1, c, w), lambda i: (i, 0, 0)),
                   pl.BlockSpec((1, GDN_HEADS, GDN_D, GDN_D), lambda i: (i, 0, 0, 0)),
                   pl.BlockSpec((1, 8, 3 * w), lambda i: (i, 0, 0))],
        out_shape=[jax.ShapeDtypeStruct((b, c, w), F32),
                   jax.ShapeDtypeStruct((b, GDN_HEADS, GDN_D, GDN_D), F32),
                   jax.ShapeDtypeStruct((b, 8, 3 * w), F32)],
        scratch_shapes=[pltpu.VMEM((8 + c, 3 * w), F32),
                        pltpu.VMEM((c, 3 * w), F32)],
        compiler_params=_cparams(("arbitrary",)),
        name="gdn_sample",
    )(proj3, proj3, proj3, proj3, ba3, conv_w, hp, gn, lvl, conv_in, s_in)


def _t5_bucket(dist):
    d = jnp.maximum(dist, 0)
    max_exact = N_BUCKETS // 2
    large = max_exact + (jnp.log(jnp.maximum(d, 1).astype(F32) / max_exact)
                         / math.log(WINDOW / max_exact) * (N_BUCKETS - max_exact)).astype(jnp.int32)
    large = jnp.minimum(large, N_BUCKETS - 1)
    return jnp.where(d < max_exact, d, large)


def _bias_table(dist, rel_bias):
    valid = (dist >= 0) & (dist < WINDOW)
    b = rel_bias.astype(F32)[_t5_bucket(dist)]
    b = jnp.where(valid[:, :, None], b, NEG)
    return jnp.transpose(b, (2, 0, 1))


def _lo_hi(slab, g):
    lane = lax.broadcasted_iota(jnp.int32, slab.shape, 1)
    if g % 2 == 0:
        lo = jnp.where(lane < SWA_HD, slab, 0.0)
        hi = pltpu.roll(lo, SWA_HD, 1)
    else:
        hi = jnp.where(lane >= SWA_HD, slab, 0.0)
        lo = pltpu.roll(hi, SWA_HD, 1)
    return lo, hi


def _sink_softmax_parts(parts, sink):
    m = sink
    for p in parts:
        m = jnp.maximum(m, jnp.max(p, axis=-1, keepdims=True))
    es = [jnp.exp(p - m) for p in parts]
    den = jnp.exp(sink - m)
    for e in es:
        den = den + jnp.sum(e, axis=-1, keepdims=True)
    inv = 1.0 / den
    return [e * inv for e in es]


def _swa_prompt_kernel(q_ref, kp_ref, kc_ref, vp_ref, vc_ref, bias_ref, sink_ref, o_ref):
    n = pl.program_id(0)
    wq = WINDOW
    col = lax.broadcasted_iota(jnp.int32, (wq, 2 * wq), 1)
    first = jnp.logical_and(n == 0, col < wq)
    sink_all = sink_ref[...]
    for g in range(SWA_KV):
        sl = slice((g // 2) * LANE, (g // 2 + 1) * LANE)
        kslab = jnp.concatenate([kp_ref[:, sl], kc_ref[:, sl]], axis=0)
        vslab = jnp.concatenate([vp_ref[:, sl], vc_ref[:, sl]], axis=0)
        k_lo, k_hi = _lo_hi(kslab, g)
        v_lo, v_hi = _lo_hi(vslab, g)
        k_lo = k_lo.astype(BF16)
        k_hi = k_hi.astype(BF16)
        v_lo = v_lo.astype(BF16)
        v_hi = v_hi.astype(BF16)
        for s in range(2):
            qs = q_ref[:, (2 * g + s) * LANE:(2 * g + s + 1) * LANE].astype(BF16)
            acc = None
            for half, (kk, vv) in enumerate(((k_lo, v_lo), (k_hi, v_hi))):
                h = 4 * g + 2 * s + half
                logits = _dot_nt(qs, kk) * (SWA_HD ** -0.5) + bias_ref[h]
                logits = jnp.where(first, NEG, logits)
                (p,) = _sink_softmax_parts([logits], sink_all[:, h:h + 1])
                pv = _dot(p.astype(BF16), vv)
                acc = pv if acc is None else acc + pv
            o_ref[:, (2 * g + s) * LANE:(2 * g + s + 1) * LANE] = acc.astype(o_ref.dtype)


def _swa_prompt(proj, qcol, bias, sinks):
    t = proj.shape[0]
    nb = t // WINDOW
    kb = qcol * (SWA_W // SWA_KVW) + SWA_W // SWA_KVW
    return pl.pallas_call(
        _swa_prompt_kernel,
        grid=(nb,),
        in_specs=[pl.BlockSpec((WINDOW, SWA_W), lambda n: (n, qcol)),
                  pl.BlockSpec((WINDOW, SWA_KVW), lambda n: (jnp.maximum(n - 1, 0), kb)),
                  pl.BlockSpec((WINDOW, SWA_KVW), lambda n: (n, kb)),
                  pl.BlockSpec((WINDOW, SWA_KVW), lambda n: (jnp.maximum(n - 1, 0), kb + 1)),
                  pl.BlockSpec((WINDOW, SWA_KVW), lambda n: (n, kb + 1)),
                  pl.BlockSpec((SWA_HEADS, WINDOW, 2 * WINDOW), lambda n: (0, 0, 0)),
                  pl.BlockSpec((1, LANE), lambda n: (0, 0))],
        out_specs=pl.BlockSpec((WINDOW, SWA_W), lambda n: (n, 0)),
        out_shape=jax.ShapeDtypeStruct((t, SWA_W), BF16),
        compiler_params=_cparams(("arbitrary",)),
        name="swa_prompt",
    )(proj, proj, proj, proj, proj, bias, sinks)


def _swa_sample_kernel(q_ref, kn_ref, vn_ref, kc_ref, vc_ref, bc_ref, bn_ref, sink_ref, o_ref):
    r = 8
    sink_all = sink_ref[...]
    q = q_ref[0]
    knew = kn_ref[0]
    vnew = vn_ref[0]
    for g in range(SWA_KV):
        sl = slice((g // 2) * LANE, (g // 2 + 1) * LANE)
        kc_lo, kc_hi = _lo_hi(kc_ref[0, :, sl], g)
        vc_lo, vc_hi = _lo_hi(vc_ref[0, :, sl], g)
        kn_lo, kn_hi = _lo_hi(knew[:, sl], g)
        vn_lo, vn_hi = _lo_hi(vnew[:, sl], g)
        qs = jnp.concatenate([q[:, (2 * g) * LANE:(2 * g + 1) * LANE],
                              q[:, (2 * g + 1) * LANE:(2 * g + 2) * LANE]], axis=0).astype(BF16)
        acc = None
        for half, (kc, kn, vc, vn) in enumerate(((kc_lo, kn_lo, vc_lo, vn_lo), (kc_hi, kn_hi, vc_hi, vn_hi))):
            lc = _dot_nt(qs, kc.astype(BF16)) * (SWA_HD ** -0.5) + bc_ref[2 * g + half]
            ln = _dot_nt(qs, _rowpad(kn, LANE).astype(BF16)) * (SWA_HD ** -0.5) + bn_ref[2 * g + half]
            h0 = 4 * g + half
            h1 = 4 * g + 2 + half
            sink = jnp.concatenate([jnp.broadcast_to(sink_all[:, h0:h0 + 1], (r, 1)),
                                    jnp.broadcast_to(sink_all[:, h1:h1 + 1], (r, 1))], axis=0)
            pc, pn = _sink_softmax_parts([lc, ln], sink)
            pv = _dot(pc.astype(BF16), vc.astype(BF16)) + _dot(pn.astype(BF16), _rowpad(vn, LANE).astype(BF16))
            acc = pv if acc is None else acc + pv
        o_ref[0, :, (2 * g) * LANE:(2 * g + 1) * LANE] = acc[:r]
        o_ref[0, :, (2 * g + 1) * LANE:(2 * g + 2) * LANE] = acc[r:]


def _swa_sample(proj3, qcol, k_cache, v_cache, bias_c, bias_n, sinks):
    b, r, _ = proj3.shape
    kb = qcol * (SWA_W // SWA_KVW) + SWA_W // SWA_KVW
    return pl.pallas_call(
        _swa_sample_kernel,
        grid=(b,),
        in_specs=[pl.BlockSpec((1, r, SWA_W), lambda i: (i, 0, qcol)),
                  pl.BlockSpec((1, r, SWA_KVW), lambda i: (i, 0, kb)),
                  pl.BlockSpec((1, r, SWA_KVW), lambda i: (i, 0, kb + 1)),
                  pl.BlockSpec((1, WINDOW, SWA_KVW), lambda i: (i, 0, 0)),
                  pl.BlockSpec((1, WINDOW, SWA_KVW), lambda i: (i, 0, 0)),
                  pl.BlockSpec((2 * SWA_KV, 2 * r, LANE), lambda i: (0, 0, 0)),
                  pl.BlockSpec((2 * SWA_KV, 2 * r, LANE), lambda i: (0, 0, 0)),
                  pl.BlockSpec((1, LANE), lambda i: (0, 0))],
        out_specs=pl.BlockSpec((1, r, SWA_W), lambda i: (i, 0, 0)),
        out_shape=jax.ShapeDtypeStruct((b, r, SWA_W), F32),
        compiler_params=_cparams(("arbitrary",)),
        name="swa_sample",
    )(proj3, proj3, proj3, k_cache, v_cache, bias_c, bias_n, sinks)


def _moe_kernel(be_ref, xb_ref, ns_ref, x_ref, wg_ref, wu_ref, wd_ref, o_ref, acc_ref, wg_s, wu_s, wd_s):
    k = pl.program_id(0)
    c = pl.program_id(1)
    ns = ns_ref[k]

    @pl.when(c == 0)
    def _():
        acc_ref[...] = jnp.zeros(acc_ref.shape, F32)

    @pl.when(ns > 0)
    def _():
        wg_s[...] = wg_ref[0].astype(BF16)
        wu_s[...] = wu_ref[0].astype(BF16)
        wd_s[...] = wd_ref[0].astype(BF16)

        def sub(s, carry):
            r = pl.multiple_of(s * MOE_SUB, MOE_SUB)
            xs = x_ref[pl.ds(r, MOE_SUB), :]
            hg = _dot(xs, wg_s[...])
            hu = _dot(xs, wu_s[...])
            hid = (_silu(hg) * hu).astype(BF16)
            acc_ref[pl.ds(r, MOE_SUB), :] += _dot(hid, wd_s[...])
            return carry

        lax.fori_loop(0, ns, sub, 0)

    @pl.when(c == MOE_NC - 1)
    def _():
        o_ref[...] = acc_ref[...].astype(o_ref.dtype)


def _moe_ffn(block_e, x_blk, n_sub, xs, w_gate, w_up, w_down):
    p, d = xs.shape
    nsb = p // MOE_ROWS
    ce = D_EXPERT // MOE_NC
    grid_spec = pltpu.PrefetchScalarGridSpec(
        num_scalar_prefetch=3,
        grid=(nsb, MOE_NC),
        in_specs=[pl.BlockSpec((MOE_ROWS, d), lambda k, c, be, xb, ns: (xb[k], 0)),
                  pl.BlockSpec((1, d, ce), lambda k, c, be, xb, ns: (be[k], 0, c)),
                  pl.BlockSpec((1, d, ce), lambda k, c, be, xb, ns: (be[k], 0, c)),
                  pl.BlockSpec((1, ce, d), lambda k, c, be, xb, ns: (be[k], c, 0))],
        out_specs=pl.BlockSpec((MOE_ROWS, d), lambda k, c, be, xb, ns: (k, 0)),
        scratch_shapes=[pltpu.VMEM((MOE_ROWS, d), F32),
                        pltpu.VMEM((d, ce), BF16),
                        pltpu.VMEM((d, ce), BF16),
                        pltpu.VMEM((ce, d), BF16)],
    )
    return pl.pallas_call(
        _moe_kernel,
        grid_spec=grid_spec,
        out_shape=jax.ShapeDtypeStruct((p, d), BF16),
        compiler_params=_cparams(("arbitrary", "arbitrary")),
        name="moe_ffn",
    )(block_e, x_blk, n_sub, xs, w_gate, w_up, w_down)


def _final_kernel(x_ref, y0_ref, y1_ref, p_ref, g_ref, nf_ref, o_ref):
    p = p_ref[...]
    moe = y0_ref[...].astype(F32) * p[:, 0:1] + y1_ref[...].astype(F32) * p[:, 1:2]
    x = x_ref[...] + g_ref[...] * moe
    o_ref[...] = x * lax.rsqrt(jnp.mean(x * x, axis=-1, keepdims=True) + EPS) * nf_ref[...]


def _final(x, y0, y1, gates, gt2, nf, tm=256):
    t, d = x.shape
    tm = min(tm, t)
    return pl.pallas_call(
        _final_kernel,
        grid=(t // tm,),
        in_specs=[pl.BlockSpec((tm, d), lambda i: (i, 0)),
                  pl.BlockSpec((tm, d), lambda i: (i, 0)),
                  pl.BlockSpec((tm, d), lambda i: (i, 0)),
                  pl.BlockSpec((tm, LANE), lambda i: (i, 0)),
                  _mod_spec(gt2, tm, d),
                  pl.BlockSpec((1, d), lambda i: (0, 0))],
        out_specs=pl.BlockSpec((tm, d), lambda i: (i, 0)),
        out_shape=jax.ShapeDtypeStruct((t, d), F32),
        compiler_params=_cparams(("arbitrary",)),
        name="final",
    )(x, y0, y1, gates, gt2, nf.reshape(1, d))


def _dispatch(expert_idx, n_blocks):
    t, k = expert_idx.shape
    a = t * k
    flat_e = expert_idx.reshape(a)
    order = jnp.argsort(flat_e, stable=True).astype(jnp.int32)
    sorted_e = flat_e[order]
    counts = jnp.zeros((N_EXPERTS,), jnp.int32).at[flat_e].add(1)
    padded = (counts + MOE_ROWS - 1) // MOE_ROWS * MOE_ROWS
    pad_end = jnp.cumsum(padded)
    pad_start = pad_end - padded
    start = jnp.cumsum(counts) - counts
    dest = pad_start[sorted_e] + jnp.arange(a, dtype=jnp.int32) - start[sorted_e]
    p = n_blocks * MOE_ROWS
    row_tok = jnp.zeros((p,), jnp.int32).at[dest].set(order // k)
    pos = jnp.zeros((a,), jnp.int32).at[order].set(dest).reshape(t, k)
    blk_start = jnp.arange(n_blocks, dtype=jnp.int32) * MOE_ROWS
    block_e = jnp.minimum(jnp.searchsorted(pad_end, blk_start, side='right'), N_EXPERTS - 1).astype(jnp.int32)
    n_valid = jnp.clip(counts[block_e] - (blk_start - pad_start[block_e]), 0, MOE_ROWS)
    n_valid = jnp.where(blk_start < pad_end[-1], n_valid, 0)
    n_sub = ((n_valid + MOE_SUB - 1) // MOE_SUB).astype(jnp.int32)
    n_active = jnp.maximum(pad_end[-1] // MOE_ROWS, 1)
    x_blk = jnp.minimum(jnp.arange(n_blocks, dtype=jnp.int32), n_active - 1).astype(jnp.int32)
    return row_tok, pos, block_e, x_blk, n_sub


def kernel(x_prompt, x_sample, state_gdn, state_conv, cache_swa_k, cache_swa_v, c_prompt, c_sample, w_ada, b_ada, norm_mix, w_in, conv_w, a_log, dt_bias, gdn_norm, swa_sinks, rel_bias, w_out, norm_moe, router_group, router_group_bias, router_expert, router_expert_bias, w_gate, w_up, w_down, norm_final):
    depth = w_ada.shape[0]
    assert depth == 1
    bp, seq, d = x_prompt.shape
    assert bp == 1 and seq % WINDOW == 0
    bs, ls, _ = x_sample.shape
    tp = bp * seq
    ts = bs * ls
    n_main = 2 * GDN_W + 2 * GDN_W
    n_ba = 2 * GDN_HEADS

    c_all = jnp.concatenate([c_prompt, c_sample], axis=0)
    m_pad = -(-c_all.shape[0] // 16) * 16
    c_all = jnp.pad(c_all, ((0, m_pad - c_all.shape[0]), (0, 0)))
    mod = _adaln(c_all, w_ada[0], b_ada[0])
    mod_p = [mod[0:1, j * d:(j + 1) * d] for j in range(6)]
    mod_s = [jnp.repeat(mod[1:1 + bs, j * d:(j + 1) * d], ls, axis=0) for j in range(6)]

    w_cat = jnp.concatenate([w_in[0][:, :n_main], w_in[0][:, n_main + n_ba:]], axis=1).astype(BF16)
    w_ba = jnp.pad(w_in[0][:, n_main:n_main + n_ba], ((0, 0), (0, LANE - n_ba))).astype(BF16)
    w_out16 = w_out[0].astype(BF16)
    qcol = n_main // SWA_W
    hp = jnp.zeros((8, LANE), F32)
    hp = hp.at[0, GDN_HEADS:2 * GDN_HEADS].set(a_log[0]).at[1, GDN_HEADS:2 * GDN_HEADS].set(dt_bias[0])
    gn = gdn_norm[0].reshape(1, GDN_D)
    sinks = jnp.pad(swa_sinks[0], (0, LANE - SWA_HEADS)).reshape(1, LANE)

    xp = x_prompt.reshape(tp, d)
    h_p = _norm_mod(xp, norm_mix[0], mod_p[1], mod_p[0])
    proj_p = _matmul(h_p, w_cat, 1024, 512, name="inproj_p")
    ba_p = _matmul(h_p, w_ba, 1024, LANE, name="inproj_ba_p")
    og_p, s_p, conv_p = _gdn_prompt(proj_p, ba_p, conv_w[0], hp, gn)
    conv_p = conv_p[8 - (CONV_W - 1):]
    qi = jnp.arange(WINDOW, dtype=jnp.int32)
    sj = jnp.arange(2 * WINDOW, dtype=jnp.int32)
    bias_p = _bias_table((WINDOW + qi)[:, None] - sj[None, :], rel_bias)
    os_p = _swa_prompt(proj_p, qcol, bias_p, sinks)
    kcol = n_main + SWA_W
    k_p = proj_p[tp - WINDOW:, kcol:kcol + SWA_KVW]
    v_p = proj_p[tp - WINDOW:, kcol + SWA_KVW:kcol + 2 * SWA_KVW]
    x1_p = _outproj(og_p, os_p, w_out16, xp, mod_p[2])

    xs = x_sample.reshape(ts, d)
    h_s = _norm_mod(xs, norm_mix[0], mod_s[1], mod_s[0])
    proj_s = _matmul(h_s, w_cat, 512, 512, name="inproj_s")
    ba_s = _matmul(h_s, w_ba, 512, LANE, name="inproj_ba_s")
    rs = 8
    assert CONV_W - 1 <= ls <= rs
    proj_s3 = proj_s.reshape(bs, ls, proj_s.shape[1])
    proj_s8 = jnp.pad(proj_s3, ((0, 0), (0, rs - ls), (0, 0)))
    ba_s8 = jnp.pad(ba_s.reshape(bs, ls, LANE), ((0, 0), (0, rs - ls), (0, 0)))
    conv_in8 = jnp.pad(state_conv[0], ((0, 0), (8 - (CONV_W - 1), 0), (0, 0)))
    og_s, s_s, conv_s = _gdn_sample(proj_s8, ba_s8, conv_w[0], hp, gn, conv_in8, state_gdn[0], ls)
    conv_s = conv_s[:, 8 - (CONV_W - 1):]
    wb = cache_swa_k.shape[2]
    assert wb == WINDOW == LANE
    dist_s = (wb + jnp.arange(ls, dtype=jnp.int32))[:, None] - jnp.arange(wb + ls, dtype=jnp.int32)[None, :]
    bias_s = _bias_table(dist_s, rel_bias)
    bias_s = jnp.pad(bias_s, ((0, 0), (0, rs - ls), (0, 2 * LANE - wb - ls)), constant_values=NEG)
    bias_s = bias_s.reshape(SWA_KV, 2, 2, rs, 2 * LANE)
    bias_s = jnp.transpose(bias_s, (0, 2, 1, 3, 4)).reshape(2 * SWA_KV, 2 * rs, 2 * LANE)
    k_cache = cache_swa_k[0].reshape(bs, wb, SWA_KVW)
    v_cache = cache_swa_v[0].reshape(bs, wb, SWA_KVW)
    os_s = _swa_sample(proj_s8, qcol, k_cache, v_cache, bias_s[:, :, :LANE], bias_s[:, :, LANE:], sinks)
    k_s = jnp.concatenate([k_cache[:, ls:], proj_s3[:, :, kcol:kcol + SWA_KVW]], axis=1)
    v_s = jnp.concatenate([v_cache[:, ls:], proj_s3[:, :, kcol + SWA_KVW:kcol + 2 * SWA_KVW]], axis=1)
    x1_s = _outproj(og_s[:, :ls].reshape(ts, GDN_W).astype(BF16), os_s[:, :ls].reshape(ts, SWA_W).astype(BF16),
                    w_out16, xs, mod_s[2], tm=512)

    wr = jnp.concatenate([router_group[0], router_expert[0]], axis=1)
    wr = jnp.pad(wr, ((0, 0), (0, LANE - wr.shape[1])))
    wr_hi = wr.astype(BF16)
    wr_mid = (wr - wr_hi.astype(F32)).astype(BF16)
    wr2 = jnp.stack([wr_hi, wr_mid])
    br = jnp.pad(jnp.concatenate([router_group_bias[0], router_expert_bias[0]]),
                 (0, LANE - N_GROUPS - N_EXPERTS)).reshape(1, LANE)
    h2_p, e_p, g_p = _norm_router(x1_p, norm_moe[0], mod_p[4], mod_p[3], wr2, br)
    h2_s, e_s, g_s = _norm_router(x1_s, norm_moe[0], mod_s[4], mod_s[3], wr2, br)
    h2 = jnp.concatenate([h2_p, h2_s], axis=0)
    eidx = jnp.concatenate([e_p[:, :2], e_s[:, :2]], axis=0)
    gates = jnp.concatenate([g_p, g_s], axis=0)
    t_all = tp + ts
    n_blocks = -(-(t_all * 2) // MOE_ROWS) + N_EXPERTS
    row_tok, pos, block_e, x_blk, n_sub = _dispatch(eidx, n_blocks)
    x_sorted = h2[row_tok]
    y_sorted = _moe_ffn(block_e, x_blk, n_sub, x_sorted, w_gate[0], w_up[0], w_down[0])
    y0 = y_sorted[pos[:, 0]]
    y1 = y_sorted[pos[:, 1]]
    y_p = _final(x1_p, y0[:tp], y1[:tp], gates[:tp], mod_p[5], norm_final)
    y_s = _final(x1_s, y0[tp:], y1[tp:], gates[tp:], mod_s[5], norm_final)

    sdt = state_gdn.dtype
    return (y_p.reshape(bp, seq, d), y_s.reshape(bs, ls, d),
            s_p.reshape(1, bp, GDN_HEADS, GDN_D, GDN_D).astype(sdt), s_s[None].astype(sdt),
            conv_p.reshape(1, bp, CONV_W - 1, 3 * GDN_W).astype(state_conv.dtype), conv_s[None].astype(state_conv.dtype),
            k_p.reshape(1, bp, WINDOW, SWA_KV, SWA_HD).astype(cache_swa_k.dtype),
            k_s.reshape(1, bs, wb, SWA_KV, SWA_HD).astype(cache_swa_k.dtype),
            v_p.reshape(1, bp, WINDOW, SWA_KV, SWA_HD).astype(cache_swa_v.dtype),
            v_s.reshape(1, bs, wb, SWA_KV, SWA_HD).astype(cache_swa_v.dtype))
```

```python
import functools
import math

import jax
import jax.numpy as jnp
from jax import lax
from jax.experimental import pallas as pl
from jax.experimental.pallas import tpu as pltpu

F32 = jnp.float32
BF16 = jnp.bfloat16
EPS = 1e-6
NEG = -1e30

LANE = 128
GDN_HEADS = 16
GDN_D = 128
GDN_W = GDN_HEADS * GDN_D
CONV_W = 4
SWA_HEADS = 32
SWA_KV = 8
SWA_HD = 64
SWA_W = SWA_HEADS * SWA_HD
SWA_KVW = SWA_KV * SWA_HD
WINDOW = 128
N_BUCKETS = 32
N_GROUPS = 8
EPG = 8
N_EXPERTS = 64
D_EXPERT = 1024
MOE_ROWS = 384
MOE_SUB = 128
MOE_NC = 4
VMEM_LIMIT = 56 * 1024 * 1024


def _cparams(sem):
    return pltpu.CompilerParams(dimension_semantics=sem, vmem_limit_bytes=VMEM_LIMIT)


def _sigmoid(x):
    return 1.0 / (1.0 + jnp.exp(-x))


def _silu(x):
    return x * _sigmoid(x)


def _dot(a, b):
    return jnp.dot(a, b, preferred_element_type=F32)


def _dot_nt(a, b):
    return lax.dot_general(a, b, (((1,), (1,)), ((), ())), preferred_element_type=F32)


def _dot_tn(a, b):
    return lax.dot_general(a, b, (((0,), (0,)), ((), ())), preferred_element_type=F32)


def _ada_kernel(c_ref, w_ref, b_ref, o_ref):
    a = _silu(c_ref[...]).astype(BF16)
    o_ref[...] = _dot(a, w_ref[...].astype(BF16)) + b_ref[...]


def _adaln(c, w, b, tn=512):
    m, d = c.shape
    n = w.shape[1]
    return pl.pallas_call(
        _ada_kernel,
        grid=(n // tn,),
        in_specs=[pl.BlockSpec((m, d), lambda j: (0, 0)),
                  pl.BlockSpec((d, tn), lambda j: (0, j)),
                  pl.BlockSpec((1, tn), lambda j: (0, j))],
        out_specs=pl.BlockSpec((m, tn), lambda j: (0, j)),
        out_shape=jax.ShapeDtypeStruct((m, n), F32),
        compiler_params=_cparams(("arbitrary",)),
        name="adaln",
    )(c, w, b.reshape(1, n))


def _norm_kernel(x_ref, g_ref, sc_ref, sh_ref, o_ref):
    x = x_ref[...]
    y = x * lax.rsqrt(jnp.mean(x * x, axis=-1, keepdims=True) + EPS) * g_ref[...]
    o_ref[...] = (y * (1.0 + sc_ref[...]) + sh_ref[...]).astype(o_ref.dtype)


def _mod_spec(mod, tm, d):
    if mod.shape[0] == 1:
        return pl.BlockSpec((1, d), lambda i: (0, 0))
    return pl.BlockSpec((tm, d), lambda i: (i, 0))


def _norm_mod(x, g, sc, sh, tm=256):
    t, d = x.shape
    tm = min(tm, t)
    return pl.pallas_call(
        _norm_kernel,
        grid=(t // tm,),
        in_specs=[pl.BlockSpec((tm, d), lambda i: (i, 0)),
                  pl.BlockSpec((1, d), lambda i: (0, 0)),
                  _mod_spec(sc, tm, d), _mod_spec(sh, tm, d)],
        out_specs=pl.BlockSpec((tm, d), lambda i: (i, 0)),
        out_shape=jax.ShapeDtypeStruct((t, d), BF16),
        compiler_params=_cparams(("arbitrary",)),
        name="norm_mod",
    )(x, g.reshape(1, d), sc, sh)


def _split3(x):
    hi = x.astype(BF16)
    r1 = x - hi.astype(F32)
    mid = r1.astype(BF16)
    lo = (r1 - mid.astype(F32)).astype(BF16)
    return hi, mid, lo


def _norm_router_kernel(*refs, nba):
    (xa_ref, xb_ref, g_ref, sca_ref, sha_ref, scb_ref, shb_ref, wr_ref, br_ref,
     o_ref, e_ref, p_ref) = refs
    first = pl.program_id(0) < nba
    x = jnp.where(first, xa_ref[...], xb_ref[...])
    sc = jnp.where(first, sca_ref[...], scb_ref[...])
    sh = jnp.where(first, sha_ref[...], shb_ref[...])
    y = x * lax.rsqrt(jnp.mean(x * x, axis=-1, keepdims=True) + EPS) * g_ref[...]
    t = y * (1.0 + sc) + sh
    t_hi, t_mid, _ = _split3(t)
    half = t.shape[1] // 2
    bits = lax.bitcast_convert_type(t_hi.astype(F32), jnp.uint32)
    o_ref[...] = lax.shift_right_logical(bits[:, :half], jnp.uint32(16)) | (bits[:, half:] & jnp.uint32(0xFFFF0000))
    w_hi = wr_ref[0]
    w_mid = wr_ref[1]
    lg = _dot(t_hi, w_hi) + (_dot(t_mid, w_hi) + _dot(t_hi, w_mid)) + br_ref[...]
    lane = lax.broadcasted_iota(jnp.int32, lg.shape, 1)
    big = jnp.int32(4 * LANE)
    lgrp = jnp.where(lane < N_GROUPS, lg, NEG)
    mg = jnp.max(lgrp, axis=-1, keepdims=True)
    grp = jnp.min(jnp.where(lgrp == mg, lane, big), axis=-1, keepdims=True)
    p_grp = 1.0 / jnp.sum(jnp.exp(lgrp - mg), axis=-1, keepdims=True)
    lo = N_GROUPS + grp * EPG
    emask = jnp.logical_and(lane >= lo, lane < lo + EPG)
    le = jnp.where(emask, lg, NEG)
    m1 = jnp.max(le, axis=-1, keepdims=True)
    i1 = jnp.min(jnp.where(le == m1, lane, big), axis=-1, keepdims=True)
    le2 = jnp.where(lane == i1, NEG, le)
    m2 = jnp.max(le2, axis=-1, keepdims=True)
    i2 = jnp.min(jnp.where(le2 == m2, lane, big), axis=-1, keepdims=True)
    e2 = jnp.exp(m2 - m1)
    w1 = p_grp / (1.0 + e2)
    w2 = p_grp * e2 / (1.0 + e2)
    e_ref[...] = jnp.where(lane == 0, i1 - N_GROUPS, jnp.where(lane == 1, i2 - N_GROUPS, 0))
    p_ref[...] = jnp.where(lane == 0, w1, jnp.where(lane == 1, w2, 0.0))


def _norm_router(xa, xb, g, sca, sha, scb, shb, wr, br, tm=256):
    ta, d = xa.shape
    tb = xb.shape[0]
    tm = min(tm, math.gcd(ta, tb))
    nba = ta // tm
    nbb = tb // tm
    t_all = ta + tb
    spec_a = pl.BlockSpec((tm, d), lambda i: (jnp.minimum(i, nba - 1), 0))
    spec_b = pl.BlockSpec((tm, d), lambda i: (jnp.maximum(i - nba, 0), 0))
    row = pl.BlockSpec((1, d), lambda i: (0, 0))
    return pl.pallas_call(
        functools.partial(_norm_router_kernel, nba=nba),
        grid=(nba + nbb,),
        in_specs=[spec_a, spec_b, row, row, row, spec_b, spec_b,
                  pl.BlockSpec((2, d, LANE), lambda i: (0, 0, 0)),
                  pl.BlockSpec((1, LANE), lambda i: (0, 0))],
        out_specs=[pl.BlockSpec((tm, d // 2), lambda i: (i, 0)),
                   pl.BlockSpec((tm, LANE), lambda i: (i, 0)),
                   pl.BlockSpec((tm, LANE), lambda i: (i, 0))],
        out_shape=[jax.ShapeDtypeStruct((t_all, d // 2), jnp.uint32),
                   jax.ShapeDtypeStruct((t_all, LANE), jnp.int32),
                   jax.ShapeDtypeStruct((t_all, LANE), F32)],
        compiler_params=_cparams(("arbitrary",)),
        name="norm_router",
    )(xa, xb, g.reshape(1, d), sca, sha, scb, shb, wr, br)


def _mm_kernel(x_ref, w_ref, o_ref):
    o_ref[...] = _dot(x_ref[...], w_ref[...]).astype(o_ref.dtype)


def _matmul(x, w, tm, tn, out_dtype=F32, name="matmul"):
    m, k = x.shape
    n = w.shape[1]
    tm = min(tm, m)
    tn = min(tn, n)
    return pl.pallas_call(
        _mm_kernel,
        grid=(m // tm, n // tn),
        in_specs=[pl.BlockSpec((tm, k), lambda i, j: (i, 0)),
                  pl.BlockSpec((k, tn), lambda i, j: (0, j))],
        out_specs=pl.BlockSpec((tm, tn), lambda i, j: (i, j)),
        out_shape=jax.ShapeDtypeStruct((m, n), out_dtype),
        compiler_params=_cparams(("arbitrary", "arbitrary")),
        name=name,
    )(x, w)


def _outproj_kernel(a_ref, b_ref, wa_ref, wb_ref, x_ref, g_ref, o_ref):
    mix = _dot(a_ref[...], wa_ref[...]) + _dot(b_ref[...], wb_ref[...])
    o_ref[...] = x_ref[...] + g_ref[...] * mix


def _outproj(a, b, w, x, gate, tm=1024, tn=512):
    m, ka = a.shape
    kb = b.shape[1]
    n = w.shape[1]
    tm = min(tm, m)
    if gate.shape[0] == 1:
        gspec = pl.BlockSpec((1, tn), lambda i, j: (0, j))
    else:
        gspec = pl.BlockSpec((tm, tn), lambda i, j: (i, j))
    return pl.pallas_call(
        _outproj_kernel,
        grid=(m // tm, n // tn),
        in_specs=[pl.BlockSpec((tm, ka), lambda i, j: (i, 0)),
                  pl.BlockSpec((tm, kb), lambda i, j: (i, 0)),
                  pl.BlockSpec((ka, tn), lambda i, j: (0, j)),
                  pl.BlockSpec((kb, tn), lambda i, j: (1, j)),
                  pl.BlockSpec((tm, tn), lambda i, j: (i, j)),
                  gspec],
        out_specs=pl.BlockSpec((tm, tn), lambda i, j: (i, j)),
        out_shape=jax.ShapeDtypeStruct((m, n), F32),
        compiler_params=_cparams(("arbitrary", "arbitrary")),
        name="outproj",
    )(a, b, w, w, x, gate)


GDN_GROUP_PROMPT = 8
GDN_GROUP_SAMPLE = 16


def _rowpad(a, rows):
    if a.shape[0] == rows:
        return a
    return jnp.concatenate([a, jnp.zeros((rows - a.shape[0], a.shape[1]), a.dtype)], axis=0)


def _gdn_heads(qn, kn, vh, bcol, gcol, s_prev, lvl_ref, c):
    cp = LANE
    n = len(qn)
    hs = range(n)
    lane = lax.broadcasted_iota(jnp.int32, (c, cp), 1)
    row = lax.broadcasted_iota(jnp.int32, (c, cp), 0)
    tri = lane <= row

    def b16(x):
        return x.astype(BF16)

    g_hi = [b16(g).astype(F32) for g in gcol]
    r1 = [gcol[h] - g_hi[h] for h in hs]
    g_mid = [b16(r).astype(F32) for r in r1]
    g_lo = [r1[h] - g_mid[h] for h in hs]
    a_mat = [b16(jnp.where(lane == 0, g_hi[h], jnp.where(lane == 1, g_mid[h], jnp.where(lane == 2, g_lo[h],
                 jnp.where(lane < 6, 1.0, 0.0))))) for h in hs]
    b_mat = [b16(_rowpad(jnp.where(lane < 3, 1.0, jnp.where(lane == 3, -g_hi[h], jnp.where(lane == 4, -g_mid[h],
                 jnp.where(lane == 5, -g_lo[h], 0.0)))), cp)) for h in hs]
    kn16 = [b16(k) for k in kn]
    qkn16 = [jnp.concatenate([b16(qn[h]), kn16[h]], axis=0) for h in hs]
    knp16 = [_rowpad(k, cp) for k in kn16]
    diff = [_dot_nt(a_mat[h], b_mat[h]) for h in hs]
    qk_kk = [_dot_nt(qkn16[h], knp16[h]) for h in hs]
    decay = [jnp.where(tri, jnp.exp(jnp.minimum(d, 0.0)), 0.0) for d in diff]
    eg = [jnp.exp(g) for g in gcol]
    lmat = [(bcol[h] * qk_kk[h][c:]) * decay[h] for h in hs]
    qkd16 = [b16(qk_kk[h][:c] * decay[h]) for h in hs]
    n_mat = [-(lm * lvl_ref[0]) for lm in lmat]
    for lv in range(1, lvl_ref.shape[0]):
        bm = [lm * lvl_ref[lv] for lm in lmat]
        n16 = [b16(x) for x in n_mat]
        w_mat = [bm[h] + _dot(n16[h], _rowpad(b16(bm[h]), cp)) for h in hs]
        n_mat = [n_mat[h] - w_mat[h] - _dot(b16(w_mat[h]), _rowpad(n16[h], cp)) for h in hs]
    rhs = [jnp.concatenate([vh[h] * bcol[h], kn[h] * (bcol[h] * eg[h])], axis=1) for h in hs]
    sol = [rhs[h] + _dot(b16(n_mat[h]), _rowpad(b16(rhs[h]), cp)) for h in hs]
    kq16 = [jnp.concatenate([b16(sol[h][:, GDN_D:]), b16(qn[h] * eg[h])], axis=0) for h in hs]
    kq_s = [_dot(kq16[h], b16(s_prev[h])) for h in hs]
    u = [sol[h][:, :GDN_D] - kq_s[h][:c] for h in hs]
    u16 = [_rowpad(b16(x), cp) for x in u]
    g_last = [g[c - 1:c, :] for g in gcol]
    kd16 = [_rowpad(b16(kn[h] * jnp.exp(g_last[h] - gcol[h])), cp) for h in hs]
    o = [kq_s[h][c:] + _dot(qkd16[h], u16[h]) for h in hs]
    s_new = [s_prev[h] * jnp.exp(g_last[h]) + _dot_tn(kd16[h], u16[h]) for h in hs]
    return o, s_new


def _gdn_kernel(*refs, c, l_valid, prompt, n_chunks, group):
    if prompt:
        (q_ref, k_ref, v_ref, z_ref, ba_ref, cw_ref, hp_ref, gn_ref, lvl_ref,
         o_ref, sout_ref, cout_ref, xe_ref, cb_ref, s_ref) = refs
    else:
        (q_ref, k_ref, v_ref, z_ref, ba_ref, cw_ref, hp_ref, gn_ref, lvl_ref, cin_ref, sin_ref,
         o_ref, sout_ref, cout_ref, xe_ref, cb_ref) = refs
    i = pl.program_id(0)
    w = GDN_W
    if prompt:
        @pl.when(i == 0)
        def _():
            xe_ref[0:8, :] = jnp.zeros((8, 3 * w), F32)
            s_ref[...] = jnp.zeros(s_ref.shape, F32)
        xe_ref[8:8 + c, 0:w] = q_ref[...]
        xe_ref[8:8 + c, w:2 * w] = k_ref[...]
        xe_ref[8:8 + c, 2 * w:3 * w] = v_ref[...]
    else:
        xe_ref[0:8, :] = cin_ref[0]
        xe_ref[8:8 + c, 0:w] = q_ref[0]
        xe_ref[8:8 + c, w:2 * w] = k_ref[0]
        xe_ref[8:8 + c, 2 * w:3 * w] = v_ref[0]
    rowc = lax.broadcasted_iota(jnp.int32, (c, 512), 0)
    for cb in range(3 * w // 512):
        cs = slice(cb * 512, (cb + 1) * 512)
        acc = xe_ref[8:8 + c, cs] * cw_ref[CONV_W - 1:CONV_W, cs]
        for j in range(CONV_W - 1):
            acc = acc + xe_ref[8 - (CONV_W - 1) + j:8 - (CONV_W - 1) + j + c, cs] * cw_ref[j:j + 1, cs]
        y = _silu(acc)
        if l_valid < c:
            y = jnp.where(rowc < l_valid, y, 0.0)
        cb_ref[:, cs] = y
    new_conv = xe_ref[l_valid:l_valid + 8, :]
    if prompt:
        cout_ref[...] = new_conv
        xe_ref[0:8, :] = new_conv
    else:
        cout_ref[0] = new_conv

    ba = ba_ref[...] if prompt else ba_ref[0]
    row = lax.broadcasted_iota(jnp.int32, (c, LANE), 0)
    beta_all = _sigmoid(ba)
    xg = ba + hp_ref[1:2, :]
    softplus = jnp.maximum(xg, 0.0) + jnp.log1p(jnp.exp(-jnp.abs(xg)))
    g_all = -jnp.exp(hp_ref[0:1, :]) * softplus
    if l_valid < c:
        beta_all = jnp.where(row < l_valid, beta_all, 0.0)
        g_all = jnp.where(row < l_valid, g_all, 0.0)
    gcum = g_all
    s = 1
    while s < c:
        gcum = gcum + jnp.where(row >= s, pltpu.roll(gcum, s, 0), 0.0)
        s *= 2
    gn = gn_ref[...]

    hg = group
    for h0 in range(0, GDN_HEADS, hg):
        hs = range(h0, h0 + hg)
        cols = [slice(h * GDN_D, (h + 1) * GDN_D) for h in hs]
        qh = [cb_ref[:, cl] for cl in cols]
        kh = [cb_ref[:, w + h * GDN_D:w + (h + 1) * GDN_D] for h in hs]
        vh = [cb_ref[:, 2 * w + h * GDN_D:2 * w + (h + 1) * GDN_D] for h in hs]
        qn = [x * lax.rsqrt(jnp.sum(x * x, axis=-1, keepdims=True) + EPS) * (GDN_D ** -0.5) for x in qh]
        kn = [x * lax.rsqrt(jnp.sum(x * x, axis=-1, keepdims=True) + EPS) for x in kh]
        bcol = [beta_all[:, h:h + 1] for h in hs]
        gcol = [gcum[:, h + GDN_HEADS:h + GDN_HEADS + 1] for h in hs]
        s_prev = [s_ref[h] if prompt else sin_ref[0, h] for h in hs]
        o, s_new = _gdn_heads(qn, kn, vh, bcol, gcol, s_prev, lvl_ref, c)
        for j, h in enumerate(hs):
            if prompt:
                s_ref[h] = s_new[j]
            else:
                sout_ref[0, h] = s_new[j]
        zh = [z_ref[:, cl] if prompt else z_ref[0, :, cl] for cl in cols]
        on = [x * lax.rsqrt(jnp.mean(x * x, axis=-1, keepdims=True) + EPS) * gn for x in o]
        for j, cl in enumerate(cols):
            res = (on[j] * _silu(zh[j])).astype(o_ref.dtype)
            if prompt:
                o_ref[:, cl] = res
            else:
                o_ref[0, :, cl] = res
    if prompt:
        @pl.when(i == n_chunks - 1)
        def _():
            sout_ref[...] = s_ref[...]


def _level_masks(c, l_valid):
    i = jnp.arange(c, dtype=jnp.int32)[:, None]
    j = jnp.arange(LANE, dtype=jnp.int32)[None, :]
    masks = []
    s = 1
    while s < l_valid:
        masks.append((i // (2 * s) == j // (2 * s)) & (i % (2 * s) >= s) & (j % (2 * s) < s))
        s *= 2
    return jnp.stack(masks).astype(F32)


def _gdn_prompt(proj, ba, conv_w, hp, gn, c=128):
    t = proj.shape[0]
    n_chunks = t // c
    w = GDN_W
    lvl = _level_masks(c, c)
    kern = functools.partial(_gdn_kernel, c=c, l_valid=c, prompt=True, n_chunks=n_chunks, group=GDN_GROUP_PROMPT)
    return pl.pallas_call(
        kern,
        grid=(n_chunks,),
        in_specs=[pl.BlockSpec((c, w), lambda i: (i, 0)),
                  pl.BlockSpec((c, w), lambda i: (i, 1)),
                  pl.BlockSpec((c, w), lambda i: (i, 2)),
                  pl.BlockSpec((c, w), lambda i: (i, 3)),
                  pl.BlockSpec((c, LANE), lambda i: (i, 0)),
                  pl.BlockSpec((CONV_W, 3 * w), lambda i: (0, 0)),
                  pl.BlockSpec((8, LANE), lambda i: (0, 0)),
                  pl.BlockSpec((1, GDN_D), lambda i: (0, 0)),
                  pl.BlockSpec(lvl.shape, lambda i: (0, 0, 0))],
        out_specs=[pl.BlockSpec((c, w), lambda i: (i, 0)),
                   pl.BlockSpec((GDN_HEADS, GDN_D, GDN_D), lambda i: (0, 0, 0)),
                   pl.BlockSpec((8, 3 * w), lambda i: (0, 0))],
        out_shape=[jax.ShapeDtypeStruct((t, w), BF16),
                   jax.ShapeDtypeStruct((GDN_HEADS, GDN_D, GDN_D), F32),
                   jax.ShapeDtypeStruct((8, 3 * w), F32)],
        scratch_shapes=[pltpu.VMEM((8 + c, 3 * w), F32),
                        pltpu.VMEM((c, 3 * w), F32),
                        pltpu.VMEM((GDN_HEADS, GDN_D, GDN_D), F32)],
        compiler_params=_cparams(("arbitrary",)),
        name="gdn_prompt",
    )(proj, proj, proj, proj, ba, conv_w, hp, gn, lvl)


def _gdn_sample(proj3, ba3, conv_w, hp, gn, conv_in, s_in, l):
    b, c, _ = proj3.shape
    w = GDN_W
    lvl = _level_masks(c, l)
    kern = functools.partial(_gdn_kernel, c=c, l_valid=l, prompt=False, n_chunks=1, group=GDN_GROUP_SAMPLE)
    return pl.pallas_call(
        kern,
        grid=(b,),
        in_specs=[pl.BlockSpec((1, c, w), lambda i: (i, 0, 0)),
                  pl.BlockSpec((1, c, w), lambda i: (i, 0, 1)),
                  pl.BlockSpec((1, c, w), lambda i: (i, 0, 2)),
                  pl.BlockSpec((1, c, w), lambda i: (i, 0, 3)),
                  pl.BlockSpec((1, c, LANE), lambda i: (i, 0, 0)),
                  pl.BlockSpec((CONV_W, 3 * w), lambda i: (0, 0)),
                  pl.BlockSpec((8, LANE), lambda i: (0, 0)),
                  pl.BlockSpec((1, GDN_D), lambda i: (0, 0)),
                  pl.BlockSpec(lvl.shape, lambda i: (0, 0, 0)),
                  pl.BlockSpec((1, 8, 3 * w), lambda i: (i, 0, 0)),
                  pl.BlockSpec((1, GDN_HEADS, GDN_D, GDN_D), lambda i: (i, 0, 0, 0))],
        out_specs=[pl.BlockSpec((1, c, w), lambda i: (i, 0, 0)),
                   pl.BlockSpec((1, GDN_HEADS, GDN_D, GDN_D), lambda i: (i, 0, 0, 0)),
                   pl.BlockSpec((1, 8, 3 * w), lambda i: (i, 0, 0))],
        out_shape=[jax.ShapeDtypeStruct((b, c, w), F32),
                   jax.ShapeDtypeStruct((b, GDN_HEADS, GDN_D, GDN_D), F32),
                   jax.ShapeDtypeStruct((b, 8, 3 * w), F32)],
        scratch_shapes=[pltpu.VMEM((8 + c, 3 * w), F32),
                        pltpu.VMEM((c, 3 * w), F32)],
        compiler_params=_cparams(("arbitrary",)),
        name="gdn_sample",
    )(proj3, proj3, proj3, proj3, ba3, conv_w, hp, gn, lvl, conv_in, s_in)


def _t5_bucket(dist):
    d = jnp.maximum(dist, 0)
    max_exact = N_BUCKETS // 2
    large = max_exact + (jnp.log(jnp.maximum(d, 1).astype(F32) / max_exact)
                         / math.log(WINDOW / max_exact) * (N_BUCKETS - max_exact)).astype(jnp.int32)
    large = jnp.minimum(large, N_BUCKETS - 1)
    return jnp.where(d < max_exact, d, large)


def _bias_table(dist, rel_bias):
    valid = (dist >= 0) & (dist < WINDOW)
    b = rel_bias.astype(F32)[_t5_bucket(dist)]
    b = jnp.where(valid[:, :, None], b, NEG)
    return jnp.transpose(b, (2, 0, 1))


def _lo_hi(slab, g):
    lane = lax.broadcasted_iota(jnp.int32, slab.shape, 1)
    if g % 2 == 0:
        lo = jnp.where(lane < SWA_HD, slab, 0.0)
        hi = pltpu.roll(lo, SWA_HD, 1)
    else:
        hi = jnp.where(lane >= SWA_HD, slab, 0.0)
        lo = pltpu.roll(hi, SWA_HD, 1)
    return lo, hi


def _sink_softmax_chains(parts, sinks):
    cs = range(len(parts))
    m = [sinks[i] for i in cs]
    for j in range(len(parts[0])):
        mx = [jnp.max(parts[i][j], axis=-1, keepdims=True) for i in cs]
        m = [jnp.maximum(m[i], mx[i]) for i in cs]
    es = [[jnp.exp(p - m[i]) for p in parts[i]] for i in cs]
    den = [jnp.exp(sinks[i] - m[i]) for i in cs]
    for j in range(len(parts[0])):
        sm = [jnp.sum(es[i][j], axis=-1, keepdims=True) for i in cs]
        den = [den[i] + sm[i] for i in cs]
    inv = [1.0 / d for d in den]
    return [[e * inv[i] for e in es[i]] for i in cs]


SWA_GROUP_PROMPT = 2


def _swa_prompt_kernel(q_ref, kp_ref, kc_ref, vp_ref, vc_ref, bias_ref, sink_ref, o_ref):
    n = pl.program_id(0)
    wq = WINDOW
    col = lax.broadcasted_iota(jnp.int32, (wq, 2 * wq), 1)
    first = jnp.logical_and(n == 0, col < wq)
    sink_all = sink_ref[...]
    scale = SWA_HD ** -0.5
    for g0 in range(0, SWA_KV, SWA_GROUP_PROMPT):
        gs = range(g0, g0 + SWA_GROUP_PROMPT)
        kv = {}
        for g in gs:
            sl = slice((g // 2) * LANE, (g // 2 + 1) * LANE)
            kslab = jnp.concatenate([kp_ref[:, sl], kc_ref[:, sl]], axis=0)
            vslab = jnp.concatenate([vp_ref[:, sl], vc_ref[:, sl]], axis=0)
            kv[g] = ([x.astype(BF16) for x in _lo_hi(kslab, g)], [x.astype(BF16) for x in _lo_hi(vslab, g)])
        chains = [(g, s, half) for g in gs for s in range(2) for half in range(2)]
        qs = {(g, s): q_ref[:, (2 * g + s) * LANE:(2 * g + s + 1) * LANE].astype(BF16) for g in gs for s in range(2)}
        logits = [_dot_nt(qs[g, s], kv[g][0][half]) for g, s, half in chains]
        logits = [jnp.where(first, NEG, logits[i] * scale + bias_ref[4 * g + 2 * s + half])
                  for i, (g, s, half) in enumerate(chains)]
        sinks = [sink_all[:, 4 * g + 2 * s + half:4 * g + 2 * s + half + 1] for g, s, half in chains]
        probs = _sink_softmax_chains([[x] for x in logits], sinks)
        pv = [_dot(probs[i][0].astype(BF16), kv[g][1][half]) for i, (g, s, half) in enumerate(chains)]
        for i, (g, s, half) in enumerate(chains):
            if half == 0:
                o_ref[:, (2 * g + s) * LANE:(2 * g + s + 1) * LANE] = (pv[i] + pv[i + 1]).astype(o_ref.dtype)


def _swa_prompt(proj, qcol, bias, sinks):
    t = proj.shape[0]
    nb = t // WINDOW
    kb = qcol * (SWA_W // SWA_KVW) + SWA_W // SWA_KVW
    return pl.pallas_call(
        _swa_prompt_kernel,
        grid=(nb,),
        in_specs=[pl.BlockSpec((WINDOW, SWA_W), lambda n: (n, qcol)),
                  pl.BlockSpec((WINDOW, SWA_KVW), lambda n: (jnp.maximum(n - 1, 0), kb)),
                  pl.BlockSpec((WINDOW, SWA_KVW), lambda n: (n, kb)),
                  pl.BlockSpec((WINDOW, SWA_KVW), lambda n: (jnp.maximum(n - 1, 0), kb + 1)),
                  pl.BlockSpec((WINDOW, SWA_KVW), lambda n: (n, kb + 1)),
                  pl.BlockSpec((SWA_HEADS, WINDOW, 2 * WINDOW), lambda n: (0, 0, 0)),
                  pl.BlockSpec((1, LANE), lambda n: (0, 0))],
        out_specs=pl.BlockSpec((WINDOW, SWA_W), lambda n: (n, 0)),
        out_shape=jax.ShapeDtypeStruct((t, SWA_W), BF16),
        compiler_params=_cparams(("arbitrary",)),
        name="swa_prompt",
    )(proj, proj, proj, proj, proj, bias, sinks)


def _swa_sample_kernel(q_ref, kn_ref, vn_ref, kc_ref, vc_ref, bc_ref, bn_ref, sink_ref, o_ref):
    r = 8
    sink_all = sink_ref[...]
    q = q_ref[0]
    knew = kn_ref[0]
    vnew = vn_ref[0]
    scale = SWA_HD ** -0.5
    gs = range(SWA_KV)
    chains = [(g, half) for g in gs for half in range(2)]
    sls = [slice((g // 2) * LANE, (g // 2 + 1) * LANE) for g in gs]
    kc = [[x.astype(BF16) for x in _lo_hi(kc_ref[0, :, sls[g]], g)] for g in gs]
    vc = [[x.astype(BF16) for x in _lo_hi(vc_ref[0, :, sls[g]], g)] for g in gs]
    kn = [[_rowpad(x, LANE).astype(BF16) for x in _lo_hi(knew[:, sls[g]], g)] for g in gs]
    vn = [[_rowpad(x, LANE).astype(BF16) for x in _lo_hi(vnew[:, sls[g]], g)] for g in gs]
    qs = [jnp.concatenate([q[:, (2 * g) * LANE:(2 * g + 1) * LANE],
                           q[:, (2 * g + 1) * LANE:(2 * g + 2) * LANE]], axis=0).astype(BF16) for g in gs]
    lc = [_dot_nt(qs[g], kc[g][half]) for g, half in chains]
    ln = [_dot_nt(qs[g], kn[g][half]) for g, half in chains]
    lc = [lc[i] * scale + bc_ref[2 * g + half] for i, (g, half) in enumerate(chains)]
    ln = [ln[i] * scale + bn_ref[2 * g + half] for i, (g, half) in enumerate(chains)]
    sinks = [jnp.concatenate([jnp.broadcast_to(sink_all[:, 4 * g + half:4 * g + half + 1], (r, 1)),
                              jnp.broadcast_to(sink_all[:, 4 * g + 2 + half:4 * g + 2 + half + 1], (r, 1))], axis=0)
             for g, half in chains]
    probs = _sink_softmax_chains([[lc[i], ln[i]] for i in range(len(chains))], sinks)
    pvc = [_dot(probs[i][0].astype(BF16), vc[g][half]) for i, (g, half) in enumerate(chains)]
    pvn = [_dot(probs[i][1].astype(BF16), vn[g][half]) for i, (g, half) in enumerate(chains)]
    for g in gs:
        acc = (pvc[2 * g] + pvn[2 * g]) + (pvc[2 * g + 1] + pvn[2 * g + 1])
        o_ref[0, :, (2 * g) * LANE:(2 * g + 1) * LANE] = acc[:r]
        o_ref[0, :, (2 * g + 1) * LANE:(2 * g + 2) * LANE] = acc[r:]


def _swa_sample(proj3, qcol, k_cache, v_cache, bias_c, bias_n, sinks):
    b, r, _ = proj3.shape
    kb = qcol * (SWA_W // SWA_KVW) + SWA_W // SWA_KVW
    return pl.pallas_call(
        _swa_sample_kernel,
        grid=(b,),
        in_specs=[pl.BlockSpec((1, r, SWA_W), lambda i: (i, 0, qcol)),
                  pl.BlockSpec((1, r, SWA_KVW), lambda i: (i, 0, kb)),
                  pl.BlockSpec((1, r, SWA_KVW), lambda i: (i, 0, kb + 1)),
                  pl.BlockSpec((1, WINDOW, SWA_KVW), lambda i: (i, 0, 0)),
                  pl.BlockSpec((1, WINDOW, SWA_KVW), lambda i: (i, 0, 0)),
                  pl.BlockSpec((2 * SWA_KV, 2 * r, LANE), lambda i: (0, 0, 0)),
                  pl.BlockSpec((2 * SWA_KV, 2 * r, LANE), lambda i: (0, 0, 0)),
                  pl.BlockSpec((1, LANE), lambda i: (0, 0))],
        out_specs=pl.BlockSpec((1, r, SWA_W), lambda i: (i, 0, 0)),
        out_shape=jax.ShapeDtypeStruct((b, r, SWA_W), F32),
        compiler_params=_cparams(("arbitrary",)),
        name="swa_sample",
    )(proj3, proj3, proj3, k_cache, v_cache, bias_c, bias_n, sinks)


def _moe_kernel(ord_ref, be_ref, bs_ref, bn_ref, h_hbm, wg_ref, wu_ref, wd_ref, y_hbm,
                xbuf, x16, acc, gsem, ssem, *, n_blocks):
    k = pl.program_id(0)
    c = pl.program_id(1)
    n = bn_ref[k]
    slot = lax.rem(k, 2)
    d = x16.shape[1]
    half = d // 2

    def gather_copy(tok, r):
        return pltpu.make_async_copy(h_hbm.at[pl.ds(tok, 1)], xbuf.at[pl.ds(r, 1)], gsem)

    def scatter_copy(sl, r, a):
        return pltpu.make_async_copy(acc.at[sl, pl.ds(r, 1)], y_hbm.at[pl.ds(a, 1)], ssem.at[sl])

    def gather_start(kk):
        st = bs_ref[kk]

        def body(r, carry):
            gather_copy(lax.shift_right_logical(ord_ref[st + r], 1), r).start()
            return carry
        lax.fori_loop(0, bn_ref[kk], body, 0)

    def gather_wait(kk):
        def body(r, carry):
            gather_copy(0, 0).wait()
            return carry
        lax.fori_loop(0, bn_ref[kk], body, 0)

    def scatter_start(kk, sl):
        st = bs_ref[kk]

        def body(r, carry):
            scatter_copy(sl, r, ord_ref[st + r]).start()
            return carry
        lax.fori_loop(0, bn_ref[kk], body, 0)

    def scatter_wait(kk, sl):
        def body(r, carry):
            scatter_copy(sl, 0, 0).wait()
            return carry
        lax.fori_loop(0, bn_ref[kk], body, 0)

    @pl.when(c == 0)
    def _():
        @pl.when(k == 0)
        def _():
            xbuf[...] = jnp.zeros(xbuf.shape, xbuf.dtype)
            gather_start(0)
        gather_wait(k)
        words = xbuf[...]
        lo = lax.bitcast_convert_type(lax.shift_left(words, jnp.uint32(16)), F32)
        hi = lax.bitcast_convert_type(words & jnp.uint32(0xFFFF0000), F32)
        x16[:, 0:half] = lo.astype(BF16)
        x16[:, half:d] = hi.astype(BF16)

        @pl.when(k + 1 < n_blocks)
        def _():
            gather_start(k + 1)
        acc[slot] = jnp.zeros(acc.shape[1:], F32)

    @pl.when(n > 0)
    def _():
        kt = 512
        hg = None
        hu = None
        for j in range(d // kt):
            xs = x16[:, j * kt:(j + 1) * kt]
            pg = _dot(xs, wg_ref[0, j * kt:(j + 1) * kt, :].astype(BF16))
            pu = _dot(xs, wu_ref[0, j * kt:(j + 1) * kt, :].astype(BF16))
            hg = pg if hg is None else hg + pg
            hu = pu if hu is None else hu + pu
        hid = (_silu(hg) * hu).astype(BF16)
        for j in range(d // kt):
            cols = slice(j * kt, (j + 1) * kt)
            acc[slot, :, cols] += _dot(hid, wd_ref[0, :, cols].astype(BF16))

    @pl.when(c == MOE_NC - 1)
    def _():
        scatter_start(k, slot)

        @pl.when(k > 0)
        def _():
            scatter_wait(k - 1, 1 - slot)

        @pl.when(k == n_blocks - 1)
        def _():
            scatter_wait(k, slot)


def _moe_ffn(order, blk_e, blk_start, blk_n, h_packed, w_gate, w_up, w_down):
    a = order.shape[0]
    d = 2 * h_packed.shape[1]
    n_blocks = blk_e.shape[0]
    ce = D_EXPERT // MOE_NC
    grid_spec = pltpu.PrefetchScalarGridSpec(
        num_scalar_prefetch=4,
        grid=(n_blocks, MOE_NC),
        in_specs=[pl.BlockSpec(memory_space=pl.ANY),
                  pl.BlockSpec((1, d, ce), lambda k, c, o, be, bs, bn: (be[k], 0, c)),
                  pl.BlockSpec((1, d, ce), lambda k, c, o, be, bs, bn: (be[k], 0, c)),
                  pl.BlockSpec((1, ce, d), lambda k, c, o, be, bs, bn: (be[k], c, 0))],
        out_specs=pl.BlockSpec(memory_space=pl.ANY),
        scratch_shapes=[pltpu.VMEM((MOE_ROWS, d // 2), jnp.uint32),
                        pltpu.VMEM((MOE_ROWS, d), BF16),
                        pltpu.VMEM((2, MOE_ROWS, d), F32),
                        pltpu.SemaphoreType.DMA(()),
                        pltpu.SemaphoreType.DMA((2,))],
    )
    return pl.pallas_call(
        functools.partial(_moe_kernel, n_blocks=n_blocks),
        grid_spec=grid_spec,
        out_shape=jax.ShapeDtypeStruct((a, d), F32),
        compiler_params=_cparams(("arbitrary", "arbitrary")),
        name="moe_ffn",
    )(order, blk_e, blk_start, blk_n, h_packed, w_gate, w_up, w_down)


def _final_kernel(x_ref, y0_ref, y1_ref, p_ref, g_ref, nf_ref, o_ref):
    p = p_ref[...]
    moe = y0_ref[...] * p[:, 0:1] + y1_ref[...] * p[:, 1:2]
    x = x_ref[...] + g_ref[...] * moe
    o_ref[...] = x * lax.rsqrt(jnp.mean(x * x, axis=-1, keepdims=True) + EPS) * nf_ref[...]


def _final(x, y_pair, row0, gates, gt2, nf, tm=256):
    t, d = x.shape
    tm = min(tm, t)
    b0 = row0 // tm
    return pl.pallas_call(
        _final_kernel,
        grid=(t // tm,),
        in_specs=[pl.BlockSpec((tm, d), lambda i: (i, 0)),
                  pl.BlockSpec((tm, d), lambda i: (b0 + i, 0)),
                  pl.BlockSpec((tm, d), lambda i: (b0 + i, 1)),
                  pl.BlockSpec((tm, LANE), lambda i: (b0 + i, 0)),
                  _mod_spec(gt2, tm, d),
                  pl.BlockSpec((1, d), lambda i: (0, 0))],
        out_specs=pl.BlockSpec((tm, d), lambda i: (i, 0)),
        out_shape=jax.ShapeDtypeStruct((t, d), F32),
        compiler_params=_cparams(("arbitrary",)),
        name="final",
    )(x, y_pair, y_pair, gates, gt2, nf.reshape(1, d))


def _dispatch(expert_idx):
    t, k = expert_idx.shape
    a = t * k
    n_blocks = a // MOE_ROWS + N_EXPERTS
    flat_e = expert_idx.reshape(a)
    order = jnp.argsort(flat_e, stable=True).astype(jnp.int32)
    counts = jnp.zeros((N_EXPERTS,), jnp.int32).at[flat_e].add(1)
    seg_start = jnp.cumsum(counts) - counts
    nblk = (counts + MOE_ROWS - 1) // MOE_ROWS
    blk_end = jnp.cumsum(nblk)
    n_active = blk_end[-1]
    kk = jnp.minimum(jnp.arange(n_blocks, dtype=jnp.int32), n_active - 1)
    blk_e = jnp.minimum(jnp.searchsorted(blk_end, kk, side='right'), N_EXPERTS - 1).astype(jnp.int32)
    j = kk - (blk_end[blk_e] - nblk[blk_e])
    blk_start = (seg_start[blk_e] + j * MOE_ROWS).astype(jnp.int32)
    blk_n = jnp.clip(counts[blk_e] - j * MOE_ROWS, 0, MOE_ROWS)
    blk_n = jnp.where(jnp.arange(n_blocks) < n_active, blk_n, 0).astype(jnp.int32)
    return order, blk_e, blk_start, blk_n


def kernel(x_prompt, x_sample, state_gdn, state_conv, cache_swa_k, cache_swa_v, c_prompt, c_sample, w_ada, b_ada, norm_mix, w_in, conv_w, a_log, dt_bias, gdn_norm, swa_sinks, rel_bias, w_out, norm_moe, router_group, router_group_bias, router_expert, router_expert_bias, w_gate, w_up, w_down, norm_final):
    depth = w_ada.shape[0]
    assert depth == 1
    bp, seq, d = x_prompt.shape
    assert bp == 1 and seq % WINDOW == 0
    bs, ls, _ = x_sample.shape
    tp = bp * seq
    ts = bs * ls
    n_main = 2 * GDN_W + 2 * GDN_W
    n_ba = 2 * GDN_HEADS

    c_all = jnp.concatenate([c_prompt, c_sample], axis=0)
    m_pad = -(-c_all.shape[0] // 16) * 16
    c_all = jnp.pad(c_all, ((0, m_pad - c_all.shape[0]), (0, 0)))
    mod = _adaln(c_all, w_ada[0], b_ada[0])
    mod_p = [mod[0:1, j * d:(j + 1) * d] for j in range(6)]
    mod_s = [jnp.repeat(mod[1:1 + bs, j * d:(j + 1) * d], ls, axis=0) for j in range(6)]

    w_cat = jnp.concatenate([w_in[0][:, :n_main], w_in[0][:, n_main + n_ba:]], axis=1).astype(BF16)
    w_ba = jnp.pad(w_in[0][:, n_main:n_main + n_ba], ((0, 0), (0, LANE - n_ba))).astype(BF16)
    w_out16 = w_out[0].astype(BF16)
    qcol = n_main // SWA_W
    hp = jnp.zeros((8, LANE), F32)
    hp = hp.at[0, GDN_HEADS:2 * GDN_HEADS].set(a_log[0]).at[1, GDN_HEADS:2 * GDN_HEADS].set(dt_bias[0])
    gn = gdn_norm[0].reshape(1, GDN_D)
    sinks = jnp.pad(swa_sinks[0], (0, LANE - SWA_HEADS)).reshape(1, LANE)

    xp = x_prompt.reshape(tp, d)
    h_p = _norm_mod(xp, norm_mix[0], mod_p[1], mod_p[0])
    proj_p = _matmul(h_p, w_cat, 1024, 512, name="inproj_p")
    ba_p = _matmul(h_p, w_ba, 1024, LANE, name="inproj_ba_p")
    og_p, s_p, conv_p = _gdn_prompt(proj_p, ba_p, conv_w[0], hp, gn)
    conv_p = conv_p[8 - (CONV_W - 1):]
    qi = jnp.arange(WINDOW, dtype=jnp.int32)
    sj = jnp.arange(2 * WINDOW, dtype=jnp.int32)
    bias_p = _bias_table((WINDOW + qi)[:, None] - sj[None, :], rel_bias)
    os_p = _swa_prompt(proj_p, qcol, bias_p, sinks)
    kcol = n_main + SWA_W
    k_p = proj_p[tp - WINDOW:, kcol:kcol + SWA_KVW]
    v_p = proj_p[tp - WINDOW:, kcol + SWA_KVW:kcol + 2 * SWA_KVW]
    x1_p = _outproj(og_p, os_p, w_out16, xp, mod_p[2])

    xs = x_sample.reshape(ts, d)
    h_s = _norm_mod(xs, norm_mix[0], mod_s[1], mod_s[0])
    proj_s = _matmul(h_s, w_cat, 512, 512, name="inproj_s")
    ba_s = _matmul(h_s, w_ba, 512, LANE, name="inproj_ba_s")
    rs = 8
    assert CONV_W - 1 <= ls <= rs
    proj_s3 = proj_s.reshape(bs, ls, proj_s.shape[1])
    proj_s8 = jnp.pad(proj_s3, ((0, 0), (0, rs - ls), (0, 0)))
    ba_s8 = jnp.pad(ba_s.reshape(bs, ls, LANE), ((0, 0), (0, rs - ls), (0, 0)))
    conv_in8 = jnp.pad(state_conv[0], ((0, 0), (8 - (CONV_W - 1), 0), (0, 0)))
    og_s, s_s, conv_s = _gdn_sample(proj_s8, ba_s8, conv_w[0], hp, gn, conv_in8, state_gdn[0], ls)
    conv_s = conv_s[:, 8 - (CONV_W - 1):]
    wb = cache_swa_k.shape[2]
    assert wb == WINDOW == LANE
    dist_s = (wb + jnp.arange(ls, dtype=jnp.int32))[:, None] - jnp.arange(wb + ls, dtype=jnp.int32)[None, :]
    bias_s = _bias_table(dist_s, rel_bias)
    bias_s = jnp.pad(bias_s, ((0, 0), (0, rs - ls), (0, 2 * LANE - wb - ls)), constant_values=NEG)
    bias_s = bias_s.reshape(SWA_KV, 2, 2, rs, 2 * LANE)
    bias_s = jnp.transpose(bias_s, (0, 2, 1, 3, 4)).reshape(2 * SWA_KV, 2 * rs, 2 * LANE)
    k_cache = cache_swa_k[0].reshape(bs, wb, SWA_KVW)
    v_cache = cache_swa_v[0].reshape(bs, wb, SWA_KVW)
    os_s = _swa_sample(proj_s8, qcol, k_cache, v_cache, bias_s[:, :, :LANE], bias_s[:, :, LANE:], sinks)
    k_s = jnp.concatenate([k_cache[:, ls:], proj_s3[:, :, kcol:kcol + SWA_KVW]], axis=1)
    v_s = jnp.concatenate([v_cache[:, ls:], proj_s3[:, :, kcol + SWA_KVW:kcol + 2 * SWA_KVW]], axis=1)
    x1_s = _outproj(og_s[:, :ls].reshape(ts, GDN_W).astype(BF16), os_s[:, :ls].reshape(ts, SWA_W).astype(BF16),
                    w_out16, xs, mod_s[2], tm=512)

    wr = jnp.concatenate([router_group[0], router_expert[0]], axis=1)
    wr = jnp.pad(wr, ((0, 0), (0, LANE - wr.shape[1])))
    wr_hi = wr.astype(BF16)
    wr_mid = (wr - wr_hi.astype(F32)).astype(BF16)
    wr2 = jnp.stack([wr_hi, wr_mid])
    br = jnp.pad(jnp.concatenate([router_group_bias[0], router_expert_bias[0]]),
                 (0, LANE - N_GROUPS - N_EXPERTS)).reshape(1, LANE)
    t_all = tp + ts
    h2, eidx, gates = _norm_router(x1_p, x1_s, norm_moe[0], mod_p[4], mod_p[3], mod_s[4], mod_s[3], wr2, br)
    order, blk_e, blk_start, blk_n = _dispatch(eidx[:, :2])
    y = _moe_ffn(order, blk_e, blk_start, blk_n, h2, w_gate[0], w_up[0], w_down[0])
    y_pair = y.reshape(t_all, 2 * d)
    y_p = _final(x1_p, y_pair, 0, gates, mod_p[5], norm_final)
    y_s = _final(x1_s, y_pair, tp, gates, mod_s[5], norm_final)

    sdt = state_gdn.dtype
    return (y_p.reshape(bp, seq, d), y_s.reshape(bs, ls, d),
            s_p.reshape(1, bp, GDN_HEADS, GDN_D, GDN_D).astype(sdt), s_s[None].astype(sdt),
            conv_p.reshape(1, bp, CONV_W - 1, 3 * GDN_W).astype(state_conv.dtype), conv_s[None].astype(state_conv.dtype),
            k_p.reshape(1, bp, WINDOW, SWA_KV, SWA_HD).astype(cache_swa_k.dtype),
            k_s.reshape(1, bs, wb, SWA_KV, SWA_HD).astype(cache_swa_k.dtype),
            v_p.reshape(1, bp, WINDOW, SWA_KV, SWA_HD).astype(cache_swa_v.dtype),
            v_s.reshape(1, bs, wb, SWA_KV, SWA_HD).astype(cache_swa_v.dtype))
```

```python
import functools
import math

import jax
import jax.numpy as jnp
from jax import lax
from jax.experimental import pallas as pl
from jax.experimental.pallas import tpu as pltpu

F32 = jnp.float32
BF16 = jnp.bfloat16
EPS = 1e-6
NEG = -1e30

LANE = 128
GDN_HEADS = 16
GDN_D = 128
GDN_W = GDN_HEADS * GDN_D
CONV_W = 4
SWA_HEADS = 32
SWA_KV = 8
SWA_HD = 64
SWA_W = SWA_HEADS * SWA_HD
SWA_KVW = SWA_KV * SWA_HD
WINDOW = 128
N_BUCKETS = 32
N_GROUPS = 8
EPG = 8
N_EXPERTS = 64
D_EXPERT = 1024
MOE_ROWS = 448
VMEM_LIMIT = 56 * 1024 * 1024


def _cparams(sem):
    return pltpu.CompilerParams(dimension_semantics=sem, vmem_limit_bytes=VMEM_LIMIT)


def _sigmoid(x):
    return 1.0 / (1.0 + jnp.exp(-x))


def _silu(x):
    return x * _sigmoid(x)


def _dot(a, b):
    return jnp.dot(a, b, preferred_element_type=F32)


def _dot_nt(a, b):
    return lax.dot_general(a, b, (((1,), (1,)), ((), ())), preferred_element_type=F32)


def _dot_tn(a, b):
    return lax.dot_general(a, b, (((0,), (0,)), ((), ())), preferred_element_type=F32)


def _ada_kernel(c_ref, w_ref, b_ref, o_ref):
    a = _silu(c_ref[...]).astype(BF16)
    o_ref[...] = _dot(a, w_ref[...].astype(BF16)) + b_ref[...]


def _adaln(c, w, b, tn=512):
    m, d = c.shape
    n = w.shape[1]
    return pl.pallas_call(
        _ada_kernel,
        grid=(n // tn,),
        in_specs=[pl.BlockSpec((m, d), lambda j: (0, 0)),
                  pl.BlockSpec((d, tn), lambda j: (0, j)),
                  pl.BlockSpec((1, tn), lambda j: (0, j))],
        out_specs=pl.BlockSpec((m, tn), lambda j: (0, j)),
        out_shape=jax.ShapeDtypeStruct((m, n), F32),
        compiler_params=_cparams(("arbitrary",)),
        name="adaln",
    )(c, w, b.reshape(1, n))


def _norm_kernel(x_ref, g_ref, sc_ref, sh_ref, o_ref):
    x = x_ref[...]
    y = x * lax.rsqrt(jnp.mean(x * x, axis=-1, keepdims=True) + EPS) * g_ref[...]
    o_ref[...] = (y * (1.0 + sc_ref[...]) + sh_ref[...]).astype(o_ref.dtype)


def _mod_spec(mod, tm, d):
    if mod.shape[0] == 1:
        return pl.BlockSpec((1, d), lambda i: (0, 0))
    return pl.BlockSpec((tm, d), lambda i: (i, 0))


def _norm_mod(x, g, sc, sh, tm=256):
    t, d = x.shape
    tm = min(tm, t)
    return pl.pallas_call(
        _norm_kernel,
        grid=(t // tm,),
        in_specs=[pl.BlockSpec((tm, d), lambda i: (i, 0)),
                  pl.BlockSpec((1, d), lambda i: (0, 0)),
                  _mod_spec(sc, tm, d), _mod_spec(sh, tm, d)],
        out_specs=pl.BlockSpec((tm, d), lambda i: (i, 0)),
        out_shape=jax.ShapeDtypeStruct((t, d), BF16),
        compiler_params=_cparams(("arbitrary",)),
        name="norm_mod",
    )(x, g.reshape(1, d), sc, sh)


def _split3(x):
    hi = x.astype(BF16)
    r1 = x - hi.astype(F32)
    mid = r1.astype(BF16)
    lo = (r1 - mid.astype(F32)).astype(BF16)
    return hi, mid, lo


def _norm_router_kernel(*refs, nba):
    (xa_ref, xb_ref, g_ref, sca_ref, sha_ref, scb_ref, shb_ref, wr_ref, br_ref,
     o_ref, e_ref, p_ref) = refs
    first = pl.program_id(0) < nba
    x = jnp.where(first, xa_ref[...], xb_ref[...])
    sc = jnp.where(first, sca_ref[...], scb_ref[...])
    sh = jnp.where(first, sha_ref[...], shb_ref[...])
    y = x * lax.rsqrt(jnp.mean(x * x, axis=-1, keepdims=True) + EPS) * g_ref[...]
    t = y * (1.0 + sc) + sh
    t_hi, t_mid, _ = _split3(t)
    half = t.shape[1] // 2
    bits = lax.bitcast_convert_type(t_hi.astype(F32), jnp.uint32)
    o_ref[...] = lax.shift_right_logical(bits[:, :half], jnp.uint32(16)) | (bits[:, half:] & jnp.uint32(0xFFFF0000))
    w_hi = wr_ref[0]
    w_mid = wr_ref[1]
    lg = _dot(t_hi, w_hi) + (_dot(t_mid, w_hi) + _dot(t_hi, w_mid)) + br_ref[...]
    lane = lax.broadcasted_iota(jnp.int32, lg.shape, 1)
    big = jnp.int32(4 * LANE)
    lgrp = jnp.where(lane < N_GROUPS, lg, NEG)
    mg = jnp.max(lgrp, axis=-1, keepdims=True)
    grp = jnp.min(jnp.where(lgrp == mg, lane, big), axis=-1, keepdims=True)
    p_grp = 1.0 / jnp.sum(jnp.exp(lgrp - mg), axis=-1, keepdims=True)
    lo = N_GROUPS + grp * EPG
    emask = jnp.logical_and(lane >= lo, lane < lo + EPG)
    le = jnp.where(emask, lg, NEG)
    m1 = jnp.max(le, axis=-1, keepdims=True)
    i1 = jnp.min(jnp.where(le == m1, lane, big), axis=-1, keepdims=True)
    le2 = jnp.where(lane == i1, NEG, le)
    m2 = jnp.max(le2, axis=-1, keepdims=True)
    i2 = jnp.min(jnp.where(le2 == m2, lane, big), axis=-1, keepdims=True)
    e2 = jnp.exp(m2 - m1)
    w1 = p_grp / (1.0 + e2)
    w2 = p_grp * e2 / (1.0 + e2)
    e_ref[...] = jnp.where(lane == 0, i1 - N_GROUPS, jnp.where(lane == 1, i2 - N_GROUPS, 0))
    p_ref[...] = jnp.where(lane == 0, w1, jnp.where(lane == 1, w2, 0.0))


def _norm_router(xa, xb, g, sca, sha, scb, shb, wr, br, tm=256):
    ta, d = xa.shape
    tb = xb.shape[0]
    tm = min(tm, math.gcd(ta, tb))
    nba = ta // tm
    nbb = tb // tm
    t_all = ta + tb
    spec_a = pl.BlockSpec((tm, d), lambda i: (jnp.minimum(i, nba - 1), 0))
    spec_b = pl.BlockSpec((tm, d), lambda i: (jnp.maximum(i - nba, 0), 0))
    row = pl.BlockSpec((1, d), lambda i: (0, 0))
    return pl.pallas_call(
        functools.partial(_norm_router_kernel, nba=nba),
        grid=(nba + nbb,),
        in_specs=[spec_a, spec_b, row, row, row, spec_b, spec_b,
                  pl.BlockSpec((2, d, LANE), lambda i: (0, 0, 0)),
                  pl.BlockSpec((1, LANE), lambda i: (0, 0))],
        out_specs=[pl.BlockSpec((tm, d // 2), lambda i: (i, 0)),
                   pl.BlockSpec((tm, LANE), lambda i: (i, 0)),
                   pl.BlockSpec((tm, LANE), lambda i: (i, 0))],
        out_shape=[jax.ShapeDtypeStruct((t_all, d // 2), jnp.uint32),
                   jax.ShapeDtypeStruct((t_all, LANE), jnp.int32),
                   jax.ShapeDtypeStruct((t_all, LANE), F32)],
        compiler_params=_cparams(("arbitrary",)),
        name="norm_router",
    )(xa, xb, g.reshape(1, d), sca, sha, scb, shb, wr, br)


def _mm_kernel(x_ref, w_ref, o_ref):
    o_ref[...] = _dot(x_ref[...], w_ref[...]).astype(o_ref.dtype)


def _matmul(x, w, tm, tn, out_dtype=F32, name="matmul"):
    m, k = x.shape
    n = w.shape[1]
    tm = min(tm, m)
    tn = min(tn, n)
    return pl.pallas_call(
        _mm_kernel,
        grid=(m // tm, n // tn),
        in_specs=[pl.BlockSpec((tm, k), lambda i, j: (i, 0)),
                  pl.BlockSpec((k, tn), lambda i, j: (0, j))],
        out_specs=pl.BlockSpec((tm, tn), lambda i, j: (i, j)),
        out_shape=jax.ShapeDtypeStruct((m, n), out_dtype),
        compiler_params=_cparams(("arbitrary", "arbitrary")),
        name=name,
    )(x, w)


def _outproj_kernel(a_ref, b_ref, wa_ref, wb_ref, x_ref, g_ref, o_ref):
    mix = _dot(a_ref[...], wa_ref[...]) + _dot(b_ref[...], wb_ref[...])
    o_ref[...] = x_ref[...] + g_ref[...] * mix


def _outproj(a, b, w, x, gate, tm=1024, tn=512):
    m, ka = a.shape
    kb = b.shape[1]
    n = w.shape[1]
    tm = min(tm, m)
    if gate.shape[0] == 1:
        gspec = pl.BlockSpec((1, tn), lambda i, j: (0, j))
    else:
        gspec = pl.BlockSpec((tm, tn), lambda i, j: (i, j))
    return pl.pallas_call(
        _outproj_kernel,
        grid=(m // tm, n // tn),
        in_specs=[pl.BlockSpec((tm, ka), lambda i, j: (i, 0)),
                  pl.BlockSpec((tm, kb), lambda i, j: (i, 0)),
                  pl.BlockSpec((ka, tn), lambda i, j: (0, j)),
                  pl.BlockSpec((kb, tn), lambda i, j: (1, j)),
                  pl.BlockSpec((tm, tn), lambda i, j: (i, j)),
                  gspec],
        out_specs=pl.BlockSpec((tm, tn), lambda i, j: (i, j)),
        out_shape=jax.ShapeDtypeStruct((m, n), F32),
        compiler_params=_cparams(("arbitrary", "arbitrary")),
        name="outproj",
    )(a, b, w, w, x, gate)


GDN_GROUP_PROMPT = 8
GDN_GROUP_SAMPLE = 16


def _rowpad(a, rows):
    if a.shape[0] == rows:
        return a
    return jnp.concatenate([a, jnp.zeros((rows - a.shape[0], a.shape[1]), a.dtype)], axis=0)


def _gdn_heads(qn, kn, vh, bcol, gcol, s_prev, lvl_ref, c):
    cp = LANE
    n = len(qn)
    hs = range(n)
    lane = lax.broadcasted_iota(jnp.int32, (c, cp), 1)
    row = lax.broadcasted_iota(jnp.int32, (c, cp), 0)
    tri = lane <= row

    def b16(x):
        return x.astype(BF16)

    g_hi = [b16(g).astype(F32) for g in gcol]
    r1 = [gcol[h] - g_hi[h] for h in hs]
    g_mid = [b16(r).astype(F32) for r in r1]
    g_lo = [r1[h] - g_mid[h] for h in hs]
    a_mat = [b16(jnp.where(lane == 0, g_hi[h], jnp.where(lane == 1, g_mid[h], jnp.where(lane == 2, g_lo[h],
                 jnp.where(lane < 6, 1.0, 0.0))))) for h in hs]
    b_mat = [b16(_rowpad(jnp.where(lane < 3, 1.0, jnp.where(lane == 3, -g_hi[h], jnp.where(lane == 4, -g_mid[h],
                 jnp.where(lane == 5, -g_lo[h], 0.0)))), cp)) for h in hs]
    kn16 = [b16(k) for k in kn]
    qkn16 = [jnp.concatenate([b16(qn[h]), kn16[h]], axis=0) for h in hs]
    knp16 = [_rowpad(k, cp) for k in kn16]
    diff = [_dot_nt(a_mat[h], b_mat[h]) for h in hs]
    qk_kk = [_dot_nt(qkn16[h], knp16[h]) for h in hs]
    decay = [jnp.where(tri, jnp.exp(jnp.minimum(d, 0.0)), 0.0) for d in diff]
    eg = [jnp.exp(g) for g in gcol]
    lmat = [(bcol[h] * qk_kk[h][c:]) * decay[h] for h in hs]
    qkd16 = [b16(qk_kk[h][:c] * decay[h]) for h in hs]
    n_mat = [-(lm * lvl_ref[0]) for lm in lmat]
    for lv in range(1, lvl_ref.shape[0]):
        bm = [lm * lvl_ref[lv] for lm in lmat]
        n16 = [b16(x) for x in n_mat]
        w_mat = [bm[h] + _dot(n16[h], _rowpad(b16(bm[h]), cp)) for h in hs]
        n_mat = [n_mat[h] - w_mat[h] - _dot(b16(w_mat[h]), _rowpad(n16[h], cp)) for h in hs]
    rhs = [jnp.concatenate([vh[h] * bcol[h], kn[h] * (bcol[h] * eg[h])], axis=1) for h in hs]
    sol = [rhs[h] + _dot(b16(n_mat[h]), _rowpad(b16(rhs[h]), cp)) for h in hs]
    kq16 = [jnp.concatenate([b16(sol[h][:, GDN_D:]), b16(qn[h] * eg[h])], axis=0) for h in hs]
    kq_s = [_dot(kq16[h], b16(s_prev[h])) for h in hs]
    u = [sol[h][:, :GDN_D] - kq_s[h][:c] for h in hs]
    u16 = [_rowpad(b16(x), cp) for x in u]
    g_last = [g[c - 1:c, :] for g in gcol]
    kd16 = [_rowpad(b16(kn[h] * jnp.exp(g_last[h] - gcol[h])), cp) for h in hs]
    o = [kq_s[h][c:] + _dot(qkd16[h], u16[h]) for h in hs]
    s_new = [s_prev[h] * jnp.exp(g_last[h]) + _dot_tn(kd16[h], u16[h]) for h in hs]
    return o, s_new


def _gdn_kernel(*refs, c, l_valid, prompt, n_chunks, group):
    if prompt:
        (q_ref, k_ref, v_ref, z_ref, ba_ref, cw_ref, hp_ref, gn_ref, lvl_ref,
         o_ref, sout_ref, cout_ref, xe_ref, cb_ref, s_ref) = refs
    else:
        (q_ref, k_ref, v_ref, z_ref, ba_ref, cw_ref, hp_ref, gn_ref, lvl_ref, cin_ref, sin_ref,
         o_ref, sout_ref, cout_ref, xe_ref, cb_ref) = refs
    i = pl.program_id(0)
    w = GDN_W
    if prompt:
        @pl.when(i == 0)
        def _():
            xe_ref[0:8, :] = jnp.zeros((8, 3 * w), F32)
            s_ref[...] = jnp.zeros(s_ref.shape, F32)
        xe_ref[8:8 + c, 0:w] = q_ref[...]
        xe_ref[8:8 + c, w:2 * w] = k_ref[...]
        xe_ref[8:8 + c, 2 * w:3 * w] = v_ref[...]
    else:
        xe_ref[0:8, :] = cin_ref[0]
        xe_ref[8:8 + c, 0:w] = q_ref[0]
        xe_ref[8:8 + c, w:2 * w] = k_ref[0]
        xe_ref[8:8 + c, 2 * w:3 * w] = v_ref[0]
    rowc = lax.broadcasted_iota(jnp.int32, (c, 512), 0)
    for cb in range(3 * w // 512):
        cs = slice(cb * 512, (cb + 1) * 512)
        acc = xe_ref[8:8 + c, cs] * cw_ref[CONV_W - 1:CONV_W, cs]
        for j in range(CONV_W - 1):
            acc = acc + xe_ref[8 - (CONV_W - 1) + j:8 - (CONV_W - 1) + j + c, cs] * cw_ref[j:j + 1, cs]
        y = _silu(acc)
        if l_valid < c:
            y = jnp.where(rowc < l_valid, y, 0.0)
        cb_ref[:, cs] = y
    new_conv = xe_ref[l_valid:l_valid + 8, :]
    if prompt:
        cout_ref[...] = new_conv
        xe_ref[0:8, :] = new_conv
    else:
        cout_ref[0] = new_conv

    ba = ba_ref[...] if prompt else ba_ref[0]
    row = lax.broadcasted_iota(jnp.int32, (c, LANE), 0)
    beta_all = _sigmoid(ba)
    xg = ba + hp_ref[1:2, :]
    softplus = jnp.maximum(xg, 0.0) + jnp.log1p(jnp.exp(-jnp.abs(xg)))
    g_all = -jnp.exp(hp_ref[0:1, :]) * softplus
    if l_valid < c:
        beta_all = jnp.where(row < l_valid, beta_all, 0.0)
        g_all = jnp.where(row < l_valid, g_all, 0.0)
    gcum = g_all
    s = 1
    while s < c:
        gcum = gcum + jnp.where(row >= s, pltpu.roll(gcum, s, 0), 0.0)
        s *= 2
    gn = gn_ref[...]

    hg = group
    for h0 in range(0, GDN_HEADS, hg):
        hs = range(h0, h0 + hg)
        cols = [slice(h * GDN_D, (h + 1) * GDN_D) for h in hs]
        qh = [cb_ref[:, cl] for cl in cols]
        kh = [cb_ref[:, w + h * GDN_D:w + (h + 1) * GDN_D] for h in hs]
        vh = [cb_ref[:, 2 * w + h * GDN_D:2 * w + (h + 1) * GDN_D] for h in hs]
        qn = [x * lax.rsqrt(jnp.sum(x * x, axis=-1, keepdims=True) + EPS) * (GDN_D ** -0.5) for x in qh]
        kn = [x * lax.rsqrt(jnp.sum(x * x, axis=-1, keepdims=True) + EPS) for x in kh]
        bcol = [beta_all[:, h:h + 1] for h in hs]
        gcol = [gcum[:, h + GDN_HEADS:h + GDN_HEADS + 1] for h in hs]
        s_prev = [s_ref[h] if prompt else sin_ref[0, h] for h in hs]
        o, s_new = _gdn_heads(qn, kn, vh, bcol, gcol, s_prev, lvl_ref, c)
        for j, h in enumerate(hs):
            if prompt:
                s_ref[h] = s_new[j]
            else:
                sout_ref[0, h] = s_new[j]
        zh = [z_ref[:, cl] if prompt else z_ref[0, :, cl] for cl in cols]
        on = [x * lax.rsqrt(jnp.mean(x * x, axis=-1, keepdims=True) + EPS) * gn for x in o]
        for j, cl in enumerate(cols):
            res = (on[j] * _silu(zh[j])).astype(o_ref.dtype)
            if prompt:
                o_ref[:, cl] = res
            else:
                o_ref[0, :, cl] = res
    if prompt:
        @pl.when(i == n_chunks - 1)
        def _():
            sout_ref[...] = s_ref[...]


def _level_masks(c, l_valid):
    i = jnp.arange(c, dtype=jnp.int32)[:, None]
    j = jnp.arange(LANE, dtype=jnp.int32)[None, :]
    masks = []
    s = 1
    while s < l_valid:
        masks.append((i // (2 * s) == j // (2 * s)) & (i % (2 * s) >= s) & (j % (2 * s) < s))
        s *= 2
    return jnp.stack(masks).astype(F32)


def _gdn_prompt(proj, ba, conv_w, hp, gn, c=128):
    t = proj.shape[0]
    n_chunks = t // c
    w = GDN_W
    lvl = _level_masks(c, c)
    kern = functools.partial(_gdn_kernel, c=c, l_valid=c, prompt=True, n_chunks=n_chunks, group=GDN_GROUP_PROMPT)
    return pl.pallas_call(
        kern,
        grid=(n_chunks,),
        in_specs=[pl.BlockSpec((c, w), lambda i: (i, 0)),
                  pl.BlockSpec((c, w), lambda i: (i, 1)),
                  pl.BlockSpec((c, w), lambda i: (i, 2)),
                  pl.BlockSpec((c, w), lambda i: (i, 3)),
                  pl.BlockSpec((c, LANE), lambda i: (i, 0)),
                  pl.BlockSpec((CONV_W, 3 * w), lambda i: (0, 0)),
                  pl.BlockSpec((8, LANE), lambda i: (0, 0)),
                  pl.BlockSpec((1, GDN_D), lambda i: (0, 0)),
                  pl.BlockSpec(lvl.shape, lambda i: (0, 0, 0))],
        out_specs=[pl.BlockSpec((c, w), lambda i: (i, 0)),
                   pl.BlockSpec((GDN_HEADS, GDN_D, GDN_D), lambda i: (0, 0, 0)),
                   pl.BlockSpec((8, 3 * w), lambda i: (0, 0))],
        out_shape=[jax.ShapeDtypeStruct((t, w), BF16),
                   jax.ShapeDtypeStruct((GDN_HEADS, GDN_D, GDN_D), F32),
                   jax.ShapeDtypeStruct((8, 3 * w), F32)],
        scratch_shapes=[pltpu.VMEM((8 + c, 3 * w), F32),
                        pltpu.VMEM((c, 3 * w), F32),
                        pltpu.VMEM((GDN_HEADS, GDN_D, GDN_D), F32)],
        compiler_params=_cparams(("arbitrary",)),
        name="gdn_prompt",
    )(proj, proj, proj, proj, ba, conv_w, hp, gn, lvl)


def _gdn_sample(proj3, ba3, conv_w, hp, gn, conv_in, s_in, l):
    b, c, _ = proj3.shape
    w = GDN_W
    lvl = _level_masks(c, l)
    kern = functools.partial(_gdn_kernel, c=c, l_valid=l, prompt=False, n_chunks=1, group=GDN_GROUP_SAMPLE)
    return pl.pallas_call(
        kern,
        grid=(b,),
        in_specs=[pl.BlockSpec((1, c, w), lambda i: (i, 0, 0)),
                  pl.BlockSpec((1, c, w), lambda i: (i, 0, 1)),
                  pl.BlockSpec((1, c, w), lambda i: (i, 0, 2)),
                  pl.BlockSpec((1, c, w), lambda i: (i, 0, 3)),
                  pl.BlockSpec((1, c, LANE), lambda i: (i, 0, 0)),
                  pl.BlockSpec((CONV_W, 3 * w), lambda i: (0, 0)),
                  pl.BlockSpec((8, LANE), lambda i: (0, 0)),
                  pl.BlockSpec((1, GDN_D), lambda i: (0, 0)),
                  pl.BlockSpec(lvl.shape, lambda i: (0, 0, 0)),
                  pl.BlockSpec((1, 8, 3 * w), lambda i: (i, 0, 0)),
                  pl.BlockSpec((1, GDN_HEADS, GDN_D, GDN_D), lambda i: (i, 0, 0, 0))],
        out_specs=[pl.BlockSpec((1, c, w), lambda i: (i, 0, 0)),
                   pl.BlockSpec((1, GDN_HEADS, GDN_D, GDN_D), lambda i: (i, 0, 0, 0)),
                   pl.BlockSpec((1, 8, 3 * w), lambda i: (i, 0, 0))],
        out_shape=[jax.ShapeDtypeStruct((b, c, w), F32),
                   jax.ShapeDtypeStruct((b, GDN_HEADS, GDN_D, GDN_D), F32),
                   jax.ShapeDtypeStruct((b, 8, 3 * w), F32)],
        scratch_shapes=[pltpu.VMEM((8 + c, 3 * w), F32),
                        pltpu.VMEM((c, 3 * w), F32)],
        compiler_params=_cparams(("arbitrary",)),
        name="gdn_sample",
    )(proj3, proj3, proj3, proj3, ba3, conv_w, hp, gn, lvl, conv_in, s_in)


def _t5_bucket(dist):
    d = jnp.maximum(dist, 0)
    max_exact = N_BUCKETS // 2
    large = max_exact + (jnp.log(jnp.maximum(d, 1).astype(F32) / max_exact)
                         / math.log(WINDOW / max_exact) * (N_BUCKETS - max_exact)).astype(jnp.int32)
    large = jnp.minimum(large, N_BUCKETS - 1)
    return jnp.where(d < max_exact, d, large)


def _bias_table(dist, rel_bias):
    valid = (dist >= 0) & (dist < WINDOW)
    b = jnp.take(rel_bias.astype(F32).T, _t5_bucket(dist), axis=1)
    return jnp.where(valid[None], b, NEG)


def _lo_hi(slab, g):
    lane = lax.broadcasted_iota(jnp.int32, slab.shape, 1)
    if g % 2 == 0:
        lo = jnp.where(lane < SWA_HD, slab, 0.0)
        hi = pltpu.roll(lo, SWA_HD, 1)
    else:
        hi = jnp.where(lane >= SWA_HD, slab, 0.0)
        lo = pltpu.roll(hi, SWA_HD, 1)
    return lo, hi


def _sink_softmax_chains(parts, sinks):
    cs = range(len(parts))
    m = [sinks[i] for i in cs]
    for j in range(len(parts[0])):
        mx = [jnp.max(parts[i][j], axis=-1, keepdims=True) for i in cs]
        m = [jnp.maximum(m[i], mx[i]) for i in cs]
    es = [[jnp.exp(p - m[i]) for p in parts[i]] for i in cs]
    den = [jnp.exp(sinks[i] - m[i]) for i in cs]
    for j in range(len(parts[0])):
        sm = [jnp.sum(es[i][j], axis=-1, keepdims=True) for i in cs]
        den = [den[i] + sm[i] for i in cs]
    inv = [1.0 / d for d in den]
    return [[e * inv[i] for e in es[i]] for i in cs]


SWA_GROUP_PROMPT = 2


def _swa_prompt_kernel(q_ref, kp_ref, kc_ref, vp_ref, vc_ref, bias_ref, sink_ref, o_ref):
    n = pl.program_id(0)
    wq = WINDOW
    col = lax.broadcasted_iota(jnp.int32, (wq, 2 * wq), 1)
    first = jnp.logical_and(n == 0, col < wq)
    sink_all = sink_ref[...]
    scale = SWA_HD ** -0.5
    for g0 in range(0, SWA_KV, SWA_GROUP_PROMPT):
        gs = range(g0, g0 + SWA_GROUP_PROMPT)
        kv = {}
        for g in gs:
            sl = slice((g // 2) * LANE, (g // 2 + 1) * LANE)
            kslab = jnp.concatenate([kp_ref[:, sl], kc_ref[:, sl]], axis=0)
            vslab = jnp.concatenate([vp_ref[:, sl], vc_ref[:, sl]], axis=0)
            kv[g] = ([x.astype(BF16) for x in _lo_hi(kslab, g)], [x.astype(BF16) for x in _lo_hi(vslab, g)])
        chains = [(g, s, half) for g in gs for s in range(2) for half in range(2)]
        qs = {(g, s): q_ref[:, (2 * g + s) * LANE:(2 * g + s + 1) * LANE].astype(BF16) for g in gs for s in range(2)}
        logits = [_dot_nt(qs[g, s], kv[g][0][half]) for g, s, half in chains]
        logits = [jnp.where(first, NEG, logits[i] * scale + bias_ref[4 * g + 2 * s + half])
                  for i, (g, s, half) in enumerate(chains)]
        sinks = [sink_all[:, 4 * g + 2 * s + half:4 * g + 2 * s + half + 1] for g, s, half in chains]
        probs = _sink_softmax_chains([[x] for x in logits], sinks)
        pv = [_dot(probs[i][0].astype(BF16), kv[g][1][half]) for i, (g, s, half) in enumerate(chains)]
        for i, (g, s, half) in enumerate(chains):
            if half == 0:
                o_ref[:, (2 * g + s) * LANE:(2 * g + s + 1) * LANE] = (pv[i] + pv[i + 1]).astype(o_ref.dtype)


def _swa_prompt(proj, qcol, bias, sinks):
    t = proj.shape[0]
    nb = t // WINDOW
    kb = qcol * (SWA_W // SWA_KVW) + SWA_W // SWA_KVW
    return pl.pallas_call(
        _swa_prompt_kernel,
        grid=(nb,),
        in_specs=[pl.BlockSpec((WINDOW, SWA_W), lambda n: (n, qcol)),
                  pl.BlockSpec((WINDOW, SWA_KVW), lambda n: (jnp.maximum(n - 1, 0), kb)),
                  pl.BlockSpec((WINDOW, SWA_KVW), lambda n: (n, kb)),
                  pl.BlockSpec((WINDOW, SWA_KVW), lambda n: (jnp.maximum(n - 1, 0), kb + 1)),
                  pl.BlockSpec((WINDOW, SWA_KVW), lambda n: (n, kb + 1)),
                  pl.BlockSpec((SWA_HEADS, WINDOW, 2 * WINDOW), lambda n: (0, 0, 0)),
                  pl.BlockSpec((1, LANE), lambda n: (0, 0))],
        out_specs=pl.BlockSpec((WINDOW, SWA_W), lambda n: (n, 0)),
        out_shape=jax.ShapeDtypeStruct((t, SWA_W), BF16),
        compiler_params=_cparams(("arbitrary",)),
        name="swa_prompt",
    )(proj, proj, proj, proj, proj, bias, sinks)


def _swa_sample_kernel(q_ref, kn_ref, vn_ref, kc_ref, vc_ref, bc_ref, bn_ref, sink_ref, o_ref):
    r = 8
    sink_all = sink_ref[...]
    q = q_ref[0]
    knew = kn_ref[0]
    vnew = vn_ref[0]
    scale = SWA_HD ** -0.5
    gs = range(SWA_KV)
    chains = [(g, half) for g in gs for half in range(2)]
    sls = [slice((g // 2) * LANE, (g // 2 + 1) * LANE) for g in gs]
    kc = [[x.astype(BF16) for x in _lo_hi(kc_ref[0, :, sls[g]], g)] for g in gs]
    vc = [[x.astype(BF16) for x in _lo_hi(vc_ref[0, :, sls[g]], g)] for g in gs]
    kn = [[_rowpad(x, LANE).astype(BF16) for x in _lo_hi(knew[:, sls[g]], g)] for g in gs]
    vn = [[_rowpad(x, LANE).astype(BF16) for x in _lo_hi(vnew[:, sls[g]], g)] for g in gs]
    qs = [jnp.concatenate([q[:, (2 * g) * LANE:(2 * g + 1) * LANE],
                           q[:, (2 * g + 1) * LANE:(2 * g + 2) * LANE]], axis=0).astype(BF16) for g in gs]
    lc = [_dot_nt(qs[g], kc[g][half]) for g, half in chains]
    ln = [_dot_nt(qs[g], kn[g][half]) for g, half in chains]
    lc = [lc[i] * scale + bc_ref[2 * g + half] for i, (g, half) in enumerate(chains)]
    ln = [ln[i] * scale + bn_ref[2 * g + half] for i, (g, half) in enumerate(chains)]
    sinks = [jnp.concatenate([jnp.broadcast_to(sink_all[:, 4 * g + half:4 * g + half + 1], (r, 1)),
                              jnp.broadcast_to(sink_all[:, 4 * g + 2 + half:4 * g + 2 + half + 1], (r, 1))], axis=0)
             for g, half in chains]
    probs = _sink_softmax_chains([[lc[i], ln[i]] for i in range(len(chains))], sinks)
    pvc = [_dot(probs[i][0].astype(BF16), vc[g][half]) for i, (g, half) in enumerate(chains)]
    pvn = [_dot(probs[i][1].astype(BF16), vn[g][half]) for i, (g, half) in enumerate(chains)]
    for g in gs:
        acc = (pvc[2 * g] + pvn[2 * g]) + (pvc[2 * g + 1] + pvn[2 * g + 1])
        o_ref[0, :, (2 * g) * LANE:(2 * g + 1) * LANE] = acc[:r]
        o_ref[0, :, (2 * g + 1) * LANE:(2 * g + 2) * LANE] = acc[r:]


def _swa_sample(proj3, qcol, k_cache, v_cache, bias_c, bias_n, sinks):
    b, r, _ = proj3.shape
    kb = qcol * (SWA_W // SWA_KVW) + SWA_W // SWA_KVW
    return pl.pallas_call(
        _swa_sample_kernel,
        grid=(b,),
        in_specs=[pl.BlockSpec((1, r, SWA_W), lambda i: (i, 0, qcol)),
                  pl.BlockSpec((1, r, SWA_KVW), lambda i: (i, 0, kb)),
                  pl.BlockSpec((1, r, SWA_KVW), lambda i: (i, 0, kb + 1)),
                  pl.BlockSpec((1, WINDOW, SWA_KVW), lambda i: (i, 0, 0)),
                  pl.BlockSpec((1, WINDOW, SWA_KVW), lambda i: (i, 0, 0)),
                  pl.BlockSpec((2 * SWA_KV, 2 * r, LANE), lambda i: (0, 0, 0)),
                  pl.BlockSpec((2 * SWA_KV, 2 * r, LANE), lambda i: (0, 0, 0)),
                  pl.BlockSpec((1, LANE), lambda i: (0, 0))],
        out_specs=pl.BlockSpec((1, r, SWA_W), lambda i: (i, 0, 0)),
        out_shape=jax.ShapeDtypeStruct((b, r, SWA_W), F32),
        compiler_params=_cparams(("arbitrary",)),
        name="swa_sample",
    )(proj3, proj3, proj3, k_cache, v_cache, bias_c, bias_n, sinks)


MOE_WCHUNK = 1024
MOE_WSLOTS = 4
MOE_LOOKAHEAD = 3


def _moe_kernel(ord_ref, be_ref, bs_ref, bn_ref, h_hbm, wg_hbm, wu_hbm, wd_hbm, y_hbm,
                xbuf, x16, hgu, hid16, acc, ybuf, wbuf, gsem, ssem, wsem, *, n_blocks, t_all):
    k = pl.program_id(0)
    n = bn_ref[k]
    slot = lax.rem(k, 2)
    d = x16.shape[1]
    half = d // 2
    kc = MOE_WCHUNK // 2
    n_gu = d // kc
    n_dn = d // MOE_WCHUNK
    n_ch = n_gu + n_dn
    assert n_ch % MOE_WSLOTS == 0 and wg_hbm.shape[2] == MOE_WCHUNK

    def gather_copy(tok, r):
        return pltpu.make_async_copy(h_hbm.at[pl.ds(tok, 1)], xbuf.at[pl.ds(r, 1)], gsem)

    def scatter_copy(sl, r, a):
        return pltpu.make_async_copy(ybuf.at[sl, pl.ds(r, 1)], y_hbm.at[pl.ds(a, 1)], ssem.at[sl])

    def w_copies(e, j):
        s = j % MOE_WSLOTS
        if j < n_gu:
            rows = pl.ds(j * kc, kc)
            return [pltpu.make_async_copy(wg_hbm.at[e, rows, :], wbuf.at[s, pl.ds(0, kc), :], wsem.at[s]),
                    pltpu.make_async_copy(wu_hbm.at[e, rows, :], wbuf.at[s, pl.ds(kc, kc), :], wsem.at[s])]
        cols = pl.ds((j - n_gu) * MOE_WCHUNK, MOE_WCHUNK)
        return [pltpu.make_async_copy(wd_hbm.at[e, :, cols], wbuf.at[s], wsem.at[s])]

    def gather_start(kk):
        st = bs_ref[kk]

        def body(r, carry):
            a = ord_ref[st + r]
            gather_copy(jnp.where(a >= t_all, a - t_all, a), r).start()
            return carry
        lax.fori_loop(0, bn_ref[kk], body, 0)

    def gather_wait(kk):
        def body(r, carry):
            gather_copy(0, 0).wait()
            return carry
        lax.fori_loop(0, bn_ref[kk], body, 0)

    def scatter_start(kk, sl):
        st = bs_ref[kk]

        def body(r, carry):
            scatter_copy(sl, r, ord_ref[st + r]).start()
            return carry
        lax.fori_loop(0, bn_ref[kk], body, 0)

    def scatter_wait(kk, sl):
        def body(r, carry):
            scatter_copy(sl, 0, 0).wait()
            return carry
        lax.fori_loop(0, bn_ref[kk], body, 0)

    @pl.when(n > 0)
    def _():
        e = be_ref[k]
        k1 = jnp.minimum(k + 1, n_blocks - 1)
        has_next = jnp.logical_and(k + 1 < n_blocks, bn_ref[k1] > 0)
        e_next = be_ref[k1]

        @pl.when(k == 0)
        def _():
            xbuf[...] = jnp.zeros(xbuf.shape, xbuf.dtype)
            gather_start(0)
            for j in range(MOE_LOOKAHEAD):
                for cp in w_copies(e, j):
                    cp.start()
        gather_wait(k)
        words = xbuf[...]
        lo = lax.bitcast_convert_type(lax.shift_left(words, jnp.uint32(16)), F32)
        hi = lax.bitcast_convert_type(words & jnp.uint32(0xFFFF0000), F32)
        x16[:, 0:half] = lo.astype(BF16)
        x16[:, half:d] = hi.astype(BF16)

        @pl.when(has_next)
        def _():
            gather_start(k1)

        for j in range(n_ch):
            jn = j + MOE_LOOKAHEAD
            if jn < n_ch:
                for cp in w_copies(e, jn):
                    cp.start()
            else:
                @pl.when(has_next)
                def _():
                    for cp in w_copies(e_next, jn - n_ch):
                        cp.start()
            for cp in w_copies(e, j):
                cp.wait()
            s = j % MOE_WSLOTS
            if j < n_gu:
                xs = x16[:, j * kc:(j + 1) * kc]
                pg = _dot(xs, wbuf[s, 0:kc, :].astype(BF16))
                pu = _dot(xs, wbuf[s, kc:2 * kc, :].astype(BF16))
                if j == 0:
                    hgu[0] = pg
                    hgu[1] = pu
                else:
                    hgu[0] += pg
                    hgu[1] += pu
                if j == n_gu - 1:
                    hid16[...] = (_silu(hgu[0]) * hgu[1]).astype(BF16)
            else:
                cols = slice((j - n_gu) * MOE_WCHUNK, (j - n_gu + 1) * MOE_WCHUNK)
                acc[:, cols] = _dot(hid16[...], wbuf[s].astype(BF16))

        bits = lax.bitcast_convert_type(acc[...].astype(BF16).astype(F32), jnp.uint32)
        ybuf[slot] = lax.shift_right_logical(bits[:, :half], jnp.uint32(16)) | (bits[:, half:] & jnp.uint32(0xFFFF0000))
        scatter_start(k, slot)

        @pl.when(k > 0)
        def _():
            scatter_wait(k - 1, 1 - slot)

        @pl.when(jnp.logical_not(has_next))
        def _():
            scatter_wait(k, slot)


def _moe_ffn(order, blk_e, blk_start, blk_n, h_packed, w_gate, w_up, w_down):
    a = order.shape[0]
    t_all = h_packed.shape[0]
    d = 2 * h_packed.shape[1]
    n_blocks = blk_e.shape[0]
    grid_spec = pltpu.PrefetchScalarGridSpec(
        num_scalar_prefetch=4,
        grid=(n_blocks,),
        in_specs=[pl.BlockSpec(memory_space=pl.ANY)] * 4,
        out_specs=pl.BlockSpec(memory_space=pl.ANY),
        scratch_shapes=[pltpu.VMEM((MOE_ROWS, d // 2), jnp.uint32),
                        pltpu.VMEM((MOE_ROWS, d), BF16),
                        pltpu.VMEM((2, MOE_ROWS, D_EXPERT), F32),
                        pltpu.VMEM((MOE_ROWS, D_EXPERT), BF16),
                        pltpu.VMEM((MOE_ROWS, d), F32),
                        pltpu.VMEM((2, MOE_ROWS, d // 2), jnp.uint32),
                        pltpu.VMEM((MOE_WSLOTS, MOE_WCHUNK, MOE_WCHUNK), F32),
                        pltpu.SemaphoreType.DMA(()),
                        pltpu.SemaphoreType.DMA((2,)),
                        pltpu.SemaphoreType.DMA((MOE_WSLOTS,))],
    )
    return pl.pallas_call(
        functools.partial(_moe_kernel, n_blocks=n_blocks, t_all=t_all),
        grid_spec=grid_spec,
        out_shape=jax.ShapeDtypeStruct((a, d // 2), jnp.uint32),
        compiler_params=_cparams(("arbitrary",)),
        name="moe_ffn",
    )(order, blk_e, blk_start, blk_n, h_packed, w_gate, w_up, w_down)


def _unpack_pairs(words):
    lo = lax.bitcast_convert_type(lax.shift_left(words, jnp.uint32(16)), F32)
    hi = lax.bitcast_convert_type(words & jnp.uint32(0xFFFF0000), F32)
    return jnp.concatenate([lo, hi], axis=1)


def _final_kernel(x_ref, y0_ref, y1_ref, p_ref, g_ref, nf_ref, o_ref):
    p = p_ref[...]
    moe = _unpack_pairs(y0_ref[...]) * p[:, 0:1] + _unpack_pairs(y1_ref[...]) * p[:, 1:2]
    x = x_ref[...] + g_ref[...] * moe
    o_ref[...] = x * lax.rsqrt(jnp.mean(x * x, axis=-1, keepdims=True) + EPS) * nf_ref[...]


def _final(x, y, row0, gates, gt2, nf, tm=256):
    t, d = x.shape
    t_all = gates.shape[0]
    tm = math.gcd(math.gcd(tm, t), math.gcd(row0, t_all))
    b0 = row0 // tm
    b1 = t_all // tm + b0
    return pl.pallas_call(
        _final_kernel,
        grid=(t // tm,),
        in_specs=[pl.BlockSpec((tm, d), lambda i: (i, 0)),
                  pl.BlockSpec((tm, d // 2), lambda i: (b0 + i, 0)),
                  pl.BlockSpec((tm, d // 2), lambda i: (b1 + i, 0)),
                  pl.BlockSpec((tm, LANE), lambda i: (b0 + i, 0)),
                  _mod_spec(gt2, tm, d),
                  pl.BlockSpec((1, d), lambda i: (0, 0))],
        out_specs=pl.BlockSpec((tm, d), lambda i: (i, 0)),
        out_shape=jax.ShapeDtypeStruct((t, d), F32),
        compiler_params=_cparams(("arbitrary",)),
        name="final",
    )(x, y, y, gates, gt2, nf.reshape(1, d))


def _dispatch(expert_idx):
    t, k = expert_idx.shape
    a = t * k
    n_blocks = a // MOE_ROWS + N_EXPERTS
    flat_e = expert_idx.T.reshape(a)
    order = jnp.argsort(flat_e, stable=True).astype(jnp.int32)
    counts = jnp.zeros((N_EXPERTS,), jnp.int32).at[flat_e].add(1)
    seg_start = jnp.cumsum(counts) - counts
    nblk = (counts + MOE_ROWS - 1) // MOE_ROWS
    blk_end = jnp.cumsum(nblk)
    n_active = blk_end[-1]
    kk = jnp.minimum(jnp.arange(n_blocks, dtype=jnp.int32), n_active - 1)
    blk_e = jnp.minimum(jnp.searchsorted(blk_end, kk, side='right'), N_EXPERTS - 1).astype(jnp.int32)
    j = kk - (blk_end[blk_e] - nblk[blk_e])
    blk_start = (seg_start[blk_e] + j * MOE_ROWS).astype(jnp.int32)
    blk_n = jnp.clip(counts[blk_e] - j * MOE_ROWS, 0, MOE_ROWS)
    blk_n = jnp.where(jnp.arange(n_blocks) < n_active, blk_n, 0).astype(jnp.int32)
    return order, blk_e, blk_start, blk_n


def kernel(x_prompt, x_sample, state_gdn, state_conv, cache_swa_k, cache_swa_v, c_prompt, c_sample, w_ada, b_ada, norm_mix, w_in, conv_w, a_log, dt_bias, gdn_norm, swa_sinks, rel_bias, w_out, norm_moe, router_group, router_group_bias, router_expert, router_expert_bias, w_gate, w_up, w_down, norm_final):
    depth = w_ada.shape[0]
    assert depth == 1
    bp, seq, d = x_prompt.shape
    assert bp == 1 and seq % WINDOW == 0
    bs, ls, _ = x_sample.shape
    tp = bp * seq
    ts = bs * ls
    n_main = 2 * GDN_W + 2 * GDN_W
    n_ba = 2 * GDN_HEADS

    c_all = jnp.concatenate([c_prompt, c_sample], axis=0)
    m_pad = -(-c_all.shape[0] // 16) * 16
    c_all = jnp.pad(c_all, ((0, m_pad - c_all.shape[0]), (0, 0)))
    mod = _adaln(c_all, w_ada[0], b_ada[0])
    mod_p = [mod[0:1, j * d:(j + 1) * d] for j in range(6)]
    mod_s = [jnp.repeat(mod[1:1 + bs, j * d:(j + 1) * d], ls, axis=0) for j in range(6)]

    w_cat = jnp.concatenate([w_in[0][:, :n_main], w_in[0][:, n_main + n_ba:]], axis=1).astype(BF16)
    w_ba = jnp.pad(w_in[0][:, n_main:n_main + n_ba], ((0, 0), (0, LANE - n_ba))).astype(BF16)
    w_out16 = w_out[0].astype(BF16)
    qcol = n_main // SWA_W
    hp = jnp.zeros((8, LANE), F32)
    hp = hp.at[0, GDN_HEADS:2 * GDN_HEADS].set(a_log[0]).at[1, GDN_HEADS:2 * GDN_HEADS].set(dt_bias[0])
    gn = gdn_norm[0].reshape(1, GDN_D)
    sinks = jnp.pad(swa_sinks[0], (0, LANE - SWA_HEADS)).reshape(1, LANE)

    xp = x_prompt.reshape(tp, d)
    h_p = _norm_mod(xp, norm_mix[0], mod_p[1], mod_p[0])
    proj_p = _matmul(h_p, w_cat, 1024, 512, name="inproj_p")
    ba_p = _matmul(h_p, w_ba, 1024, LANE, name="inproj_ba_p")
    og_p, s_p, conv_p = _gdn_prompt(proj_p, ba_p, conv_w[0], hp, gn)
    conv_p = conv_p[8 - (CONV_W - 1):]
    qi = jnp.arange(WINDOW, dtype=jnp.int32)
    sj = jnp.arange(2 * WINDOW, dtype=jnp.int32)
    bias_p = _bias_table((WINDOW + qi)[:, None] - sj[None, :], rel_bias)
    os_p = _swa_prompt(proj_p, qcol, bias_p, sinks)
    kcol = n_main + SWA_W
    k_p = proj_p[tp - WINDOW:, kcol:kcol + SWA_KVW]
    v_p = proj_p[tp - WINDOW:, kcol + SWA_KVW:kcol + 2 * SWA_KVW]
    x1_p = _outproj(og_p, os_p, w_out16, xp, mod_p[2])

    xs = x_sample.reshape(ts, d)
    h_s = _norm_mod(xs, norm_mix[0], mod_s[1], mod_s[0])
    proj_s = _matmul(h_s, w_cat, 512, 512, name="inproj_s")
    ba_s = _matmul(h_s, w_ba, 512, LANE, name="inproj_ba_s")
    rs = 8
    assert CONV_W - 1 <= ls <= rs
    proj_s3 = proj_s.reshape(bs, ls, proj_s.shape[1])
    proj_s8 = jnp.pad(proj_s3, ((0, 0), (0, rs - ls), (0, 0)))
    ba_s8 = jnp.pad(ba_s.reshape(bs, ls, LANE), ((0, 0), (0, rs - ls), (0, 0)))
    conv_in8 = jnp.pad(state_conv[0], ((0, 0), (8 - (CONV_W - 1), 0), (0, 0)))
    og_s, s_s, conv_s = _gdn_sample(proj_s8, ba_s8, conv_w[0], hp, gn, conv_in8, state_gdn[0], ls)
    conv_s = conv_s[:, 8 - (CONV_W - 1):]
    wb = cache_swa_k.shape[2]
    assert wb == WINDOW == LANE
    dist_s = (wb + jnp.arange(ls, dtype=jnp.int32))[:, None] - jnp.arange(wb + ls, dtype=jnp.int32)[None, :]
    bias_s = _bias_table(dist_s, rel_bias)
    bias_s = jnp.pad(bias_s, ((0, 0), (0, rs - ls), (0, 2 * LANE - wb - ls)), constant_values=NEG)
    bias_s = bias_s.reshape(SWA_KV, 2, 2, rs, 2 * LANE)
    bias_s = jnp.transpose(bias_s, (0, 2, 1, 3, 4)).reshape(2 * SWA_KV, 2 * rs, 2 * LANE)
    k_cache = cache_swa_k[0].reshape(bs, wb, SWA_KVW)
    v_cache = cache_swa_v[0].reshape(bs, wb, SWA_KVW)
    os_s = _swa_sample(proj_s8, qcol, k_cache, v_cache, bias_s[:, :, :LANE], bias_s[:, :, LANE:], sinks)
    k_s = jnp.concatenate([k_cache[:, ls:], proj_s3[:, :, kcol:kcol + SWA_KVW]], axis=1)
    v_s = jnp.concatenate([v_cache[:, ls:], proj_s3[:, :, kcol + SWA_KVW:kcol + 2 * SWA_KVW]], axis=1)
    x1_s = _outproj(og_s[:, :ls].reshape(ts, GDN_W).astype(BF16), os_s[:, :ls].reshape(ts, SWA_W).astype(BF16),
                    w_out16, xs, mod_s[2], tm=512)

    wr = jnp.concatenate([router_group[0], router_expert[0]], axis=1)
    wr = jnp.pad(wr, ((0, 0), (0, LANE - wr.shape[1])))
    wr_hi = wr.astype(BF16)
    wr_mid = (wr - wr_hi.astype(F32)).astype(BF16)
    wr2 = jnp.stack([wr_hi, wr_mid])
    br = jnp.pad(jnp.concatenate([router_group_bias[0], router_expert_bias[0]]),
                 (0, LANE - N_GROUPS - N_EXPERTS)).reshape(1, LANE)
    t_all = tp + ts
    h2, eidx, gates = _norm_router(x1_p, x1_s, norm_moe[0], mod_p[4], mod_p[3], mod_s[4], mod_s[3], wr2, br)
    order, blk_e, blk_start, blk_n = _dispatch(eidx[:, :2])
    y = _moe_ffn(order, blk_e, blk_start, blk_n, h2, w_gate[0], w_up[0], w_down[0])
    y_p = _final(x1_p, y, 0, gates, mod_p[5], norm_final)
    y_s = _final(x1_s, y, tp, gates, mod_s[5], norm_final)

    sdt = state_gdn.dtype
    return (y_p.reshape(bp, seq, d), y_s.reshape(bs, ls, d),
            s_p.reshape(1, bp, GDN_HEADS, GDN_D, GDN_D).astype(sdt), s_s[None].astype(sdt),
            conv_p.reshape(1, bp, CONV_W - 1, 3 * GDN_W).astype(state_conv.dtype), conv_s[None].astype(state_conv.dtype),
            k_p.reshape(1, bp, WINDOW, SWA_KV, SWA_HD).astype(cache_swa_k.dtype),
            k_s.reshape(1, bs, wb, SWA_KV, SWA_HD).astype(cache_swa_k.dtype),
            v_p.reshape(1, bp, WINDOW, SWA_KV, SWA_HD).astype(cache_swa_v.dtype),
            v_s.reshape(1, bs, wb, SWA_KV, SWA_HD).astype(cache_swa_v.dtype))
```

```python
import functools
import math

import jax
import jax.numpy as jnp
from jax import lax
from jax.experimental import pallas as pl
from jax.experimental.pallas import tpu as pltpu

F32 = jnp.float32
BF16 = jnp.bfloat16
EPS = 1e-6
NEG = -1e30

LANE = 128
GDN_HEADS = 16
GDN_D = 128
GDN_W = GDN_HEADS * GDN_D
CONV_W = 4
SWA_HEADS = 32
SWA_KV = 8
SWA_HD = 64
SWA_W = SWA_HEADS * SWA_HD
SWA_KVW = SWA_KV * SWA_HD
WINDOW = 128
N_BUCKETS = 32
N_GROUPS = 8
EPG = 8
N_EXPERTS = 64
D_EXPERT = 1024
MOE_ROWS = 448
VMEM_LIMIT = 56 * 1024 * 1024


def _cparams(sem):
    return pltpu.CompilerParams(dimension_semantics=sem, vmem_limit_bytes=VMEM_LIMIT)


def _sigmoid(x):
    return 1.0 / (1.0 + jnp.exp(-x))


def _silu(x):
    return x * _sigmoid(x)


def _dot(a, b):
    return jnp.dot(a, b, preferred_element_type=F32)


def _dot_nt(a, b):
    return lax.dot_general(a, b, (((1,), (1,)), ((), ())), preferred_element_type=F32)


def _dot_tn(a, b):
    return lax.dot_general(a, b, (((0,), (0,)), ((), ())), preferred_element_type=F32)


def _ada_kernel(c_ref, w_ref, b_ref, o_ref):
    a = _silu(c_ref[...]).astype(BF16)
    o_ref[...] = _dot(a, w_ref[...].astype(BF16)) + b_ref[...]


def _adaln(c, w, b, tn=512):
    m, d = c.shape
    n = w.shape[1]
    return pl.pallas_call(
        _ada_kernel,
        grid=(n // tn,),
        in_specs=[pl.BlockSpec((m, d), lambda j: (0, 0)),
                  pl.BlockSpec((d, tn), lambda j: (0, j)),
                  pl.BlockSpec((1, tn), lambda j: (0, j))],
        out_specs=pl.BlockSpec((m, tn), lambda j: (0, j)),
        out_shape=jax.ShapeDtypeStruct((m, n), F32),
        compiler_params=_cparams(("arbitrary",)),
        name="adaln",
    )(c, w, b.reshape(1, n))


def _norm_kernel(x_ref, g_ref, sc_ref, sh_ref, o_ref):
    x = x_ref[...]
    y = x * lax.rsqrt(jnp.mean(x * x, axis=-1, keepdims=True) + EPS) * g_ref[...]
    o_ref[...] = (y * (1.0 + sc_ref[...]) + sh_ref[...]).astype(o_ref.dtype)


def _mod_spec(mod, tm, d):
    if mod.shape[0] == 1:
        return pl.BlockSpec((1, d), lambda i: (0, 0))
    return pl.BlockSpec((tm, d), lambda i: (i, 0))


def _norm_mod(x, g, sc, sh, tm=256):
    t, d = x.shape
    tm = min(tm, t)
    return pl.pallas_call(
        _norm_kernel,
        grid=(t // tm,),
        in_specs=[pl.BlockSpec((tm, d), lambda i: (i, 0)),
                  pl.BlockSpec((1, d), lambda i: (0, 0)),
                  _mod_spec(sc, tm, d), _mod_spec(sh, tm, d)],
        out_specs=pl.BlockSpec((tm, d), lambda i: (i, 0)),
        out_shape=jax.ShapeDtypeStruct((t, d), BF16),
        compiler_params=_cparams(("arbitrary",)),
        name="norm_mod",
    )(x, g.reshape(1, d), sc, sh)


def _norm_router_kernel(*refs, nba):
    (xa_ref, xb_ref, g_ref, sca_ref, sha_ref, scb_ref, shb_ref, wr_ref, br_ref,
     o_ref, e_ref, p_ref) = refs
    first = pl.program_id(0) < nba
    x = jnp.where(first, xa_ref[...], xb_ref[...])
    sc = jnp.where(first, sca_ref[...], scb_ref[...])
    sh = jnp.where(first, sha_ref[...], shb_ref[...])
    y = x * lax.rsqrt(jnp.mean(x * x, axis=-1, keepdims=True) + EPS) * g_ref[...]
    t = y * (1.0 + sc) + sh
    t16 = t.astype(BF16)
    half = t.shape[1] // 2
    bits = lax.bitcast_convert_type(t16.astype(F32), jnp.uint32)
    o_ref[...] = lax.shift_right_logical(bits[:, :half], jnp.uint32(16)) | (bits[:, half:] & jnp.uint32(0xFFFF0000))
    lg = _dot(t16, wr_ref[...]) + br_ref[...]
    lane = lax.broadcasted_iota(jnp.int32, lg.shape, 1)
    big = jnp.int32(4 * LANE)
    lgrp = jnp.where(lane < N_GROUPS, lg, NEG)
    mg = jnp.max(lgrp, axis=-1, keepdims=True)
    grp = jnp.min(jnp.where(lgrp == mg, lane, big), axis=-1, keepdims=True)
    p_grp = 1.0 / jnp.sum(jnp.exp(lgrp - mg), axis=-1, keepdims=True)
    lo = N_GROUPS + grp * EPG
    emask = jnp.logical_and(lane >= lo, lane < lo + EPG)
    le = jnp.where(emask, lg, NEG)
    m1 = jnp.max(le, axis=-1, keepdims=True)
    i1 = jnp.min(jnp.where(le == m1, lane, big), axis=-1, keepdims=True)
    le2 = jnp.where(lane == i1, NEG, le)
    m2 = jnp.max(le2, axis=-1, keepdims=True)
    i2 = jnp.min(jnp.where(le2 == m2, lane, big), axis=-1, keepdims=True)
    e2 = jnp.exp(m2 - m1)
    w1 = p_grp / (1.0 + e2)
    w2 = p_grp * e2 / (1.0 + e2)
    e_ref[...] = jnp.where(lane == 0, i1 - N_GROUPS, jnp.where(lane == 1, i2 - N_GROUPS, 0))
    p_ref[...] = jnp.where(lane == 0, w1, jnp.where(lane == 1, w2, 0.0))


def _norm_router(xa, xb, g, sca, sha, scb, shb, wr, br, tm=256):
    ta, d = xa.shape
    tb = xb.shape[0]
    tm = min(tm, math.gcd(ta, tb))
    nba = ta // tm
    nbb = tb // tm
    t_all = ta + tb
    spec_a = pl.BlockSpec((tm, d), lambda i: (jnp.minimum(i, nba - 1), 0))
    spec_b = pl.BlockSpec((tm, d), lambda i: (jnp.maximum(i - nba, 0), 0))
    row = pl.BlockSpec((1, d), lambda i: (0, 0))
    return pl.pallas_call(
        functools.partial(_norm_router_kernel, nba=nba),
        grid=(nba + nbb,),
        in_specs=[spec_a, spec_b, row, row, row, spec_b, spec_b,
                  pl.BlockSpec((d, LANE), lambda i: (0, 0)),
                  pl.BlockSpec((1, LANE), lambda i: (0, 0))],
        out_specs=[pl.BlockSpec((tm, d // 2), lambda i: (i, 0)),
                   pl.BlockSpec((tm, LANE), lambda i: (i, 0)),
                   pl.BlockSpec((tm, LANE), lambda i: (i, 0))],
        out_shape=[jax.ShapeDtypeStruct((t_all, d // 2), jnp.uint32),
                   jax.ShapeDtypeStruct((t_all, LANE), jnp.int32),
                   jax.ShapeDtypeStruct((t_all, LANE), F32)],
        compiler_params=_cparams(("arbitrary",)),
        name="norm_router",
    )(xa, xb, g.reshape(1, d), sca, sha, scb, shb, wr, br)


def _mm_kernel(x_ref, w_ref, o_ref):
    o_ref[...] = _dot(x_ref[...], w_ref[...]).astype(o_ref.dtype)


def _matmul(x, w, tm, tn, out_dtype=F32, name="matmul"):
    m, k = x.shape
    n = w.shape[1]
    tm = min(tm, m)
    tn = min(tn, n)
    return pl.pallas_call(
        _mm_kernel,
        grid=(m // tm, n // tn),
        in_specs=[pl.BlockSpec((tm, k), lambda i, j: (i, 0)),
                  pl.BlockSpec((k, tn), lambda i, j: (0, j))],
        out_specs=pl.BlockSpec((tm, tn), lambda i, j: (i, j)),
        out_shape=jax.ShapeDtypeStruct((m, n), out_dtype),
        compiler_params=_cparams(("arbitrary", "arbitrary")),
        name=name,
    )(x, w)


def _mm_f32w_kernel(x_ref, w_ref, o_ref):
    o_ref[...] = _dot(x_ref[...], w_ref[...].astype(BF16))


def _matmul_f32w(x, w, n, tm, tn, name):
    m, k = x.shape
    tm = min(tm, m)
    return pl.pallas_call(
        _mm_f32w_kernel,
        grid=(m // tm, n // tn),
        in_specs=[pl.BlockSpec((tm, k), lambda i, j: (i, 0)),
                  pl.BlockSpec((k, tn), lambda i, j: (0, j))],
        out_specs=pl.BlockSpec((tm, tn), lambda i, j: (i, j)),
        out_shape=jax.ShapeDtypeStruct((m, n), F32),
        compiler_params=_cparams(("arbitrary", "arbitrary")),
        name=name,
    )(x, w)


def _outproj_kernel(a_ref, b_ref, wa_ref, wb_ref, x_ref, g_ref, o_ref):
    mix = _dot(a_ref[...], wa_ref[...]) + _dot(b_ref[...], wb_ref[...])
    o_ref[...] = x_ref[...] + g_ref[...] * mix


def _outproj(a, b, w, x, gate, tm=1024, tn=512):
    m, ka = a.shape
    kb = b.shape[1]
    n = w.shape[1]
    tm = min(tm, m)
    if gate.shape[0] == 1:
        gspec = pl.BlockSpec((1, tn), lambda i, j: (0, j))
    else:
        gspec = pl.BlockSpec((tm, tn), lambda i, j: (i, j))
    return pl.pallas_call(
        _outproj_kernel,
        grid=(m // tm, n // tn),
        in_specs=[pl.BlockSpec((tm, ka), lambda i, j: (i, 0)),
                  pl.BlockSpec((tm, kb), lambda i, j: (i, 0)),
                  pl.BlockSpec((ka, tn), lambda i, j: (0, j)),
                  pl.BlockSpec((kb, tn), lambda i, j: (1, j)),
                  pl.BlockSpec((tm, tn), lambda i, j: (i, j)),
                  gspec],
        out_specs=pl.BlockSpec((tm, tn), lambda i, j: (i, j)),
        out_shape=jax.ShapeDtypeStruct((m, n), F32),
        compiler_params=_cparams(("arbitrary", "arbitrary")),
        name="outproj",
    )(a, b, w, w, x, gate)


GDN_GROUP_PROMPT = 8
GDN_GROUP_SAMPLE = 16


def _rowpad(a, rows):
    if a.shape[0] == rows:
        return a
    return jnp.concatenate([a, jnp.zeros((rows - a.shape[0], a.shape[1]), a.dtype)], axis=0)


def _gdn_heads(qn, kn, vh, bcol, gcol, s_prev, lvl_ref, c):
    cp = LANE
    n = len(qn)
    hs = range(n)
    lane = lax.broadcasted_iota(jnp.int32, (c, cp), 1)
    row = lax.broadcasted_iota(jnp.int32, (c, cp), 0)
    tri = lane <= row

    def b16(x):
        return x.astype(BF16)

    g_hi = [b16(g).astype(F32) for g in gcol]
    r1 = [gcol[h] - g_hi[h] for h in hs]
    g_mid = [b16(r).astype(F32) for r in r1]
    g_lo = [r1[h] - g_mid[h] for h in hs]
    a_mat = [b16(jnp.where(lane == 0, g_hi[h], jnp.where(lane == 1, g_mid[h], jnp.where(lane == 2, g_lo[h],
                 jnp.where(lane < 6, 1.0, 0.0))))) for h in hs]
    b_mat = [b16(_rowpad(jnp.where(lane < 3, 1.0, jnp.where(lane == 3, -g_hi[h], jnp.where(lane == 4, -g_mid[h],
                 jnp.where(lane == 5, -g_lo[h], 0.0)))), cp)) for h in hs]
    kn16 = [b16(k) for k in kn]
    qkn16 = [jnp.concatenate([b16(qn[h]), kn16[h]], axis=0) for h in hs]
    knp16 = [_rowpad(k, cp) for k in kn16]
    diff = [_dot_nt(a_mat[h], b_mat[h]) for h in hs]
    qk_kk = [_dot_nt(qkn16[h], knp16[h]) for h in hs]
    decay = [jnp.where(tri, jnp.exp(jnp.minimum(d, 0.0)), 0.0) for d in diff]
    eg = [jnp.exp(g) for g in gcol]
    lmat = [(bcol[h] * qk_kk[h][c:]) * decay[h] for h in hs]
    qkd16 = [b16(qk_kk[h][:c] * decay[h]) for h in hs]
    n_mat = [-(lm * lvl_ref[0]) for lm in lmat]
    for lv in range(1, lvl_ref.shape[0]):
        bm = [lm * lvl_ref[lv] for lm in lmat]
        n16 = [b16(x) for x in n_mat]
        w_mat = [bm[h] + _dot(n16[h], _rowpad(b16(bm[h]), cp)) for h in hs]
        n_mat = [n_mat[h] - w_mat[h] - _dot(b16(w_mat[h]), _rowpad(n16[h], cp)) for h in hs]
    rhs = [jnp.concatenate([vh[h] * bcol[h], kn[h] * (bcol[h] * eg[h])], axis=1) for h in hs]
    sol = [rhs[h] + _dot(b16(n_mat[h]), _rowpad(b16(rhs[h]), cp)) for h in hs]
    kq16 = [jnp.concatenate([b16(sol[h][:, GDN_D:]), b16(qn[h] * eg[h])], axis=0) for h in hs]
    kq_s = [_dot(kq16[h], b16(s_prev[h])) for h in hs]
    u = [sol[h][:, :GDN_D] - kq_s[h][:c] for h in hs]
    u16 = [_rowpad(b16(x), cp) for x in u]
    g_last = [g[c - 1:c, :] for g in gcol]
    kd16 = [_rowpad(b16(kn[h] * jnp.exp(g_last[h] - gcol[h])), cp) for h in hs]
    o = [kq_s[h][c:] + _dot(qkd16[h], u16[h]) for h in hs]
    s_new = [s_prev[h] * jnp.exp(g_last[h]) + _dot_tn(kd16[h], u16[h]) for h in hs]
    return o, s_new


def _gdn_kernel(*refs, c, l_valid, prompt, n_chunks, group):
    if prompt:
        (q_ref, k_ref, v_ref, z_ref, ba_ref, cw_ref, hp_ref, gn_ref, lvl_ref,
         o_ref, sout_ref, cout_ref, xe_ref, cb_ref, s_ref) = refs
    else:
        (q_ref, k_ref, v_ref, z_ref, ba_ref, cw_ref, hp_ref, gn_ref, lvl_ref, cin_ref, sin_ref,
         o_ref, sout_ref, cout_ref, xe_ref, cb_ref) = refs
    i = pl.program_id(0)
    w = GDN_W
    if prompt:
        @pl.when(i == 0)
        def _():
            xe_ref[0:8, :] = jnp.zeros((8, 3 * w), F32)
            s_ref[...] = jnp.zeros(s_ref.shape, F32)
        xe_ref[8:8 + c, 0:w] = q_ref[...]
        xe_ref[8:8 + c, w:2 * w] = k_ref[...]
        xe_ref[8:8 + c, 2 * w:3 * w] = v_ref[...]
    else:
        xe_ref[0:8, :] = cin_ref[0]
        xe_ref[8:8 + c, 0:w] = q_ref[0]
        xe_ref[8:8 + c, w:2 * w] = k_ref[0]
        xe_ref[8:8 + c, 2 * w:3 * w] = v_ref[0]
    rowc = lax.broadcasted_iota(jnp.int32, (c, 512), 0)
    for cb in range(3 * w // 512):
        cs = slice(cb * 512, (cb + 1) * 512)
        acc = xe_ref[8:8 + c, cs] * cw_ref[CONV_W - 1:CONV_W, cs]
        for j in range(CONV_W - 1):
            acc = acc + xe_ref[8 - (CONV_W - 1) + j:8 - (CONV_W - 1) + j + c, cs] * cw_ref[j:j + 1, cs]
        y = _silu(acc)
        if l_valid < c:
            y = jnp.where(rowc < l_valid, y, 0.0)
        cb_ref[:, cs] = y
    new_conv = xe_ref[l_valid:l_valid + 8, :]
    if prompt:
        cout_ref[...] = new_conv
        xe_ref[0:8, :] = new_conv
    else:
        cout_ref[0] = new_conv

    ba = ba_ref[...] if prompt else ba_ref[0]
    row = lax.broadcasted_iota(jnp.int32, (c, LANE), 0)
    beta_all = _sigmoid(ba)
    xg = ba + hp_ref[1:2, :]
    softplus = jnp.maximum(xg, 0.0) + jnp.log1p(jnp.exp(-jnp.abs(xg)))
    g_all = -jnp.exp(hp_ref[0:1, :]) * softplus
    if l_valid < c:
        beta_all = jnp.where(row < l_valid, beta_all, 0.0)
        g_all = jnp.where(row < l_valid, g_all, 0.0)
    gcum = g_all
    s = 1
    while s < c:
        gcum = gcum + jnp.where(row >= s, pltpu.roll(gcum, s, 0), 0.0)
        s *= 2
    gn = gn_ref[...]

    hg = group
    for h0 in range(0, GDN_HEADS, hg):
        hs = range(h0, h0 + hg)
        cols = [slice(h * GDN_D, (h + 1) * GDN_D) for h in hs]
        qh = [cb_ref[:, cl] for cl in cols]
        kh = [cb_ref[:, w + h * GDN_D:w + (h + 1) * GDN_D] for h in hs]
        vh = [cb_ref[:, 2 * w + h * GDN_D:2 * w + (h + 1) * GDN_D] for h in hs]
        qn = [x * lax.rsqrt(jnp.sum(x * x, axis=-1, keepdims=True) + EPS) * (GDN_D ** -0.5) for x in qh]
        kn = [x * lax.rsqrt(jnp.sum(x * x, axis=-1, keepdims=True) + EPS) for x in kh]
        bcol = [beta_all[:, h:h + 1] for h in hs]
        gcol = [gcum[:, h + GDN_HEADS:h + GDN_HEADS + 1] for h in hs]
        s_prev = [s_ref[h] if prompt else sin_ref[0, h] for h in hs]
        o, s_new = _gdn_heads(qn, kn, vh, bcol, gcol, s_prev, lvl_ref, c)
        for j, h in enumerate(hs):
            if prompt:
                s_ref[h] = s_new[j]
            else:
                sout_ref[0, h] = s_new[j]
        zh = [z_ref[:, cl] if prompt else z_ref[0, :, cl] for cl in cols]
        on = [x * lax.rsqrt(jnp.mean(x * x, axis=-1, keepdims=True) + EPS) * gn for x in o]
        for j, cl in enumerate(cols):
            res = (on[j] * _silu(zh[j])).astype(o_ref.dtype)
            if prompt:
                o_ref[:, cl] = res
            else:
                o_ref[0, :, cl] = res
    if prompt:
        @pl.when(i == n_chunks - 1)
        def _():
            sout_ref[...] = s_ref[...]


def _level_masks(c, l_valid):
    i = jnp.arange(c, dtype=jnp.int32)[:, None]
    j = jnp.arange(LANE, dtype=jnp.int32)[None, :]
    masks = []
    s = 1
    while s < l_valid:
        masks.append((i // (2 * s) == j // (2 * s)) & (i % (2 * s) >= s) & (j % (2 * s) < s))
        s *= 2
    return jnp.stack(masks).astype(F32)


def _gdn_prompt(proj, ba_col, conv_w, hp, gn, c=128):
    t = proj.shape[0]
    n_chunks = t // c
    w = GDN_W
    lvl = _level_masks(c, c)
    kern = functools.partial(_gdn_kernel, c=c, l_valid=c, prompt=True, n_chunks=n_chunks, group=GDN_GROUP_PROMPT)
    return pl.pallas_call(
        kern,
        grid=(n_chunks,),
        in_specs=[pl.BlockSpec((c, w), lambda i: (i, 0)),
                  pl.BlockSpec((c, w), lambda i: (i, 1)),
                  pl.BlockSpec((c, w), lambda i: (i, 2)),
                  pl.BlockSpec((c, w), lambda i: (i, 3)),
                  pl.BlockSpec((c, LANE), lambda i: (i, ba_col)),
                  pl.BlockSpec((CONV_W, 3 * w), lambda i: (0, 0)),
                  pl.BlockSpec((8, LANE), lambda i: (0, 0)),
                  pl.BlockSpec((1, GDN_D), lambda i: (0, 0)),
                  pl.BlockSpec(lvl.shape, lambda i: (0, 0, 0))],
        out_specs=[pl.BlockSpec((c, w), lambda i: (i, 0)),
                   pl.BlockSpec((GDN_HEADS, GDN_D, GDN_D), lambda i: (0, 0, 0)),
                   pl.BlockSpec((8, 3 * w), lambda i: (0, 0))],
        out_shape=[jax.ShapeDtypeStruct((t, w), BF16),
                   jax.ShapeDtypeStruct((GDN_HEADS, GDN_D, GDN_D), F32),
                   jax.ShapeDtypeStruct((8, 3 * w), F32)],
        scratch_shapes=[pltpu.VMEM((8 + c, 3 * w), F32),
                        pltpu.VMEM((c, 3 * w), F32),
                        pltpu.VMEM((GDN_HEADS, GDN_D, GDN_D), F32)],
        compiler_params=_cparams(("arbitrary",)),
        name="gdn_prompt",
    )(proj, proj, proj, proj, proj, conv_w, hp, gn, lvl)


def _gdn_sample(proj3, ba_col, conv_w, hp, gn, conv_in, s_in, l):
    b, c, _ = proj3.shape
    w = GDN_W
    lvl = _level_masks(c, l)
    kern = functools.partial(_gdn_kernel, c=c, l_valid=l, prompt=False, n_chunks=1, group=GDN_GROUP_SAMPLE)
    return pl.pallas_call(
        kern,
        grid=(b,),
        in_specs=[pl.BlockSpec((1, c, w), lambda i: (i, 0, 0)),
                  pl.BlockSpec((1, c, w), lambda i: (i, 0, 1)),
                  pl.BlockSpec((1, c, w), lambda i: (i, 0, 2)),
                  pl.BlockSpec((1, c, w), lambda i: (i, 0, 3)),
                  pl.BlockSpec((1, c, LANE), lambda i: (i, 0, ba_col)),
                  pl.BlockSpec((CONV_W, 3 * w), lambda i: (0, 0)),
                  pl.BlockSpec((8, LANE), lambda i: (0, 0)),
                  pl.BlockSpec((1, GDN_D), lambda i: (0, 0)),
                  pl.BlockSpec(lvl.shape, lambda i: (0, 0, 0)),
                  pl.BlockSpec((1, 8, 3 * w), lambda i: (i, 0, 0)),
                  pl.BlockSpec((1, GDN_HEADS, GDN_D, GDN_D), lambda i: (i, 0, 0, 0))],
        out_specs=[pl.BlockSpec((1, c, w), lambda i: (i, 0, 0)),
                   pl.BlockSpec((1, GDN_HEADS, GDN_D, GDN_D), lambda i: (i, 0, 0, 0)),
                   pl.BlockSpec((1, 8, 3 * w), lambda i: (i, 0, 0))],
        out_shape=[jax.ShapeDtypeStruct((b, c, w), F32),
                   jax.ShapeDtypeStruct((b, GDN_HEADS, GDN_D, GDN_D), F32),
                   jax.ShapeDtypeStruct((b, 8, 3 * w), F32)],
        scratch_shapes=[pltpu.VMEM((8 + c, 3 * w), F32),
                        pltpu.VMEM((c, 3 * w), F32)],
        compiler_params=_cparams(("arbitrary",)),
        name="gdn_sample",
    )(proj3, proj3, proj3, proj3, proj3, conv_w, hp, gn, lvl, conv_in, s_in)


def _t5_bucket(dist):
    d = jnp.maximum(dist, 0)
    max_exact = N_BUCKETS // 2
    large = max_exact + (jnp.log(jnp.maximum(d, 1).astype(F32) / max_exact)
                         / math.log(WINDOW / max_exact) * (N_BUCKETS - max_exact)).astype(jnp.int32)
    large = jnp.minimum(large, N_BUCKETS - 1)
    return jnp.where(d < max_exact, d, large)


def _bias_table(dist, rel_bias):
    valid = (dist >= 0) & (dist < WINDOW)
    onehot = (_t5_bucket(dist)[None] == jnp.arange(N_BUCKETS, dtype=jnp.int32)[:, None, None]).astype(F32)
    b = jnp.einsum('bh,bqs->hqs', rel_bias.astype(F32), onehot, precision=lax.Precision.HIGHEST)
    return jnp.where(valid[None], b, NEG)


def _lo_hi(slab, g):
    lane = lax.broadcasted_iota(jnp.int32, slab.shape, 1)
    if g % 2 == 0:
        lo = jnp.where(lane < SWA_HD, slab, 0.0)
        hi = pltpu.roll(lo, SWA_HD, 1)
    else:
        hi = jnp.where(lane >= SWA_HD, slab, 0.0)
        lo = pltpu.roll(hi, SWA_HD, 1)
    return lo, hi


def _sink_softmax_chains(parts, sinks):
    cs = range(len(parts))
    m = [sinks[i] for i in cs]
    for j in range(len(parts[0])):
        mx = [jnp.max(parts[i][j], axis=-1, keepdims=True) for i in cs]
        m = [jnp.maximum(m[i], mx[i]) for i in cs]
    es = [[jnp.exp(p - m[i]) for p in parts[i]] for i in cs]
    den = [jnp.exp(sinks[i] - m[i]) for i in cs]
    for j in range(len(parts[0])):
        sm = [jnp.sum(es[i][j], axis=-1, keepdims=True) for i in cs]
        den = [den[i] + sm[i] for i in cs]
    inv = [1.0 / d for d in den]
    return [[e * inv[i] for e in es[i]] for i in cs]


SWA_GROUP_PROMPT = 2


def _swa_prompt_kernel(q_ref, kp_ref, kc_ref, vp_ref, vc_ref, bias_ref, sink_ref, o_ref):
    n = pl.program_id(0)
    wq = WINDOW
    col = lax.broadcasted_iota(jnp.int32, (wq, 2 * wq), 1)
    first = jnp.logical_and(n == 0, col < wq)
    sink_all = sink_ref[...]
    scale = SWA_HD ** -0.5
    for g0 in range(0, SWA_KV, SWA_GROUP_PROMPT):
        gs = range(g0, g0 + SWA_GROUP_PROMPT)
        kv = {}
        for g in gs:
            sl = slice((g // 2) * LANE, (g // 2 + 1) * LANE)
            kslab = jnp.concatenate([kp_ref[:, sl], kc_ref[:, sl]], axis=0)
            vslab = jnp.concatenate([vp_ref[:, sl], vc_ref[:, sl]], axis=0)
            kv[g] = ([x.astype(BF16) for x in _lo_hi(kslab, g)], [x.astype(BF16) for x in _lo_hi(vslab, g)])
        chains = [(g, s, half) for g in gs for s in range(2) for half in range(2)]
        qs = {(g, s): q_ref[:, (2 * g + s) * LANE:(2 * g + s + 1) * LANE].astype(BF16) for g in gs for s in range(2)}
        logits = [_dot_nt(qs[g, s], kv[g][0][half]) for g, s, half in chains]
        logits = [jnp.where(first, NEG, logits[i] * scale + bias_ref[4 * g + 2 * s + half])
                  for i, (g, s, half) in enumerate(chains)]
        sinks = [sink_all[:, 4 * g + 2 * s + half:4 * g + 2 * s + half + 1] for g, s, half in chains]
        probs = _sink_softmax_chains([[x] for x in logits], sinks)
        pv = [_dot(probs[i][0].astype(BF16), kv[g][1][half]) for i, (g, s, half) in enumerate(chains)]
        for i, (g, s, half) in enumerate(chains):
            if half == 0:
                o_ref[:, (2 * g + s) * LANE:(2 * g + s + 1) * LANE] = (pv[i] + pv[i + 1]).astype(o_ref.dtype)


def _swa_prompt(proj, qcol, bias, sinks):
    t = proj.shape[0]
    nb = t // WINDOW
    kb = qcol * (SWA_W // SWA_KVW) + SWA_W // SWA_KVW
    return pl.pallas_call(
        _swa_prompt_kernel,
        grid=(nb,),
        in_specs=[pl.BlockSpec((WINDOW, SWA_W), lambda n: (n, qcol)),
                  pl.BlockSpec((WINDOW, SWA_KVW), lambda n: (jnp.maximum(n - 1, 0), kb)),
                  pl.BlockSpec((WINDOW, SWA_KVW), lambda n: (n, kb)),
                  pl.BlockSpec((WINDOW, SWA_KVW), lambda n: (jnp.maximum(n - 1, 0), kb + 1)),
                  pl.BlockSpec((WINDOW, SWA_KVW), lambda n: (n, kb + 1)),
                  pl.BlockSpec((SWA_HEADS, WINDOW, 2 * WINDOW), lambda n: (0, 0, 0)),
                  pl.BlockSpec((1, LANE), lambda n: (0, 0))],
        out_specs=pl.BlockSpec((WINDOW, SWA_W), lambda n: (n, 0)),
        out_shape=jax.ShapeDtypeStruct((t, SWA_W), BF16),
        compiler_params=_cparams(("arbitrary",)),
        name="swa_prompt",
    )(proj, proj, proj, proj, proj, bias, sinks)


def _swa_sample_kernel(q_ref, kn_ref, vn_ref, kc_ref, vc_ref, bc_ref, bn_ref, sink_ref, o_ref, ko_ref, vo_ref, *, l):
    r = 8
    wb = WINDOW
    for c_ref, n_ref, out_ref in ((kc_ref, kn_ref, ko_ref), (vc_ref, vn_ref, vo_ref)):
        out_ref[0, 0:wb - l, :] = c_ref[0, l:wb, :]
        out_ref[0, wb - l:wb, :] = n_ref[0, 0:l, :]
    sink_all = sink_ref[...]
    q = q_ref[0]
    knew = kn_ref[0]
    vnew = vn_ref[0]
    scale = SWA_HD ** -0.5
    gs = range(SWA_KV)
    chains = [(g, half) for g in gs for half in range(2)]
    sls = [slice((g // 2) * LANE, (g // 2 + 1) * LANE) for g in gs]
    kc = [[x.astype(BF16) for x in _lo_hi(kc_ref[0, :, sls[g]], g)] for g in gs]
    vc = [[x.astype(BF16) for x in _lo_hi(vc_ref[0, :, sls[g]], g)] for g in gs]
    kn = [[_rowpad(x, LANE).astype(BF16) for x in _lo_hi(knew[:, sls[g]], g)] for g in gs]
    vn = [[_rowpad(x, LANE).astype(BF16) for x in _lo_hi(vnew[:, sls[g]], g)] for g in gs]
    qs = [jnp.concatenate([q[:, (2 * g) * LANE:(2 * g + 1) * LANE],
                           q[:, (2 * g + 1) * LANE:(2 * g + 2) * LANE]], axis=0).astype(BF16) for g in gs]
    lc = [_dot_nt(qs[g], kc[g][half]) for g, half in chains]
    ln = [_dot_nt(qs[g], kn[g][half]) for g, half in chains]
    lc = [lc[i] * scale + bc_ref[2 * g + half] for i, (g, half) in enumerate(chains)]
    ln = [ln[i] * scale + bn_ref[2 * g + half] for i, (g, half) in enumerate(chains)]
    sinks = [jnp.concatenate([jnp.broadcast_to(sink_all[:, 4 * g + half:4 * g + half + 1], (r, 1)),
                              jnp.broadcast_to(sink_all[:, 4 * g + 2 + half:4 * g + 2 + half + 1], (r, 1))], axis=0)
             for g, half in chains]
    probs = _sink_softmax_chains([[lc[i], ln[i]] for i in range(len(chains))], sinks)
    pvc = [_dot(probs[i][0].astype(BF16), vc[g][half]) for i, (g, half) in enumerate(chains)]
    pvn = [_dot(probs[i][1].astype(BF16), vn[g][half]) for i, (g, half) in enumerate(chains)]
    for g in gs:
        acc = (pvc[2 * g] + pvn[2 * g]) + (pvc[2 * g + 1] + pvn[2 * g + 1])
        o_ref[0, :, (2 * g) * LANE:(2 * g + 1) * LANE] = acc[:r]
        o_ref[0, :, (2 * g + 1) * LANE:(2 * g + 2) * LANE] = acc[r:]


def _swa_sample(proj3, qcol, k_cache, v_cache, bias_c, bias_n, sinks, l):
    b, r, _ = proj3.shape
    kb = qcol * (SWA_W // SWA_KVW) + SWA_W // SWA_KVW
    cache_spec = pl.BlockSpec((1, WINDOW, SWA_KVW), lambda i: (i, 0, 0))
    return pl.pallas_call(
        functools.partial(_swa_sample_kernel, l=l),
        grid=(b,),
        in_specs=[pl.BlockSpec((1, r, SWA_W), lambda i: (i, 0, qcol)),
                  pl.BlockSpec((1, r, SWA_KVW), lambda i: (i, 0, kb)),
                  pl.BlockSpec((1, r, SWA_KVW), lambda i: (i, 0, kb + 1)),
                  pl.BlockSpec((1, WINDOW, SWA_KVW), lambda i: (i, 0, 0)),
                  pl.BlockSpec((1, WINDOW, SWA_KVW), lambda i: (i, 0, 0)),
                  pl.BlockSpec((2 * SWA_KV, 2 * r, LANE), lambda i: (0, 0, 0)),
                  pl.BlockSpec((2 * SWA_KV, 2 * r, LANE), lambda i: (0, 0, 0)),
                  pl.BlockSpec((1, LANE), lambda i: (0, 0))],
        out_specs=[pl.BlockSpec((1, r, SWA_W), lambda i: (i, 0, 0)), cache_spec, cache_spec],
        out_shape=[jax.ShapeDtypeStruct((b, r, SWA_W), F32),
                   jax.ShapeDtypeStruct(k_cache.shape, F32),
                   jax.ShapeDtypeStruct(v_cache.shape, F32)],
        compiler_params=_cparams(("arbitrary",)),
        name="swa_sample",
    )(proj3, proj3, proj3, k_cache, v_cache, bias_c, bias_n, sinks)


MOE_WCHUNK = 1024
MOE_WSLOTS = 4
MOE_LOOKAHEAD = 3
MOE_ROWS_SMALL = 256
MOE_ROW_UNROLL = 4


def _moe_kernel(ord_ref, be_ref, bs_ref, bn_ref, h_hbm, wg_hbm, wu_hbm, wd_hbm, y_hbm,
                xbuf, x16, hgu, hid16, acc, ybuf, wbuf, gsem, ssem, wsem, *, n_blocks, t_all):
    k = pl.program_id(0)
    n = bn_ref[k]
    slot = lax.rem(k, 2)
    d = x16.shape[1]
    half = d // 2
    kc = MOE_WCHUNK // 2
    n_gu = d // kc
    n_dn = d // MOE_WCHUNK
    n_ch = n_gu + n_dn
    assert n_ch % MOE_WSLOTS == 0 and wg_hbm.shape[2] == MOE_WCHUNK

    def gather_copy(tok, r):
        return pltpu.make_async_copy(h_hbm.at[pl.ds(tok, 1)], xbuf.at[pl.ds(r, 1)], gsem)

    def scatter_copy(sl, r, a):
        return pltpu.make_async_copy(ybuf.at[sl, pl.ds(r, 1)], y_hbm.at[pl.ds(a, 1)], ssem.at[sl])

    def w_copies(e, j):
        s = j % MOE_WSLOTS
        if j < n_gu:
            rows = pl.ds(j * kc, kc)
            return [pltpu.make_async_copy(wg_hbm.at[e, rows, :], wbuf.at[s, pl.ds(0, kc), :], wsem.at[s]),
                    pltpu.make_async_copy(wu_hbm.at[e, rows, :], wbuf.at[s, pl.ds(kc, kc), :], wsem.at[s])]
        cols = pl.ds((j - n_gu) * MOE_WCHUNK, MOE_WCHUNK)
        return [pltpu.make_async_copy(wd_hbm.at[e, :, cols], wbuf.at[s], wsem.at[s])]

    def for_rows(cnt, fn):
        groups = lax.shift_right_logical(cnt, MOE_ROW_UNROLL.bit_length() - 1)

        def group(q, carry):
            for u in range(MOE_ROW_UNROLL):
                fn(q * MOE_ROW_UNROLL + u)
            return carry

        def single(r, carry):
            fn(r)
            return carry
        lax.fori_loop(0, groups, group, 0)
        lax.fori_loop(groups * MOE_ROW_UNROLL, cnt, single, 0)

    def gather_start(kk):
        st = bs_ref[kk]

        def one(r):
            a = ord_ref[st + r]
            gather_copy(jnp.where(a >= t_all, a - t_all, a), r).start()
        for_rows(bn_ref[kk], one)

    def gather_wait(kk):
        for_rows(bn_ref[kk], lambda r: gather_copy(0, 0).wait())

    def scatter_start(kk, sl):
        st = bs_ref[kk]
        for_rows(bn_ref[kk], lambda r: scatter_copy(sl, r, ord_ref[st + r]).start())

    def scatter_wait(kk, sl):
        for_rows(bn_ref[kk], lambda r: scatter_copy(sl, 0, 0).wait())

    @pl.when(n > 0)
    def _():
        e = be_ref[k]
        k1 = jnp.minimum(k + 1, n_blocks - 1)
        has_next = jnp.logical_and(k + 1 < n_blocks, bn_ref[k1] > 0)
        e_next = be_ref[k1]

        @pl.when(k == 0)
        def _():
            xbuf[...] = jnp.zeros(xbuf.shape, xbuf.dtype)
            gather_start(0)
            for j in range(MOE_LOOKAHEAD):
                for cp in w_copies(e, j):
                    cp.start()
        gather_wait(k)
        words = xbuf[...]
        lo = lax.bitcast_convert_type(lax.shift_left(words, jnp.uint32(16)), F32)
        hi = lax.bitcast_convert_type(words & jnp.uint32(0xFFFF0000), F32)
        x16[:, 0:half] = lo.astype(BF16)
        x16[:, half:d] = hi.astype(BF16)

        @pl.when(has_next)
        def _():
            gather_start(k1)

        def ffn(m):
            for j in range(n_ch):
                jn = j + MOE_LOOKAHEAD
                if jn < n_ch:
                    for cp in w_copies(e, jn):
                        cp.start()
                else:
                    @pl.when(has_next)
                    def _():
                        for cp in w_copies(e_next, jn - n_ch):
                            cp.start()
                for cp in w_copies(e, j):
                    cp.wait()
                s = j % MOE_WSLOTS
                if j < n_gu:
                    xs = x16[0:m, j * kc:(j + 1) * kc]
                    pg = _dot(xs, wbuf[s, 0:kc, :].astype(BF16))
                    pu = _dot(xs, wbuf[s, kc:2 * kc, :].astype(BF16))
                    if j == 0:
                        hgu[0, 0:m] = pg
                        hgu[1, 0:m] = pu
                    else:
                        hgu[0, 0:m] += pg
                        hgu[1, 0:m] += pu
                    if j == n_gu - 1:
                        hid16[0:m] = (_silu(hgu[0, 0:m]) * hgu[1, 0:m]).astype(BF16)
                else:
                    cols = slice((j - n_gu) * MOE_WCHUNK, (j - n_gu + 1) * MOE_WCHUNK)
                    acc[0:m, cols] = _dot(hid16[0:m], wbuf[s].astype(BF16))
            bits = lax.bitcast_convert_type(acc[0:m].astype(BF16).astype(F32), jnp.uint32)
            ybuf[slot, 0:m] = (lax.shift_right_logical(bits[:, :half], jnp.uint32(16))
                               | (bits[:, half:] & jnp.uint32(0xFFFF0000)))

        @pl.when(n <= MOE_ROWS_SMALL)
        def _():
            ffn(MOE_ROWS_SMALL)

        @pl.when(n > MOE_ROWS_SMALL)
        def _():
            ffn(MOE_ROWS)
        scatter_start(k, slot)

        @pl.when(k > 0)
        def _():
            scatter_wait(k - 1, 1 - slot)

        @pl.when(jnp.logical_not(has_next))
        def _():
            scatter_wait(k, slot)


def _moe_ffn(order, blk_e, blk_start, blk_n, h_packed, w_gate, w_up, w_down):
    a = order.shape[0]
    t_all = h_packed.shape[0]
    d = 2 * h_packed.shape[1]
    n_blocks = blk_e.shape[0]
    grid_spec = pltpu.PrefetchScalarGridSpec(
        num_scalar_prefetch=4,
        grid=(n_blocks,),
        in_specs=[pl.BlockSpec(memory_space=pl.ANY)] * 4,
        out_specs=pl.BlockSpec(memory_space=pl.ANY),
        scratch_shapes=[pltpu.VMEM((MOE_ROWS, d // 2), jnp.uint32),
                        pltpu.VMEM((MOE_ROWS, d), BF16),
                        pltpu.VMEM((2, MOE_ROWS, D_EXPERT), F32),
                        pltpu.VMEM((MOE_ROWS, D_EXPERT), BF16),
                        pltpu.VMEM((MOE_ROWS, d), F32),
                        pltpu.VMEM((2, MOE_ROWS, d // 2), jnp.uint32),
                        pltpu.VMEM((MOE_WSLOTS, MOE_WCHUNK, MOE_WCHUNK), F32),
                        pltpu.SemaphoreType.DMA(()),
                        pltpu.SemaphoreType.DMA((2,)),
                        pltpu.SemaphoreType.DMA((MOE_WSLOTS,))],
    )
    return pl.pallas_call(
        functools.partial(_moe_kernel, n_blocks=n_blocks, t_all=t_all),
        grid_spec=grid_spec,
        out_shape=jax.ShapeDtypeStruct((a, d // 2), jnp.uint32),
        compiler_params=_cparams(("arbitrary",)),
        name="moe_ffn",
    )(order, blk_e, blk_start, blk_n, h_packed, w_gate, w_up, w_down)


def _unpack_pairs(words):
    lo = lax.bitcast_convert_type(lax.shift_left(words, jnp.uint32(16)), F32)
    hi = lax.bitcast_convert_type(words & jnp.uint32(0xFFFF0000), F32)
    return jnp.concatenate([lo, hi], axis=1)


def _final_kernel(x_ref, y0_ref, y1_ref, p_ref, g_ref, nf_ref, o_ref):
    p = p_ref[...]
    moe = _unpack_pairs(y0_ref[...]) * p[:, 0:1] + _unpack_pairs(y1_ref[...]) * p[:, 1:2]
    x = x_ref[...] + g_ref[...] * moe
    o_ref[...] = x * lax.rsqrt(jnp.mean(x * x, axis=-1, keepdims=True) + EPS) * nf_ref[...]


def _final(x, y, row0, gates, gt2, nf, tm=256):
    t, d = x.shape
    t_all = gates.shape[0]
    tm = math.gcd(math.gcd(tm, t), math.gcd(row0, t_all))
    b0 = row0 // tm
    b1 = t_all // tm + b0
    return pl.pallas_call(
        _final_kernel,
        grid=(t // tm,),
        in_specs=[pl.BlockSpec((tm, d), lambda i: (i, 0)),
                  pl.BlockSpec((tm, d // 2), lambda i: (b0 + i, 0)),
                  pl.BlockSpec((tm, d // 2), lambda i: (b1 + i, 0)),
                  pl.BlockSpec((tm, LANE), lambda i: (b0 + i, 0)),
                  _mod_spec(gt2, tm, d),
                  pl.BlockSpec((1, d), lambda i: (0, 0))],
        out_specs=pl.BlockSpec((tm, d), lambda i: (i, 0)),
        out_shape=jax.ShapeDtypeStruct((t, d), F32),
        compiler_params=_cparams(("arbitrary",)),
        name="final",
    )(x, y, y, gates, gt2, nf.reshape(1, d))


def _dispatch(expert_idx):
    t, k = expert_idx.shape
    a = t * k
    n_blocks = a // MOE_ROWS + N_EXPERTS
    flat_e = expert_idx.T.reshape(a)
    order = jnp.argsort(flat_e, stable=True).astype(jnp.int32)
    counts = jnp.zeros((N_EXPERTS,), jnp.int32).at[flat_e].add(1)
    seg_start = jnp.cumsum(counts) - counts
    nblk = (counts + MOE_ROWS - 1) // MOE_ROWS
    blk_end = jnp.cumsum(nblk)
    n_active = blk_end[-1]
    kk = jnp.minimum(jnp.arange(n_blocks, dtype=jnp.int32), n_active - 1)
    blk_e = jnp.minimum(jnp.searchsorted(blk_end, kk, side='right'), N_EXPERTS - 1).astype(jnp.int32)
    j = kk - (blk_end[blk_e] - nblk[blk_e])
    blk_start = (seg_start[blk_e] + j * MOE_ROWS).astype(jnp.int32)
    blk_n = jnp.clip(counts[blk_e] - j * MOE_ROWS, 0, MOE_ROWS)
    blk_n = jnp.where(jnp.arange(n_blocks) < n_active, blk_n, 0).astype(jnp.int32)
    return order, blk_e, blk_start, blk_n


def kernel(x_prompt, x_sample, state_gdn, state_conv, cache_swa_k, cache_swa_v, c_prompt, c_sample, w_ada, b_ada, norm_mix, w_in, conv_w, a_log, dt_bias, gdn_norm, swa_sinks, rel_bias, w_out, norm_moe, router_group, router_group_bias, router_expert, router_expert_bias, w_gate, w_up, w_down, norm_final):
    depth = w_ada.shape[0]
    assert depth == 1
    bp, seq, d = x_prompt.shape
    assert bp == 1 and seq % WINDOW == 0
    bs, ls, _ = x_sample.shape
    tp = bp * seq
    ts = bs * ls
    n_main = 2 * GDN_W + 2 * GDN_W
    n_ba = 2 * GDN_HEADS

    c_all = jnp.concatenate([c_prompt, c_sample], axis=0)
    m_pad = -(-c_all.shape[0] // 16) * 16
    c_all = jnp.pad(c_all, ((0, m_pad - c_all.shape[0]), (0, 0)))
    mod = _adaln(c_all, w_ada[0], b_ada[0])
    mod_p = [mod[0:1, j * d:(j + 1) * d] for j in range(6)]
    mod_s = [jnp.repeat(mod[1:1 + bs, j * d:(j + 1) * d], ls, axis=0) for j in range(6)]

    n_gdn = n_main + 512
    w_swa = w_in[0][:, n_main + n_ba:].astype(BF16)
    w_out16 = w_out[0].astype(BF16)
    ba_col = n_main // LANE
    kcol = SWA_W
    hp = jnp.zeros((8, LANE), F32)
    hp = hp.at[0, GDN_HEADS:2 * GDN_HEADS].set(a_log[0]).at[1, GDN_HEADS:2 * GDN_HEADS].set(dt_bias[0])
    gn = gdn_norm[0].reshape(1, GDN_D)
    sinks = jnp.pad(swa_sinks[0], (0, LANE - SWA_HEADS)).reshape(1, LANE)

    xp = x_prompt.reshape(tp, d)
    h_p = _norm_mod(xp, norm_mix[0], mod_p[1], mod_p[0])
    proj_p = _matmul_f32w(h_p, w_in[0], n_gdn, 1024, 512, name="inproj_p")
    pswa_p = _matmul(h_p, w_swa, 1024, 512, name="inproj_swa_p")
    og_p, s_p, conv_p = _gdn_prompt(proj_p, ba_col, conv_w[0], hp, gn)
    conv_p = conv_p[8 - (CONV_W - 1):]
    qi = jnp.arange(WINDOW, dtype=jnp.int32)
    sj = jnp.arange(2 * WINDOW, dtype=jnp.int32)
    bias_p = _bias_table((WINDOW + qi)[:, None] - sj[None, :], rel_bias)
    os_p = _swa_prompt(pswa_p, 0, bias_p, sinks)
    k_p = pswa_p[tp - WINDOW:, kcol:kcol + SWA_KVW]
    v_p = pswa_p[tp - WINDOW:, kcol + SWA_KVW:kcol + 2 * SWA_KVW]
    x1_p = _outproj(og_p, os_p, w_out16, xp, mod_p[2])

    xs = x_sample.reshape(ts, d)
    h_s = _norm_mod(xs, norm_mix[0], mod_s[1], mod_s[0])
    proj_s = _matmul_f32w(h_s, w_in[0], n_gdn, 512, 512, name="inproj_s")
    pswa_s = _matmul(h_s, w_swa, 512, 512, name="inproj_swa_s")
    rs = 8
    assert CONV_W - 1 <= ls <= rs
    proj_s8 = jnp.pad(proj_s.reshape(bs, ls, n_gdn), ((0, 0), (0, rs - ls), (0, 0)))
    pswa_s8 = jnp.pad(pswa_s.reshape(bs, ls, pswa_s.shape[1]), ((0, 0), (0, rs - ls), (0, 0)))
    conv_in8 = jnp.pad(state_conv[0], ((0, 0), (8 - (CONV_W - 1), 0), (0, 0)))
    og_s, s_s, conv_s = _gdn_sample(proj_s8, ba_col, conv_w[0], hp, gn, conv_in8, state_gdn[0], ls)
    conv_s = conv_s[:, 8 - (CONV_W - 1):]
    wb = cache_swa_k.shape[2]
    assert wb == WINDOW == LANE
    dist_s = (wb + jnp.arange(ls, dtype=jnp.int32))[:, None] - jnp.arange(wb + ls, dtype=jnp.int32)[None, :]
    bias_s = _bias_table(dist_s, rel_bias)
    bias_s = jnp.pad(bias_s, ((0, 0), (0, rs - ls), (0, 2 * LANE - wb - ls)), constant_values=NEG)
    bias_s = bias_s.reshape(SWA_KV, 2, 2, rs, 2 * LANE)
    bias_s = jnp.transpose(bias_s, (0, 2, 1, 3, 4)).reshape(2 * SWA_KV, 2 * rs, 2 * LANE)
    k_cache = cache_swa_k[0].reshape(bs, wb, SWA_KVW)
    v_cache = cache_swa_v[0].reshape(bs, wb, SWA_KVW)
    os_s, k_s, v_s = _swa_sample(pswa_s8, 0, k_cache, v_cache, bias_s[:, :, :LANE], bias_s[:, :, LANE:], sinks, ls)
    x1_s = _outproj(og_s[:, :ls].reshape(ts, GDN_W).astype(BF16), os_s[:, :ls].reshape(ts, SWA_W).astype(BF16),
                    w_out16, xs, mod_s[2], tm=512)

    wr = jnp.concatenate([router_group[0], router_expert[0]], axis=1)
    wr2 = jnp.pad(wr, ((0, 0), (0, LANE - wr.shape[1]))).astype(BF16)
    br = jnp.pad(jnp.concatenate([router_group_bias[0], router_expert_bias[0]]),
                 (0, LANE - N_GROUPS - N_EXPERTS)).reshape(1, LANE)
    t_all = tp + ts
    h2, eidx, gates = _norm_router(x1_p, x1_s, norm_moe[0], mod_p[4], mod_p[3], mod_s[4], mod_s[3], wr2, br)
    order, blk_e, blk_start, blk_n = _dispatch(eidx[:, :2])
    y = _moe_ffn(order, blk_e, blk_start, blk_n, h2, w_gate[0], w_up[0], w_down[0])
    y_p = _final(x1_p, y, 0, gates, mod_p[5], norm_final)
    y_s = _final(x1_s, y, tp, gates, mod_s[5], norm_final)

    sdt = state_gdn.dtype
    return (y_p.reshape(bp, seq, d), y_s.reshape(bs, ls, d),
            s_p.reshape(1, bp, GDN_HEADS, GDN_D, GDN_D).astype(sdt), s_s[None].astype(sdt),
            conv_p.reshape(1, bp, CONV_W - 1, 3 * GDN_W).astype(state_conv.dtype), conv_s[None].astype(state_conv.dtype),
            k_p.reshape(1, bp, WINDOW, SWA_KV, SWA_HD).astype(cache_swa_k.dtype),
            k_s.reshape(1, bs, wb, SWA_KV, SWA_HD).astype(cache_swa_k.dtype),
            v_p.reshape(1, bp, WINDOW, SWA_KV, SWA_HD).astype(cache_swa_v.dtype),
            v_s.reshape(1, bs, wb, SWA_KV, SWA_HD).astype(cache_swa_v.dtype))
```

```python
import functools
import math
from typing import NamedTuple

import jax
import jax.numpy as jnp
from jax import lax
from jax.experimental import pallas as pl
from jax.experimental.pallas import tpu as pltpu

F32 = jnp.float32
BF16 = jnp.bfloat16
EPS = 1e-6
NEG = -1e30

LANE = 128
GDN_HEADS = 16
GDN_D = 128
GDN_W = GDN_HEADS * GDN_D
CONV_W = 4
SWA_HEADS = 32
SWA_KV = 8
SWA_HD = 64
SWA_W = SWA_HEADS * SWA_HD
SWA_KVW = SWA_KV * SWA_HD
WINDOW = 128
N_BUCKETS = 32
N_GROUPS = 8
EPG = 8
N_EXPERTS = 64
D_EXPERT = 1024
MOE_ROWS = 448
VMEM_LIMIT = 56 * 1024 * 1024


def _cparams(sem):
    return pltpu.CompilerParams(dimension_semantics=sem, vmem_limit_bytes=VMEM_LIMIT)


def _sigmoid(x):
    return 1.0 / (1.0 + jnp.exp(-x))


def _silu(x):
    return x * _sigmoid(x)


def _dot(a, b):
    return jnp.dot(a, b, preferred_element_type=F32)


def _dot_nt(a, b):
    return lax.dot_general(a, b, (((1,), (1,)), ((), ())), preferred_element_type=F32)


def _dot_tn(a, b):
    return lax.dot_general(a, b, (((0,), (0,)), ((), ())), preferred_element_type=F32)


def _ada_kernel(c_ref, w_ref, b_ref, o_ref):
    a = _silu(c_ref[...]).astype(BF16)
    o_ref[...] = _dot(a, w_ref[...].astype(BF16)) + b_ref[...]


def _adaln(c, w, b, tn=512):
    m, d = c.shape
    n = w.shape[1]
    return pl.pallas_call(
        _ada_kernel,
        grid=(n // tn,),
        in_specs=[pl.BlockSpec((m, d), lambda j: (0, 0)),
                  pl.BlockSpec((d, tn), lambda j: (0, j)),
                  pl.BlockSpec((1, tn), lambda j: (0, j))],
        out_specs=pl.BlockSpec((m, tn), lambda j: (0, j)),
        out_shape=jax.ShapeDtypeStruct((m, n), F32),
        compiler_params=_cparams(("arbitrary",)),
        name="adaln",
    )(c, w, b.reshape(1, n))


def _norm_kernel(x_ref, g_ref, sc_ref, sh_ref, o_ref):
    x = x_ref[...]
    y = x * lax.rsqrt(jnp.mean(x * x, axis=-1, keepdims=True) + EPS) * g_ref[...]
    o_ref[...] = (y * (1.0 + sc_ref[...]) + sh_ref[...]).astype(o_ref.dtype)


class Mod(NamedTuple):
    arr: jax.Array
    col: int


def _mod_spec(mod, tm, d):
    if mod.arr.shape[0] == 1:
        return pl.BlockSpec((1, d), lambda i: (0, mod.col))
    return pl.BlockSpec((tm, d), lambda i: (i, mod.col))


def _norm_mod(x, g, sc, sh, tm=256):
    t, d = x.shape
    tm = min(tm, t)
    return pl.pallas_call(
        _norm_kernel,
        grid=(t // tm,),
        in_specs=[pl.BlockSpec((tm, d), lambda i: (i, 0)),
                  pl.BlockSpec((1, d), lambda i: (0, 0)),
                  _mod_spec(sc, tm, d), _mod_spec(sh, tm, d)],
        out_specs=pl.BlockSpec((tm, d), lambda i: (i, 0)),
        out_shape=jax.ShapeDtypeStruct((t, d), BF16),
        compiler_params=_cparams(("arbitrary",)),
        name="norm_mod",
    )(x, g.reshape(1, d), sc.arr, sh.arr)


def _norm_router_kernel(*refs, nba):
    (xa_ref, xb_ref, g_ref, sca_ref, sha_ref, scb_ref, shb_ref, wr_ref, br_ref,
     o_ref, e_ref, p_ref) = refs
    first = pl.program_id(0) < nba
    x = jnp.where(first, xa_ref[...], xb_ref[...])
    sc = jnp.where(first, sca_ref[...], scb_ref[...])
    sh = jnp.where(first, sha_ref[...], shb_ref[...])
    y = x * lax.rsqrt(jnp.mean(x * x, axis=-1, keepdims=True) + EPS) * g_ref[...]
    t = y * (1.0 + sc) + sh
    t16 = t.astype(BF16)
    half = t.shape[1] // 2
    bits = lax.bitcast_convert_type(t16.astype(F32), jnp.uint32)
    o_ref[...] = lax.shift_right_logical(bits[:, :half], jnp.uint32(16)) | (bits[:, half:] & jnp.uint32(0xFFFF0000))
    lg = _dot(t16, wr_ref[...]) + br_ref[...]
    lane = lax.broadcasted_iota(jnp.int32, lg.shape, 1)
    big = jnp.int32(4 * LANE)
    lgrp = jnp.where(lane < N_GROUPS, lg, NEG)
    mg = jnp.max(lgrp, axis=-1, keepdims=True)
    grp = jnp.min(jnp.where(lgrp == mg, lane, big), axis=-1, keepdims=True)
    p_grp = 1.0 / jnp.sum(jnp.exp(lgrp - mg), axis=-1, keepdims=True)
    lo = N_GROUPS + grp * EPG
    emask = jnp.logical_and(lane >= lo, lane < lo + EPG)
    le = jnp.where(emask, lg, NEG)
    m1 = jnp.max(le, axis=-1, keepdims=True)
    i1 = jnp.min(jnp.where(le == m1, lane, big), axis=-1, keepdims=True)
    le2 = jnp.where(lane == i1, NEG, le)
    m2 = jnp.max(le2, axis=-1, keepdims=True)
    i2 = jnp.min(jnp.where(le2 == m2, lane, big), axis=-1, keepdims=True)
    e2 = jnp.exp(m2 - m1)
    w1 = p_grp / (1.0 + e2)
    w2 = p_grp * e2 / (1.0 + e2)
    e_ref[...] = jnp.where(lane == 0, i1 - N_GROUPS, jnp.where(lane == 1, i2 - N_GROUPS, 0))
    p_ref[...] = jnp.where(lane == 0, w1, jnp.where(lane == 1, w2, 0.0))


def _norm_router(xa, xb, g, sca, sha, scb, shb, wr, br, tm=256):
    ta, d = xa.shape
    tb = xb.shape[0]
    tm = min(tm, math.gcd(ta, tb))
    nba = ta // tm
    nbb = tb // tm
    t_all = ta + tb
    spec_a = pl.BlockSpec((tm, d), lambda i: (jnp.minimum(i, nba - 1), 0))
    spec_b = pl.BlockSpec((tm, d), lambda i: (jnp.maximum(i - nba, 0), 0))
    row = pl.BlockSpec((1, d), lambda i: (0, 0))

    def mod_a(m):
        return pl.BlockSpec((1, d), lambda i: (0, m.col))

    def mod_b(m):
        return pl.BlockSpec((tm, d), lambda i: (jnp.maximum(i - nba, 0), m.col))
    return pl.pallas_call(
        functools.partial(_norm_router_kernel, nba=nba),
        grid=(nba + nbb,),
        in_specs=[spec_a, spec_b, row, mod_a(sca), mod_a(sha), mod_b(scb), mod_b(shb),
                  pl.BlockSpec((d, LANE), lambda i: (0, 0)),
                  pl.BlockSpec((1, LANE), lambda i: (0, 0))],
        out_specs=[pl.BlockSpec((tm, d // 2), lambda i: (i, 0)),
                   pl.BlockSpec((tm, LANE), lambda i: (i, 0)),
                   pl.BlockSpec((tm, LANE), lambda i: (i, 0))],
        out_shape=[jax.ShapeDtypeStruct((t_all, d // 2), jnp.uint32),
                   jax.ShapeDtypeStruct((t_all, LANE), jnp.int32),
                   jax.ShapeDtypeStruct((t_all, LANE), F32)],
        compiler_params=_cparams(("arbitrary",)),
        name="norm_router",
    )(xa, xb, g.reshape(1, d), sca.arr, sha.arr, scb.arr, shb.arr, wr, br)


def _mm_kernel(x_ref, w_ref, o_ref):
    o_ref[...] = _dot(x_ref[...], w_ref[...]).astype(o_ref.dtype)


def _matmul(x, w, tm, tn, out_dtype=F32, name="matmul"):
    m, k = x.shape
    n = w.shape[1]
    tm = min(tm, m)
    tn = min(tn, n)
    return pl.pallas_call(
        _mm_kernel,
        grid=(m // tm, n // tn),
        in_specs=[pl.BlockSpec((tm, k), lambda i, j: (i, 0)),
                  pl.BlockSpec((k, tn), lambda i, j: (0, j))],
        out_specs=pl.BlockSpec((tm, tn), lambda i, j: (i, j)),
        out_shape=jax.ShapeDtypeStruct((m, n), out_dtype),
        compiler_params=_cparams(("arbitrary", "arbitrary")),
        name=name,
    )(x, w)


def _mm_f32w_kernel(x_ref, w_ref, o_ref):
    o_ref[...] = _dot(x_ref[...], w_ref[...].astype(BF16))


def _matmul_f32w(x, w, n, tm, tn, name):
    m, k = x.shape
    tm = min(tm, m)
    return pl.pallas_call(
        _mm_f32w_kernel,
        grid=(m // tm, n // tn),
        in_specs=[pl.BlockSpec((tm, k), lambda i, j: (i, 0)),
                  pl.BlockSpec((k, tn), lambda i, j: (0, j))],
        out_specs=pl.BlockSpec((tm, tn), lambda i, j: (i, j)),
        out_shape=jax.ShapeDtypeStruct((m, n), F32),
        compiler_params=_cparams(("arbitrary", "arbitrary")),
        name=name,
    )(x, w)


def _outproj_kernel(a_ref, b_ref, wa_ref, wb_ref, x_ref, g_ref, o_ref):
    mix = _dot(a_ref[...], wa_ref[...]) + _dot(b_ref[...], wb_ref[...])
    o_ref[...] = x_ref[...] + g_ref[...] * mix


def _outproj(a, b, w, x, gate, tm=1024, tn=512):
    m, ka = a.shape
    kb = b.shape[1]
    n = w.shape[1]
    tm = min(tm, m)
    g0 = gate.col * (n // tn)
    if gate.arr.shape[0] == 1:
        gspec = pl.BlockSpec((1, tn), lambda i, j: (0, g0 + j))
    else:
        gspec = pl.BlockSpec((tm, tn), lambda i, j: (i, g0 + j))
    return pl.pallas_call(
        _outproj_kernel,
        grid=(m // tm, n // tn),
        in_specs=[pl.BlockSpec((tm, ka), lambda i, j: (i, 0)),
                  pl.BlockSpec((tm, kb), lambda i, j: (i, 0)),
                  pl.BlockSpec((ka, tn), lambda i, j: (0, j)),
                  pl.BlockSpec((kb, tn), lambda i, j: (1, j)),
                  pl.BlockSpec((tm, tn), lambda i, j: (i, j)),
                  gspec],
        out_specs=pl.BlockSpec((tm, tn), lambda i, j: (i, j)),
        out_shape=jax.ShapeDtypeStruct((m, n), F32),
        compiler_params=_cparams(("arbitrary", "arbitrary")),
        name="outproj",
    )(a, b, w, w, x, gate.arr)


GDN_GROUP_PROMPT = 8
GDN_GROUP_SAMPLE = 16


def _rowpad(a, rows):
    if a.shape[0] == rows:
        return a
    return jnp.concatenate([a, jnp.zeros((rows - a.shape[0], a.shape[1]), a.dtype)], axis=0)


def _gdn_heads(qn, kn, vh, bcol, gcol, s_prev, lvl_ref, c):
    cp = LANE
    n = len(qn)
    hs = range(n)
    lane = lax.broadcasted_iota(jnp.int32, (c, cp), 1)
    row = lax.broadcasted_iota(jnp.int32, (c, cp), 0)
    tri = lane <= row

    def b16(x):
        return x.astype(BF16)

    g_hi = [b16(g).astype(F32) for g in gcol]
    r1 = [gcol[h] - g_hi[h] for h in hs]
    g_mid = [b16(r).astype(F32) for r in r1]
    g_lo = [r1[h] - g_mid[h] for h in hs]
    a_mat = [b16(jnp.where(lane == 0, g_hi[h], jnp.where(lane == 1, g_mid[h], jnp.where(lane == 2, g_lo[h],
                 jnp.where(lane < 6, 1.0, 0.0))))) for h in hs]
    b_mat = [b16(_rowpad(jnp.where(lane < 3, 1.0, jnp.where(lane == 3, -g_hi[h], jnp.where(lane == 4, -g_mid[h],
                 jnp.where(lane == 5, -g_lo[h], 0.0)))), cp)) for h in hs]
    kn16 = [b16(k) for k in kn]
    qkn16 = [jnp.concatenate([b16(qn[h]), kn16[h]], axis=0) for h in hs]
    knp16 = [_rowpad(k, cp) for k in kn16]
    diff = [_dot_nt(a_mat[h], b_mat[h]) for h in hs]
    qk_kk = [_dot_nt(qkn16[h], knp16[h]) for h in hs]
    decay = [jnp.where(tri, jnp.exp(jnp.minimum(d, 0.0)), 0.0) for d in diff]
    eg = [jnp.exp(g) for g in gcol]
    lmat = [(bcol[h] * qk_kk[h][c:]) * decay[h] for h in hs]
    qkd16 = [b16(qk_kk[h][:c] * decay[h]) for h in hs]
    n_mat = [-(lm * lvl_ref[0]) for lm in lmat]
    for lv in range(1, lvl_ref.shape[0]):
        bm = [lm * lvl_ref[lv] for lm in lmat]
        n16 = [b16(x) for x in n_mat]
        w_mat = [bm[h] + _dot(n16[h], _rowpad(b16(bm[h]), cp)) for h in hs]
        n_mat = [n_mat[h] - w_mat[h] - _dot(b16(w_mat[h]), _rowpad(n16[h], cp)) for h in hs]
    rhs = [jnp.concatenate([vh[h] * bcol[h], kn[h] * (bcol[h] * eg[h])], axis=1) for h in hs]
    sol = [rhs[h] + _dot(b16(n_mat[h]), _rowpad(b16(rhs[h]), cp)) for h in hs]
    kq16 = [jnp.concatenate([b16(sol[h][:, GDN_D:]), b16(qn[h] * eg[h])], axis=0) for h in hs]
    kq_s = [_dot(kq16[h], b16(s_prev[h])) for h in hs]
    u = [sol[h][:, :GDN_D] - kq_s[h][:c] for h in hs]
    u16 = [_rowpad(b16(x), cp) for x in u]
    g_last = [g[c - 1:c, :] for g in gcol]
    kd16 = [_rowpad(b16(kn[h] * jnp.exp(g_last[h] - gcol[h])), cp) for h in hs]
    o = [kq_s[h][c:] + _dot(qkd16[h], u16[h]) for h in hs]
    s_new = [s_prev[h] * jnp.exp(g_last[h]) + _dot_tn(kd16[h], u16[h]) for h in hs]
    return o, s_new


def _gdn_kernel(*refs, c, l_valid, prompt, n_chunks, group):
    if prompt:
        (q_ref, k_ref, v_ref, z_ref, ba_ref, cw_ref, hp_ref, gn_ref, lvl_ref,
         o_ref, sout_ref, cout_ref, xe_ref, cb_ref, s_ref) = refs
    else:
        (q_ref, k_ref, v_ref, z_ref, ba_ref, cw_ref, hp_ref, gn_ref, lvl_ref, cin_ref, sin_ref,
         o_ref, sout_ref, cout_ref, xe_ref, cb_ref) = refs
    i = pl.program_id(0)
    w = GDN_W
    if prompt:
        @pl.when(i == 0)
        def _():
            xe_ref[0:8, :] = jnp.zeros((8, 3 * w), F32)
            s_ref[...] = jnp.zeros(s_ref.shape, F32)
        xe_ref[8:8 + c, 0:w] = q_ref[...]
        xe_ref[8:8 + c, w:2 * w] = k_ref[...]
        xe_ref[8:8 + c, 2 * w:3 * w] = v_ref[...]
    else:
        xe_ref[0:8, :] = cin_ref[0]
        xe_ref[8:8 + c, 0:w] = q_ref[0]
        xe_ref[8:8 + c, w:2 * w] = k_ref[0]
        xe_ref[8:8 + c, 2 * w:3 * w] = v_ref[0]
    rowc = lax.broadcasted_iota(jnp.int32, (c, 512), 0)
    for cb in range(3 * w // 512):
        cs = slice(cb * 512, (cb + 1) * 512)
        acc = xe_ref[8:8 + c, cs] * cw_ref[CONV_W - 1:CONV_W, cs]
        for j in range(CONV_W - 1):
            acc = acc + xe_ref[8 - (CONV_W - 1) + j:8 - (CONV_W - 1) + j + c, cs] * cw_ref[j:j + 1, cs]
        y = _silu(acc)
        if l_valid < c:
            y = jnp.where(rowc < l_valid, y, 0.0)
        cb_ref[:, cs] = y
    new_conv = xe_ref[l_valid:l_valid + 8, :]
    if prompt:
        cout_ref[...] = new_conv
        xe_ref[0:8, :] = new_conv
    else:
        cout_ref[0] = new_conv

    ba = ba_ref[...] if prompt else ba_ref[0]
    row = lax.broadcasted_iota(jnp.int32, (c, LANE), 0)
    beta_all = _sigmoid(ba)
    xg = ba + hp_ref[1:2, :]
    softplus = jnp.maximum(xg, 0.0) + jnp.log1p(jnp.exp(-jnp.abs(xg)))
    g_all = -jnp.exp(hp_ref[0:1, :]) * softplus
    if l_valid < c:
        beta_all = jnp.where(row < l_valid, beta_all, 0.0)
        g_all = jnp.where(row < l_valid, g_all, 0.0)
    gcum = g_all
    s = 1
    while s < c:
        gcum = gcum + jnp.where(row >= s, pltpu.roll(gcum, s, 0), 0.0)
        s *= 2
    gn = gn_ref[...]

    hg = group
    for h0 in range(0, GDN_HEADS, hg):
        hs = range(h0, h0 + hg)
        cols = [slice(h * GDN_D, (h + 1) * GDN_D) for h in hs]
        qh = [cb_ref[:, cl] for cl in cols]
        kh = [cb_ref[:, w + h * GDN_D:w + (h + 1) * GDN_D] for h in hs]
        vh = [cb_ref[:, 2 * w + h * GDN_D:2 * w + (h + 1) * GDN_D] for h in hs]
        qn = [x * lax.rsqrt(jnp.sum(x * x, axis=-1, keepdims=True) + EPS) * (GDN_D ** -0.5) for x in qh]
        kn = [x * lax.rsqrt(jnp.sum(x * x, axis=-1, keepdims=True) + EPS) for x in kh]
        bcol = [beta_all[:, h:h + 1] for h in hs]
        gcol = [gcum[:, h + GDN_HEADS:h + GDN_HEADS + 1] for h in hs]
        s_prev = [s_ref[h] if prompt else sin_ref[0, h] for h in hs]
        o, s_new = _gdn_heads(qn, kn, vh, bcol, gcol, s_prev, lvl_ref, c)
        for j, h in enumerate(hs):
            if prompt:
                s_ref[h] = s_new[j]
            else:
                sout_ref[0, h] = s_new[j]
        zh = [z_ref[:, cl] if prompt else z_ref[0, :, cl] for cl in cols]
        on = [x * lax.rsqrt(jnp.mean(x * x, axis=-1, keepdims=True) + EPS) * gn for x in o]
        for j, cl in enumerate(cols):
            res = (on[j] * _silu(zh[j])).astype(o_ref.dtype)
            if prompt:
                o_ref[:, cl] = res
            else:
                o_ref[0, :, cl] = res
    if prompt:
        @pl.when(i == n_chunks - 1)
        def _():
            sout_ref[...] = s_ref[...]


def _level_masks(c, l_valid):
    i = jnp.arange(c, dtype=jnp.int32)[:, None]
    j = jnp.arange(LANE, dtype=jnp.int32)[None, :]
    masks = []
    s = 1
    while s < l_valid:
        masks.append((i // (2 * s) == j // (2 * s)) & (i % (2 * s) >= s) & (j % (2 * s) < s))
        s *= 2
    return jnp.stack(masks).astype(F32)


def _gdn_prompt(proj, ba_col, conv_w, hp, gn, c=128):
    t = proj.shape[0]
    n_chunks = t // c
    w = GDN_W
    lvl = _level_masks(c, c)
    kern = functools.partial(_gdn_kernel, c=c, l_valid=c, prompt=True, n_chunks=n_chunks, group=GDN_GROUP_PROMPT)
    return pl.pallas_call(
        kern,
        grid=(n_chunks,),
        in_specs=[pl.BlockSpec((c, w), lambda i: (i, 0)),
                  pl.BlockSpec((c, w), lambda i: (i, 1)),
                  pl.BlockSpec((c, w), lambda i: (i, 2)),
                  pl.BlockSpec((c, w), lambda i: (i, 3)),
                  pl.BlockSpec((c, LANE), lambda i: (i, ba_col)),
                  pl.BlockSpec((CONV_W, 3 * w), lambda i: (0, 0)),
                  pl.BlockSpec((8, LANE), lambda i: (0, 0)),
                  pl.BlockSpec((1, GDN_D), lambda i: (0, 0)),
                  pl.BlockSpec(lvl.shape, lambda i: (0, 0, 0))],
        out_specs=[pl.BlockSpec((c, w), lambda i: (i, 0)),
                   pl.BlockSpec((GDN_HEADS, GDN_D, GDN_D), lambda i: (0, 0, 0)),
                   pl.BlockSpec((8, 3 * w), lambda i: (0, 0))],
        out_shape=[jax.ShapeDtypeStruct((t, w), BF16),
                   jax.ShapeDtypeStruct((GDN_HEADS, GDN_D, GDN_D), F32),
                   jax.ShapeDtypeStruct((8, 3 * w), F32)],
        scratch_shapes=[pltpu.VMEM((8 + c, 3 * w), F32),
                        pltpu.VMEM((c, 3 * w), F32),
                        pltpu.VMEM((GDN_HEADS, GDN_D, GDN_D), F32)],
        compiler_params=_cparams(("arbitrary",)),
        name="gdn_prompt",
    )(proj, proj, proj, proj, proj, conv_w, hp, gn, lvl)


def _gdn_sample(proj3, ba_col, conv_w, hp, gn, conv_in, s_in, l):
    b, c, _ = proj3.shape
    w = GDN_W
    lvl = _level_masks(c, l)
    kern = functools.partial(_gdn_kernel, c=c, l_valid=l, prompt=False, n_chunks=1, group=GDN_GROUP_SAMPLE)
    return pl.pallas_call(
        kern,
        grid=(b,),
        in_specs=[pl.BlockSpec((1, c, w), lambda i: (i, 0, 0)),
                  pl.BlockSpec((1, c, w), lambda i: (i, 0, 1)),
                  pl.BlockSpec((1, c, w), lambda i: (i, 0, 2)),
                  pl.BlockSpec((1, c, w), lambda i: (i, 0, 3)),
                  pl.BlockSpec((1, c, LANE), lambda i: (i, 0, ba_col)),
                  pl.BlockSpec((CONV_W, 3 * w), lambda i: (0, 0)),
                  pl.BlockSpec((8, LANE), lambda i: (0, 0)),
                  pl.BlockSpec((1, GDN_D), lambda i: (0, 0)),
                  pl.BlockSpec(lvl.shape, lambda i: (0, 0, 0)),
                  pl.BlockSpec((1, 8, 3 * w), lambda i: (i, 0, 0)),
                  pl.BlockSpec((1, GDN_HEADS, GDN_D, GDN_D), lambda i: (i, 0, 0, 0))],
        out_specs=[pl.BlockSpec((1, c, w), lambda i: (i, 0, 0)),
                   pl.BlockSpec((1, GDN_HEADS, GDN_D, GDN_D), lambda i: (i, 0, 0, 0)),
                   pl.BlockSpec((1, 8, 3 * w), lambda i: (i, 0, 0))],
        out_shape=[jax.ShapeDtypeStruct((b, c, w), F32),
                   jax.ShapeDtypeStruct((b, GDN_HEADS, GDN_D, GDN_D), F32),
                   jax.ShapeDtypeStruct((b, 8, 3 * w), F32)],
        scratch_shapes=[pltpu.VMEM((8 + c, 3 * w), F32),
                        pltpu.VMEM((c, 3 * w), F32)],
        compiler_params=_cparams(("arbitrary",)),
        name="gdn_sample",
    )(proj3, proj3, proj3, proj3, proj3, conv_w, hp, gn, lvl, conv_in, s_in)


def _t5_bucket(dist):
    d = jnp.maximum(dist, 0)
    max_exact = N_BUCKETS // 2
    large = max_exact + (jnp.log(jnp.maximum(d, 1).astype(F32) / max_exact)
                         / math.log(WINDOW / max_exact) * (N_BUCKETS - max_exact)).astype(jnp.int32)
    large = jnp.minimum(large, N_BUCKETS - 1)
    return jnp.where(d < max_exact, d, large)


def _bias_table(dist, rel_bias):
    valid = (dist >= 0) & (dist < WINDOW)
    onehot = (_t5_bucket(dist)[None] == jnp.arange(N_BUCKETS, dtype=jnp.int32)[:, None, None]).astype(F32)
    b = jnp.einsum('bh,bqs->hqs', rel_bias.astype(F32), onehot, precision=lax.Precision.HIGHEST)
    return jnp.where(valid[None], b, NEG)


def _lo_hi(slab, g):
    lane = lax.broadcasted_iota(jnp.int32, slab.shape, 1)
    if g % 2 == 0:
        lo = jnp.where(lane < SWA_HD, slab, 0.0)
        hi = pltpu.roll(lo, SWA_HD, 1)
    else:
        hi = jnp.where(lane >= SWA_HD, slab, 0.0)
        lo = pltpu.roll(hi, SWA_HD, 1)
    return lo, hi


def _sink_softmax_chains(parts, sinks):
    cs = range(len(parts))
    m = [sinks[i] for i in cs]
    for j in range(len(parts[0])):
        mx = [jnp.max(parts[i][j], axis=-1, keepdims=True) for i in cs]
        m = [jnp.maximum(m[i], mx[i]) for i in cs]
    es = [[jnp.exp(p - m[i]) for p in parts[i]] for i in cs]
    den = [jnp.exp(sinks[i] - m[i]) for i in cs]
    for j in range(len(parts[0])):
        sm = [jnp.sum(es[i][j], axis=-1, keepdims=True) for i in cs]
        den = [den[i] + sm[i] for i in cs]
    inv = [1.0 / d for d in den]
    return [[e * inv[i] for e in es[i]] for i in cs]


SWA_GROUP_PROMPT = 8


def _swa_prompt_kernel(q_ref, kp_ref, kc_ref, vp_ref, vc_ref, bias_ref, sink_ref, o_ref):
    n = pl.program_id(0)
    wq = WINDOW
    col = lax.broadcasted_iota(jnp.int32, (wq, 2 * wq), 1)
    first = jnp.logical_and(n == 0, col < wq)
    sink_all = sink_ref[...]
    scale = SWA_HD ** -0.5
    for g0 in range(0, SWA_KV, SWA_GROUP_PROMPT):
        gs = range(g0, g0 + SWA_GROUP_PROMPT)
        kv = {}
        for g in gs:
            sl = slice((g // 2) * LANE, (g // 2 + 1) * LANE)
            kslab = jnp.concatenate([kp_ref[:, sl], kc_ref[:, sl]], axis=0)
            vslab = jnp.concatenate([vp_ref[:, sl], vc_ref[:, sl]], axis=0)
            kv[g] = ([x.astype(BF16) for x in _lo_hi(kslab, g)], [x.astype(BF16) for x in _lo_hi(vslab, g)])
        chains = [(g, s, half) for g in gs for s in range(2) for half in range(2)]
        qs = {(g, s): q_ref[:, (2 * g + s) * LANE:(2 * g + s + 1) * LANE].astype(BF16) for g in gs for s in range(2)}
        logits = [_dot_nt(qs[g, s], kv[g][0][half]) for g, s, half in chains]
        logits = [jnp.where(first, NEG, logits[i] * scale + bias_ref[4 * g + 2 * s + half])
                  for i, (g, s, half) in enumerate(chains)]
        sinks = [sink_all[:, 4 * g + 2 * s + half:4 * g + 2 * s + half + 1] for g, s, half in chains]
        probs = _sink_softmax_chains([[x] for x in logits], sinks)
        pv = [_dot(probs[i][0].astype(BF16), kv[g][1][half]) for i, (g, s, half) in enumerate(chains)]
        for i, (g, s, half) in enumerate(chains):
            if half == 0:
                o_ref[:, (2 * g + s) * LANE:(2 * g + s + 1) * LANE] = (pv[i] + pv[i + 1]).astype(o_ref.dtype)


def _swa_prompt(proj, qcol, bias, sinks):
    t = proj.shape[0]
    nb = t // WINDOW
    kb = qcol * (SWA_W // SWA_KVW) + SWA_W // SWA_KVW
    return pl.pallas_call(
        _swa_prompt_kernel,
        grid=(nb,),
        in_specs=[pl.BlockSpec((WINDOW, SWA_W), lambda n: (n, qcol)),
                  pl.BlockSpec((WINDOW, SWA_KVW), lambda n: (jnp.maximum(n - 1, 0), kb)),
                  pl.BlockSpec((WINDOW, SWA_KVW), lambda n: (n, kb)),
                  pl.BlockSpec((WINDOW, SWA_KVW), lambda n: (jnp.maximum(n - 1, 0), kb + 1)),
                  pl.BlockSpec((WINDOW, SWA_KVW), lambda n: (n, kb + 1)),
                  pl.BlockSpec((SWA_HEADS, WINDOW, 2 * WINDOW), lambda n: (0, 0, 0)),
                  pl.BlockSpec((1, LANE), lambda n: (0, 0))],
        out_specs=pl.BlockSpec((WINDOW, SWA_W), lambda n: (n, 0)),
        out_shape=jax.ShapeDtypeStruct((t, SWA_W), BF16),
        compiler_params=_cparams(("arbitrary",)),
        name="swa_prompt",
    )(proj, proj, proj, proj, proj, bias, sinks)


def _swa_sample_kernel(q_ref, kn_ref, vn_ref, kc_ref, vc_ref, bc_ref, bn_ref, sink_ref, o_ref, ko_ref, vo_ref, *, l):
    r = 8
    wb = WINDOW
    for c_ref, n_ref, out_ref in ((kc_ref, kn_ref, ko_ref), (vc_ref, vn_ref, vo_ref)):
        out_ref[0, 0:wb - l, :] = c_ref[0, l:wb, :]
        out_ref[0, wb - l:wb, :] = n_ref[0, 0:l, :]
    sink_all = sink_ref[...]
    q = q_ref[0]
    knew = kn_ref[0]
    vnew = vn_ref[0]
    scale = SWA_HD ** -0.5
    gs = range(SWA_KV)
    chains = [(g, half) for g in gs for half in range(2)]
    sls = [slice((g // 2) * LANE, (g // 2 + 1) * LANE) for g in gs]
    kc = [[x.astype(BF16) for x in _lo_hi(kc_ref[0, :, sls[g]], g)] for g in gs]
    vc = [[x.astype(BF16) for x in _lo_hi(vc_ref[0, :, sls[g]], g)] for g in gs]
    kn = [[_rowpad(x, LANE).astype(BF16) for x in _lo_hi(knew[:, sls[g]], g)] for g in gs]
    vn = [[_rowpad(x, LANE).astype(BF16) for x in _lo_hi(vnew[:, sls[g]], g)] for g in gs]
    qs = [jnp.concatenate([q[:, (2 * g) * LANE:(2 * g + 1) * LANE],
                           q[:, (2 * g + 1) * LANE:(2 * g + 2) * LANE]], axis=0).astype(BF16) for g in gs]
    lc = [_dot_nt(qs[g], kc[g][half]) for g, half in chains]
    ln = [_dot_nt(qs[g], kn[g][half]) for g, half in chains]
    lc = [lc[i] * scale + bc_ref[2 * g + half] for i, (g, half) in enumerate(chains)]
    ln = [ln[i] * scale + bn_ref[2 * g + half] for i, (g, half) in enumerate(chains)]
    sinks = [jnp.concatenate([jnp.broadcast_to(sink_all[:, 4 * g + half:4 * g + half + 1], (r, 1)),
                              jnp.broadcast_to(sink_all[:, 4 * g + 2 + half:4 * g + 2 + half + 1], (r, 1))], axis=0)
             for g, half in chains]
    probs = _sink_softmax_chains([[lc[i], ln[i]] for i in range(len(chains))], sinks)
    pvc = [_dot(probs[i][0].astype(BF16), vc[g][half]) for i, (g, half) in enumerate(chains)]
    pvn = [_dot(probs[i][1].astype(BF16), vn[g][half]) for i, (g, half) in enumerate(chains)]
    for g in gs:
        acc = (pvc[2 * g] + pvn[2 * g]) + (pvc[2 * g + 1] + pvn[2 * g + 1])
        o_ref[0, :, (2 * g) * LANE:(2 * g + 1) * LANE] = acc[:r]
        o_ref[0, :, (2 * g + 1) * LANE:(2 * g + 2) * LANE] = acc[r:]


def _swa_sample(proj3, qcol, k_cache, v_cache, bias_c, bias_n, sinks, l):
    b, r, _ = proj3.shape
    kb = qcol * (SWA_W // SWA_KVW) + SWA_W // SWA_KVW
    cache_spec = pl.BlockSpec((1, WINDOW, SWA_KVW), lambda i: (i, 0, 0))
    return pl.pallas_call(
        functools.partial(_swa_sample_kernel, l=l),
        grid=(b,),
        in_specs=[pl.BlockSpec((1, r, SWA_W), lambda i: (i, 0, qcol)),
                  pl.BlockSpec((1, r, SWA_KVW), lambda i: (i, 0, kb)),
                  pl.BlockSpec((1, r, SWA_KVW), lambda i: (i, 0, kb + 1)),
                  pl.BlockSpec((1, WINDOW, SWA_KVW), lambda i: (i, 0, 0)),
                  pl.BlockSpec((1, WINDOW, SWA_KVW), lambda i: (i, 0, 0)),
                  pl.BlockSpec((2 * SWA_KV, 2 * r, LANE), lambda i: (0, 0, 0)),
                  pl.BlockSpec((2 * SWA_KV, 2 * r, LANE), lambda i: (0, 0, 0)),
                  pl.BlockSpec((1, LANE), lambda i: (0, 0))],
        out_specs=[pl.BlockSpec((1, r, SWA_W), lambda i: (i, 0, 0)), cache_spec, cache_spec],
        out_shape=[jax.ShapeDtypeStruct((b, r, SWA_W), F32),
                   jax.ShapeDtypeStruct(k_cache.shape, F32),
                   jax.ShapeDtypeStruct(v_cache.shape, F32)],
        compiler_params=_cparams(("arbitrary",)),
        name="swa_sample",
    )(proj3, proj3, proj3, k_cache, v_cache, bias_c, bias_n, sinks)


MOE_WCHUNK = 1024
MOE_WSLOTS = 6
MOE_LOOKAHEAD = 5
MOE_ROWS_SMALL = 256
MOE_ROW_UNROLL = 4


def _moe_kernel(ord_ref, be_ref, bs_ref, bn_ref, h_hbm, wg_hbm, wu_hbm, wd_hbm, y_hbm,
                xbuf, x16, hgu, hid16, acc, ybuf, wbuf, gsem, ssem, wsem, *, n_blocks, t_all):
    k = pl.program_id(0)
    n = bn_ref[k]
    slot = lax.rem(k, 2)
    d = x16.shape[1]
    half = d // 2
    kc = MOE_WCHUNK // 2
    n_gu = d // kc
    n_dn = d // MOE_WCHUNK
    n_ch = n_gu + n_dn
    assert n_ch % MOE_WSLOTS == 0 and wg_hbm.shape[2] == MOE_WCHUNK

    def gather_copy(tok, r):
        return pltpu.make_async_copy(h_hbm.at[pl.ds(tok, 1)], xbuf.at[pl.ds(r, 1)], gsem)

    def scatter_copy(sl, r, a):
        return pltpu.make_async_copy(ybuf.at[sl, pl.ds(r, 1)], y_hbm.at[pl.ds(a, 1)], ssem.at[sl])

    def w_copies(e, j):
        s = j % MOE_WSLOTS
        if j < n_gu:
            rows = pl.ds(j * kc, kc)
            return [pltpu.make_async_copy(wg_hbm.at[e, rows, :], wbuf.at[s, pl.ds(0, kc), :], wsem.at[s]),
                    pltpu.make_async_copy(wu_hbm.at[e, rows, :], wbuf.at[s, pl.ds(kc, kc), :], wsem.at[s])]
        cols = pl.ds((j - n_gu) * MOE_WCHUNK, MOE_WCHUNK)
        return [pltpu.make_async_copy(wd_hbm.at[e, :, cols], wbuf.at[s], wsem.at[s])]

    def for_rows(cnt, fn):
        groups = lax.shift_right_logical(cnt, MOE_ROW_UNROLL.bit_length() - 1)

        def group(q, carry):
            for u in range(MOE_ROW_UNROLL):
                fn(q * MOE_ROW_UNROLL + u)
            return carry

        def single(r, carry):
            fn(r)
            return carry
        lax.fori_loop(0, groups, group, 0)
        lax.fori_loop(groups * MOE_ROW_UNROLL, cnt, single, 0)

    def gather_start(kk):
        st = bs_ref[kk]

        def one(r):
            a = ord_ref[st + r]
            gather_copy(jnp.where(a >= t_all, a - t_all, a), r).start()
        for_rows(bn_ref[kk], one)

    def gather_wait(kk):
        for_rows(bn_ref[kk], lambda r: gather_copy(0, 0).wait())

    def scatter_start(kk, sl):
        st = bs_ref[kk]
        for_rows(bn_ref[kk], lambda r: scatter_copy(sl, r, ord_ref[st + r]).start())

    def scatter_wait(kk, sl):
        for_rows(bn_ref[kk], lambda r: scatter_copy(sl, 0, 0).wait())

    @pl.when(n > 0)
    def _():
        e = be_ref[k]
        k1 = jnp.minimum(k + 1, n_blocks - 1)
        has_next = jnp.logical_and(k + 1 < n_blocks, bn_ref[k1] > 0)
        e_next = be_ref[k1]

        @pl.when(k == 0)
        def _():
            xbuf[...] = jnp.zeros(xbuf.shape, xbuf.dtype)
            gather_start(0)
            for j in range(MOE_LOOKAHEAD):
                for cp in w_copies(e, j):
                    cp.start()
        gather_wait(k)
        words = xbuf[...]
        lo = lax.bitcast_convert_type(lax.shift_left(words, jnp.uint32(16)), F32)
        hi = lax.bitcast_convert_type(words & jnp.uint32(0xFFFF0000), F32)
        x16[:, 0:half] = lo.astype(BF16)
        x16[:, half:d] = hi.astype(BF16)

        @pl.when(has_next)
        def _():
            gather_start(k1)

        def ffn(m):
            for j in range(n_ch):
                jn = j + MOE_LOOKAHEAD
                if jn < n_ch:
                    for cp in w_copies(e, jn):
                        cp.start()
                else:
                    @pl.when(has_next)
                    def _():
                        for cp in w_copies(e_next, jn - n_ch):
                            cp.start()
                for cp in w_copies(e, j):
                    cp.wait()
                s = j % MOE_WSLOTS
                if j < n_gu:
                    xs = x16[0:m, j * kc:(j + 1) * kc]
                    pg = _dot(xs, wbuf[s, 0:kc, :].astype(BF16))
                    pu = _dot(xs, wbuf[s, kc:2 * kc, :].astype(BF16))
                    if j == 0:
                        hgu[0, 0:m] = pg
                        hgu[1, 0:m] = pu
                    else:
                        hgu[0, 0:m] += pg
                        hgu[1, 0:m] += pu
                    if j == n_gu - 1:
                        hid16[0:m] = (_silu(hgu[0, 0:m]) * hgu[1, 0:m]).astype(BF16)
                else:
                    cols = slice((j - n_gu) * MOE_WCHUNK, (j - n_gu + 1) * MOE_WCHUNK)
                    acc[0:m, cols] = _dot(hid16[0:m], wbuf[s].astype(BF16))
            bits = lax.bitcast_convert_type(acc[0:m].astype(BF16).astype(F32), jnp.uint32)
            ybuf[slot, 0:m] = (lax.shift_right_logical(bits[:, :half], jnp.uint32(16))
                               | (bits[:, half:] & jnp.uint32(0xFFFF0000)))

        @pl.when(n <= MOE_ROWS_SMALL)
        def _():
            ffn(MOE_ROWS_SMALL)

        @pl.when(n > MOE_ROWS_SMALL)
        def _():
            ffn(MOE_ROWS)
        scatter_start(k, slot)

        @pl.when(k > 0)
        def _():
            scatter_wait(k - 1, 1 - slot)

        @pl.when(jnp.logical_not(has_next))
        def _():
            scatter_wait(k, slot)


def _moe_ffn(order, blk_e, blk_start, blk_n, h_packed, w_gate, w_up, w_down):
    a = order.shape[0]
    t_all = h_packed.shape[0]
    d = 2 * h_packed.shape[1]
    n_blocks = blk_e.shape[0]
    grid_spec = pltpu.PrefetchScalarGridSpec(
        num_scalar_prefetch=4,
        grid=(n_blocks,),
        in_specs=[pl.BlockSpec(memory_space=pl.ANY)] * 4,
        out_specs=pl.BlockSpec(memory_space=pl.ANY),
        scratch_shapes=[pltpu.VMEM((MOE_ROWS, d // 2), jnp.uint32),
                        pltpu.VMEM((MOE_ROWS, d), BF16),
                        pltpu.VMEM((2, MOE_ROWS, D_EXPERT), F32),
                        pltpu.VMEM((MOE_ROWS, D_EXPERT), BF16),
                        pltpu.VMEM((MOE_ROWS, d), F32),
                        pltpu.VMEM((2, MOE_ROWS, d // 2), jnp.uint32),
                        pltpu.VMEM((MOE_WSLOTS, MOE_WCHUNK, MOE_WCHUNK), F32),
                        pltpu.SemaphoreType.DMA(()),
                        pltpu.SemaphoreType.DMA((2,)),
                        pltpu.SemaphoreType.DMA((MOE_WSLOTS,))],
    )
    return pl.pallas_call(
        functools.partial(_moe_kernel, n_blocks=n_blocks, t_all=t_all),
        grid_spec=grid_spec,
        out_shape=jax.ShapeDtypeStruct((a, d // 2), jnp.uint32),
        compiler_params=_cparams(("arbitrary",)),
        name="moe_ffn",
    )(order, blk_e, blk_start, blk_n, h_packed, w_gate, w_up, w_down)


def _unpack_pairs(words):
    lo = lax.bitcast_convert_type(lax.shift_left(words, jnp.uint32(16)), F32)
    hi = lax.bitcast_convert_type(words & jnp.uint32(0xFFFF0000), F32)
    return jnp.concatenate([lo, hi], axis=1)


def _final_kernel(x_ref, y0_ref, y1_ref, p_ref, g_ref, nf_ref, o_ref):
    p = p_ref[...]
    moe = _unpack_pairs(y0_ref[...]) * p[:, 0:1] + _unpack_pairs(y1_ref[...]) * p[:, 1:2]
    x = x_ref[...] + g_ref[...] * moe
    o_ref[...] = x * lax.rsqrt(jnp.mean(x * x, axis=-1, keepdims=True) + EPS) * nf_ref[...]


def _final(x, y, row0, gates, gt2, nf, tm=256):
    t, d = x.shape
    t_all = gates.shape[0]
    tm = math.gcd(math.gcd(tm, t), math.gcd(row0, t_all))
    b0 = row0 // tm
    b1 = t_all // tm + b0
    return pl.pallas_call(
        _final_kernel,
        grid=(t // tm,),
        in_specs=[pl.BlockSpec((tm, d), lambda i: (i, 0)),
                  pl.BlockSpec((tm, d // 2), lambda i: (b0 + i, 0)),
                  pl.BlockSpec((tm, d // 2), lambda i: (b1 + i, 0)),
                  pl.BlockSpec((tm, LANE), lambda i: (b0 + i, 0)),
                  _mod_spec(gt2, tm, d),
                  pl.BlockSpec((1, d), lambda i: (0, 0))],
        out_specs=pl.BlockSpec((tm, d), lambda i: (i, 0)),
        out_shape=jax.ShapeDtypeStruct((t, d), F32),
        compiler_params=_cparams(("arbitrary",)),
        name="final",
    )(x, y, y, gates, gt2.arr, nf.reshape(1, d))


def _dispatch(expert_idx):
    t, k = expert_idx.shape
    a = t * k
    n_blocks = a // MOE_ROWS + N_EXPERTS
    flat_e = expert_idx.T.reshape(a)
    order = jnp.argsort(flat_e, stable=True).astype(jnp.int32)
    counts = jnp.zeros((N_EXPERTS,), jnp.int32).at[flat_e].add(1)
    seg_start = jnp.cumsum(counts) - counts
    nblk = (counts + MOE_ROWS - 1) // MOE_ROWS
    blk_end = jnp.cumsum(nblk)
    n_active = blk_end[-1]
    kk = jnp.minimum(jnp.arange(n_blocks, dtype=jnp.int32), n_active - 1)
    blk_e = jnp.minimum(jnp.searchsorted(blk_end, kk, side='right'), N_EXPERTS - 1).astype(jnp.int32)
    j = kk - (blk_end[blk_e] - nblk[blk_e])
    blk_start = (seg_start[blk_e] + j * MOE_ROWS).astype(jnp.int32)
    blk_n = jnp.clip(counts[blk_e] - j * MOE_ROWS, 0, MOE_ROWS)
    blk_n = jnp.where(jnp.arange(n_blocks) < n_active, blk_n, 0).astype(jnp.int32)
    return order, blk_e, blk_start, blk_n


def kernel(x_prompt, x_sample, state_gdn, state_conv, cache_swa_k, cache_swa_v, c_prompt, c_sample, w_ada, b_ada, norm_mix, w_in, conv_w, a_log, dt_bias, gdn_norm, swa_sinks, rel_bias, w_out, norm_moe, router_group, router_group_bias, router_expert, router_expert_bias, w_gate, w_up, w_down, norm_final):
    depth = w_ada.shape[0]
    assert depth == 1
    bp, seq, d = x_prompt.shape
    assert bp == 1 and seq % WINDOW == 0
    bs, ls, _ = x_sample.shape
    tp = bp * seq
    ts = bs * ls
    n_main = 2 * GDN_W + 2 * GDN_W
    n_ba = 2 * GDN_HEADS

    c_all = jnp.concatenate([c_prompt, c_sample], axis=0)
    m_pad = -(-c_all.shape[0] // 16) * 16
    c_all = jnp.pad(c_all, ((0, m_pad - c_all.shape[0]), (0, 0)))
    mod = _adaln(c_all, w_ada[0], b_ada[0])
    mod_rows_s = jnp.repeat(mod[1:1 + bs], ls, axis=0)
    mod_p = [Mod(mod[0:1], j) for j in range(6)]
    mod_s = [Mod(mod_rows_s, j) for j in range(6)]

    n_gdn = n_main + 512
    w_swa = w_in[0][:, n_main + n_ba:]
    w_out16 = w_out[0].astype(BF16)
    ba_col = n_main // LANE
    kcol = SWA_W
    hp = jnp.zeros((8, LANE), F32)
    hp = hp.at[0, GDN_HEADS:2 * GDN_HEADS].set(a_log[0]).at[1, GDN_HEADS:2 * GDN_HEADS].set(dt_bias[0])
    gn = gdn_norm[0].reshape(1, GDN_D)
    sinks = jnp.pad(swa_sinks[0], (0, LANE - SWA_HEADS)).reshape(1, LANE)

    xp = x_prompt.reshape(tp, d)
    h_p = _norm_mod(xp, norm_mix[0], mod_p[1], mod_p[0])
    proj_p = _matmul_f32w(h_p, w_in[0], n_gdn, 1024, 512, name="inproj_p")
    pswa_p = _matmul_f32w(h_p, w_swa, w_swa.shape[1], 1024, 512, name="inproj_swa_p")
    og_p, s_p, conv_p = _gdn_prompt(proj_p, ba_col, conv_w[0], hp, gn)
    conv_p = conv_p[8 - (CONV_W - 1):]
    qi = jnp.arange(WINDOW, dtype=jnp.int32)
    sj = jnp.arange(2 * WINDOW, dtype=jnp.int32)
    bias_p = _bias_table((WINDOW + qi)[:, None] - sj[None, :], rel_bias)
    os_p = _swa_prompt(pswa_p, 0, bias_p, sinks)
    k_p = pswa_p[tp - WINDOW:, kcol:kcol + SWA_KVW]
    v_p = pswa_p[tp - WINDOW:, kcol + SWA_KVW:kcol + 2 * SWA_KVW]
    x1_p = _outproj(og_p, os_p, w_out16, xp, mod_p[2])

    xs = x_sample.reshape(ts, d)
    h_s = _norm_mod(xs, norm_mix[0], mod_s[1], mod_s[0])
    proj_s = _matmul_f32w(h_s, w_in[0], n_gdn, 512, 512, name="inproj_s")
    pswa_s = _matmul_f32w(h_s, w_swa, w_swa.shape[1], 512, 512, name="inproj_swa_s")
    rs = 8
    assert CONV_W - 1 <= ls <= rs
    proj_s8 = jnp.pad(proj_s.reshape(bs, ls, n_gdn), ((0, 0), (0, rs - ls), (0, 0)))
    pswa_s8 = jnp.pad(pswa_s.reshape(bs, ls, pswa_s.shape[1]), ((0, 0), (0, rs - ls), (0, 0)))
    conv_in8 = jnp.pad(state_conv[0], ((0, 0), (8 - (CONV_W - 1), 0), (0, 0)))
    og_s, s_s, conv_s = _gdn_sample(proj_s8, ba_col, conv_w[0], hp, gn, conv_in8, state_gdn[0], ls)
    conv_s = conv_s[:, 8 - (CONV_W - 1):]
    wb = cache_swa_k.shape[2]
    assert wb == WINDOW == LANE
    dist_s = (wb + jnp.arange(ls, dtype=jnp.int32))[:, None] - jnp.arange(wb + ls, dtype=jnp.int32)[None, :]
    bias_s = _bias_table(dist_s, rel_bias)
    bias_s = jnp.pad(bias_s, ((0, 0), (0, rs - ls), (0, 2 * LANE - wb - ls)), constant_values=NEG)
    bias_s = bias_s.reshape(SWA_KV, 2, 2, rs, 2 * LANE)
    bias_s = jnp.transpose(bias_s, (0, 2, 1, 3, 4)).reshape(2 * SWA_KV, 2 * rs, 2 * LANE)
    k_cache = cache_swa_k[0].reshape(bs, wb, SWA_KVW)
    v_cache = cache_swa_v[0].reshape(bs, wb, SWA_KVW)
    os_s, k_s, v_s = _swa_sample(pswa_s8, 0, k_cache, v_cache, bias_s[:, :, :LANE], bias_s[:, :, LANE:], sinks, ls)
    x1_s = _outproj(og_s[:, :ls].reshape(ts, GDN_W).astype(BF16), os_s[:, :ls].reshape(ts, SWA_W).astype(BF16),
                    w_out16, xs, mod_s[2], tm=512)

    wr = jnp.concatenate([router_group[0], router_expert[0]], axis=1)
    wr2 = jnp.pad(wr, ((0, 0), (0, LANE - wr.shape[1]))).astype(BF16)
    br = jnp.pad(jnp.concatenate([router_group_bias[0], router_expert_bias[0]]),
                 (0, LANE - N_GROUPS - N_EXPERTS)).reshape(1, LANE)
    t_all = tp + ts
    h2, eidx, gates = _norm_router(x1_p, x1_s, norm_moe[0], mod_p[4], mod_p[3], mod_s[4], mod_s[3], wr2, br)
    order, blk_e, blk_start, blk_n = _dispatch(eidx[:, :2])
    y = _moe_ffn(order, blk_e, blk_start, blk_n, h2, w_gate[0], w_up[0], w_down[0])
    y_p = _final(x1_p, y, 0, gates, mod_p[5], norm_final)
    y_s = _final(x1_s, y, tp, gates, mod_s[5], norm_final)

    sdt = state_gdn.dtype
    return (y_p.reshape(bp, seq, d), y_s.reshape(bs, ls, d),
            s_p.reshape(1, bp, GDN_HEADS, GDN_D, GDN_D).astype(sdt), s_s[None].astype(sdt),
            conv_p.reshape(1, bp, CONV_W - 1, 3 * GDN_W).astype(state_conv.dtype), conv_s[None].astype(state_conv.dtype),
            k_p.reshape(1, bp, WINDOW, SWA_KV, SWA_HD).astype(cache_swa_k.dtype),
            k_s.reshape(1, bs, wb, SWA_KV, SWA_HD).astype(cache_swa_k.dtype),
            v_p.reshape(1, bp, WINDOW, SWA_KV, SWA_HD).astype(cache_swa_v.dtype),
            v_s.reshape(1, bs, wb, SWA_KV, SWA_HD).astype(cache_swa_v.dtype))
```

```python
import functools
import math
from typing import NamedTuple

import jax
import jax.numpy as jnp
from jax import lax
from jax.experimental import pallas as pl
from jax.experimental.pallas import tpu as pltpu

F32 = jnp.float32
BF16 = jnp.bfloat16
EPS = 1e-6
NEG = -1e30

LANE = 128
GDN_HEADS = 16
GDN_D = 128
GDN_W = GDN_HEADS * GDN_D
CONV_W = 4
SWA_HEADS = 32
SWA_KV = 8
SWA_HD = 64
SWA_W = SWA_HEADS * SWA_HD
SWA_KVW = SWA_KV * SWA_HD
WINDOW = 128
N_BUCKETS = 32
N_GROUPS = 8
EPG = 8
N_EXPERTS = 64
D_EXPERT = 1024
MOE_ROWS = 448
VMEM_LIMIT = 56 * 1024 * 1024


def _cparams(sem):
    return pltpu.CompilerParams(dimension_semantics=sem, vmem_limit_bytes=VMEM_LIMIT)


def _sigmoid(x):
    return 1.0 / (1.0 + jnp.exp(-x))


def _silu(x):
    return x * _sigmoid(x)


def _dot(a, b):
    return jnp.dot(a, b, preferred_element_type=F32)


def _dot_nt(a, b):
    return lax.dot_general(a, b, (((1,), (1,)), ((), ())), preferred_element_type=F32)


def _dot_tn(a, b):
    return lax.dot_general(a, b, (((0,), (0,)), ((), ())), preferred_element_type=F32)


def _ada_kernel(c_ref, w_ref, b_ref, o_ref):
    a = _silu(c_ref[...]).astype(BF16)
    o_ref[...] = _dot(a, w_ref[...].astype(BF16)) + b_ref[...]


def _adaln(c, w, b, tn=512):
    m, d = c.shape
    n = w.shape[1]
    return pl.pallas_call(
        _ada_kernel,
        grid=(n // tn,),
        in_specs=[pl.BlockSpec((m, d), lambda j: (0, 0)),
                  pl.BlockSpec((d, tn), lambda j: (0, j)),
                  pl.BlockSpec((1, tn), lambda j: (0, j))],
        out_specs=pl.BlockSpec((m, tn), lambda j: (0, j)),
        out_shape=jax.ShapeDtypeStruct((m, n), F32),
        compiler_params=_cparams(("arbitrary",)),
        name="adaln",
    )(c, w, b.reshape(1, n))


def _norm_kernel(x_ref, g_ref, sc_ref, sh_ref, o_ref):
    x = x_ref[...]
    y = x * lax.rsqrt(jnp.mean(x * x, axis=-1, keepdims=True) + EPS) * g_ref[...]
    o_ref[...] = (y * (1.0 + sc_ref[...]) + sh_ref[...]).astype(o_ref.dtype)


class Mod(NamedTuple):
    arr: jax.Array
    col: int


def _mod_spec(mod, tm, d):
    if mod.arr.shape[0] == 1:
        return pl.BlockSpec((1, d), lambda i: (0, mod.col))
    return pl.BlockSpec((tm, d), lambda i: (i, mod.col))


def _norm_mod(x, g, sc, sh, tm=256):
    t, d = x.shape
    tm = min(tm, t)
    return pl.pallas_call(
        _norm_kernel,
        grid=(t // tm,),
        in_specs=[pl.BlockSpec((tm, d), lambda i: (i, 0)),
                  pl.BlockSpec((1, d), lambda i: (0, 0)),
                  _mod_spec(sc, tm, d), _mod_spec(sh, tm, d)],
        out_specs=pl.BlockSpec((tm, d), lambda i: (i, 0)),
        out_shape=jax.ShapeDtypeStruct((t, d), BF16),
        compiler_params=_cparams(("arbitrary",)),
        name="norm_mod",
    )(x, g.reshape(1, d), sc.arr, sh.arr)


def _norm_router_kernel(*refs, nba):
    (xa_ref, xb_ref, g_ref, sca_ref, sha_ref, scb_ref, shb_ref, wr_ref, br_ref,
     o_ref, e_ref, p_ref) = refs
    first = pl.program_id(0) < nba
    x = jnp.where(first, xa_ref[...], xb_ref[...])
    sc = jnp.where(first, sca_ref[...], scb_ref[...])
    sh = jnp.where(first, sha_ref[...], shb_ref[...])
    y = x * lax.rsqrt(jnp.mean(x * x, axis=-1, keepdims=True) + EPS) * g_ref[...]
    t = y * (1.0 + sc) + sh
    t16 = t.astype(BF16)
    half = t.shape[1] // 2
    bits = lax.bitcast_convert_type(t16.astype(F32), jnp.uint32)
    o_ref[...] = lax.shift_right_logical(bits[:, :half], jnp.uint32(16)) | (bits[:, half:] & jnp.uint32(0xFFFF0000))
    lg = _dot(t16, wr_ref[...]) + br_ref[...]
    lane = lax.broadcasted_iota(jnp.int32, lg.shape, 1)
    big = jnp.int32(4 * LANE)
    lgrp = jnp.where(lane < N_GROUPS, lg, NEG)
    mg = jnp.max(lgrp, axis=-1, keepdims=True)
    grp = jnp.min(jnp.where(lgrp == mg, lane, big), axis=-1, keepdims=True)
    p_grp = 1.0 / jnp.sum(jnp.exp(lgrp - mg), axis=-1, keepdims=True)
    lo = N_GROUPS + grp * EPG
    emask = jnp.logical_and(lane >= lo, lane < lo + EPG)
    le = jnp.where(emask, lg, NEG)
    m1 = jnp.max(le, axis=-1, keepdims=True)
    i1 = jnp.min(jnp.where(le == m1, lane, big), axis=-1, keepdims=True)
    le2 = jnp.where(lane == i1, NEG, le)
    m2 = jnp.max(le2, axis=-1, keepdims=True)
    i2 = jnp.min(jnp.where(le2 == m2, lane, big), axis=-1, keepdims=True)
    e2 = jnp.exp(m2 - m1)
    w1 = p_grp / (1.0 + e2)
    w2 = p_grp * e2 / (1.0 + e2)
    e_ref[...] = jnp.where(lane == 0, i1 - N_GROUPS, jnp.where(lane == 1, i2 - N_GROUPS, 0))
    p_ref[...] = jnp.where(lane == 0, w1, jnp.where(lane == 1, w2, 0.0))


def _norm_router(xa, xb, g, sca, sha, scb, shb, wr, br, tm=256):
    ta, d = xa.shape
    tb = xb.shape[0]
    tm = min(tm, math.gcd(ta, tb))
    nba = ta // tm
    nbb = tb // tm
    t_all = ta + tb
    spec_a = pl.BlockSpec((tm, d), lambda i: (jnp.minimum(i, nba - 1), 0))
    spec_b = pl.BlockSpec((tm, d), lambda i: (jnp.maximum(i - nba, 0), 0))
    row = pl.BlockSpec((1, d), lambda i: (0, 0))

    def mod_a(m):
        return pl.BlockSpec((1, d), lambda i: (0, m.col))

    def mod_b(m):
        return pl.BlockSpec((tm, d), lambda i: (jnp.maximum(i - nba, 0), m.col))
    return pl.pallas_call(
        functools.partial(_norm_router_kernel, nba=nba),
        grid=(nba + nbb,),
        in_specs=[spec_a, spec_b, row, mod_a(sca), mod_a(sha), mod_b(scb), mod_b(shb),
                  pl.BlockSpec((d, LANE), lambda i: (0, 0)),
                  pl.BlockSpec((1, LANE), lambda i: (0, 0))],
        out_specs=[pl.BlockSpec((tm, d // 2), lambda i: (i, 0)),
                   pl.BlockSpec((tm, LANE), lambda i: (i, 0)),
                   pl.BlockSpec((tm, LANE), lambda i: (i, 0))],
        out_shape=[jax.ShapeDtypeStruct((t_all, d // 2), jnp.uint32),
                   jax.ShapeDtypeStruct((t_all, LANE), jnp.int32),
                   jax.ShapeDtypeStruct((t_all, LANE), F32)],
        compiler_params=_cparams(("arbitrary",)),
        name="norm_router",
    )(xa, xb, g.reshape(1, d), sca.arr, sha.arr, scb.arr, shb.arr, wr, br)


def _mm_kernel(x_ref, w_ref, o_ref):
    o_ref[...] = _dot(x_ref[...], w_ref[...]).astype(o_ref.dtype)


def _matmul(x, w, tm, tn, out_dtype=F32, name="matmul"):
    m, k = x.shape
    n = w.shape[1]
    tm = min(tm, m)
    tn = min(tn, n)
    return pl.pallas_call(
        _mm_kernel,
        grid=(m // tm, n // tn),
        in_specs=[pl.BlockSpec((tm, k), lambda i, j: (i, 0)),
                  pl.BlockSpec((k, tn), lambda i, j: (0, j))],
        out_specs=pl.BlockSpec((tm, tn), lambda i, j: (i, j)),
        out_shape=jax.ShapeDtypeStruct((m, n), out_dtype),
        compiler_params=_cparams(("arbitrary", "arbitrary")),
        name=name,
    )(x, w)


def _mm_f32wt_kernel(x_ref, wt_ref, o_ref):
    o_ref[...] = _dot_nt(x_ref[...], wt_ref[...].astype(BF16))


def _matmul_f32wt(x, wt, n, tm, tn, name):
    m, k = x.shape
    tm = min(tm, m)
    return pl.pallas_call(
        _mm_f32wt_kernel,
        grid=(m // tm, n // tn),
        in_specs=[pl.BlockSpec((tm, k), lambda i, j: (i, 0)),
                  pl.BlockSpec((tn, k), lambda i, j: (j, 0))],
        out_specs=pl.BlockSpec((tm, tn), lambda i, j: (i, j)),
        out_shape=jax.ShapeDtypeStruct((m, n), F32),
        compiler_params=_cparams(("arbitrary", "arbitrary")),
        name=name,
    )(x, wt)


def _outproj_kernel(a_ref, b_ref, wa_ref, wb_ref, x_ref, g_ref, o_ref):
    mix = _dot(a_ref[...], wa_ref[...]) + _dot(b_ref[...], wb_ref[...])
    o_ref[...] = x_ref[...] + g_ref[...] * mix


def _outproj(a, b, w, x, gate, tm=1024, tn=512):
    m, ka = a.shape
    kb = b.shape[1]
    n = w.shape[1]
    tm = min(tm, m)
    g0 = gate.col * (n // tn)
    if gate.arr.shape[0] == 1:
        gspec = pl.BlockSpec((1, tn), lambda i, j: (0, g0 + j))
    else:
        gspec = pl.BlockSpec((tm, tn), lambda i, j: (i, g0 + j))
    return pl.pallas_call(
        _outproj_kernel,
        grid=(m // tm, n // tn),
        in_specs=[pl.BlockSpec((tm, ka), lambda i, j: (i, 0)),
                  pl.BlockSpec((tm, kb), lambda i, j: (i, 0)),
                  pl.BlockSpec((ka, tn), lambda i, j: (0, j)),
                  pl.BlockSpec((kb, tn), lambda i, j: (1, j)),
                  pl.BlockSpec((tm, tn), lambda i, j: (i, j)),
                  gspec],
        out_specs=pl.BlockSpec((tm, tn), lambda i, j: (i, j)),
        out_shape=jax.ShapeDtypeStruct((m, n), F32),
        compiler_params=_cparams(("arbitrary", "arbitrary")),
        name="outproj",
    )(a, b, w, w, x, gate.arr)


GDN_GROUP_PROMPT = 8
GDN_SAMPLE_SEQS = 4
GDN_GROUP_SAMPLE = 32


def _rowpad(a, rows):
    if a.shape[0] == rows:
        return a
    return jnp.concatenate([a, jnp.zeros((rows - a.shape[0], a.shape[1]), a.dtype)], axis=0)


def _gdn_heads(qn, kn, vh, bcol, gcol, s_prev, lvl_ref, c):
    cp = LANE
    n = len(qn)
    hs = range(n)
    lane = lax.broadcasted_iota(jnp.int32, (c, cp), 1)
    row = lax.broadcasted_iota(jnp.int32, (c, cp), 0)
    tri = lane <= row

    def b16(x):
        return x.astype(BF16)

    g_hi = [b16(g).astype(F32) for g in gcol]
    r1 = [gcol[h] - g_hi[h] for h in hs]
    g_mid = [b16(r).astype(F32) for r in r1]
    g_lo = [r1[h] - g_mid[h] for h in hs]
    a_mat = [b16(jnp.where(lane == 0, g_hi[h], jnp.where(lane == 1, g_mid[h], jnp.where(lane == 2, g_lo[h],
                 jnp.where(lane < 6, 1.0, 0.0))))) for h in hs]
    b_mat = [b16(_rowpad(jnp.where(lane < 3, 1.0, jnp.where(lane == 3, -g_hi[h], jnp.where(lane == 4, -g_mid[h],
                 jnp.where(lane == 5, -g_lo[h], 0.0)))), cp)) for h in hs]
    kn16 = [b16(k) for k in kn]
    qkn16 = [jnp.concatenate([b16(qn[h]), kn16[h]], axis=0) for h in hs]
    knp16 = [_rowpad(k, cp) for k in kn16]
    diff = [_dot_nt(a_mat[h], b_mat[h]) for h in hs]
    qk_kk = [_dot_nt(qkn16[h], knp16[h]) for h in hs]
    decay = [jnp.where(tri, jnp.exp(jnp.minimum(d, 0.0)), 0.0) for d in diff]
    eg = [jnp.exp(g) for g in gcol]
    lmat = [(bcol[h] * qk_kk[h][c:]) * decay[h] for h in hs]
    qkd16 = [b16(qk_kk[h][:c] * decay[h]) for h in hs]
    n_mat = [-(lm * lvl_ref[0]) for lm in lmat]
    for lv in range(1, lvl_ref.shape[0]):
        bm = [lm * lvl_ref[lv] for lm in lmat]
        n16 = [b16(x) for x in n_mat]
        w_mat = [bm[h] + _dot(n16[h], _rowpad(b16(bm[h]), cp)) for h in hs]
        n_mat = [n_mat[h] - w_mat[h] - _dot(b16(w_mat[h]), _rowpad(n16[h], cp)) for h in hs]
    rhs = [jnp.concatenate([vh[h] * bcol[h], kn[h] * (bcol[h] * eg[h])], axis=1) for h in hs]
    sol = [rhs[h] + _dot(b16(n_mat[h]), _rowpad(b16(rhs[h]), cp)) for h in hs]
    kq16 = [jnp.concatenate([b16(sol[h][:, GDN_D:]), b16(qn[h] * eg[h])], axis=0) for h in hs]
    kq_s = [_dot(kq16[h], b16(s_prev[h])) for h in hs]
    u = [sol[h][:, :GDN_D] - kq_s[h][:c] for h in hs]
    u16 = [_rowpad(b16(x), cp) for x in u]
    g_last = [g[c - 1:c, :] for g in gcol]
    kd16 = [_rowpad(b16(kn[h] * jnp.exp(g_last[h] - gcol[h])), cp) for h in hs]
    o = [kq_s[h][c:] + _dot(qkd16[h], u16[h]) for h in hs]
    s_new = [s_prev[h] * jnp.exp(g_last[h]) + _dot_tn(kd16[h], u16[h]) for h in hs]
    return o, s_new


def _gdn_kernel(*refs, c, l_valid, prompt, n_chunks, group):
    if prompt:
        (q_ref, k_ref, v_ref, z_ref, ba_ref, cw_ref, hp_ref, gn_ref, lvl_ref,
         o_ref, sout_ref, cout_ref, xe_ref, cb_ref, s_ref) = refs
        nseq = 1
    else:
        (q_ref, k_ref, v_ref, z_ref, ba_ref, cw_ref, hp_ref, gn_ref, lvl_ref, cin_ref, sin_ref,
         o_ref, sout_ref, cout_ref, xe_ref, cb_ref) = refs
        nseq = q_ref.shape[0]
    i = pl.program_id(0)
    w = GDN_W

    def blk(ref, b):
        return ref if prompt else ref.at[b]

    if prompt:
        @pl.when(i == 0)
        def _():
            xe_ref[0, 0:8, :] = jnp.zeros((8, 3 * w), F32)
            s_ref[...] = jnp.zeros(s_ref.shape, F32)
    rowc = lax.broadcasted_iota(jnp.int32, (c, 512), 0)
    row = lax.broadcasted_iota(jnp.int32, (c, LANE), 0)
    beta_all = []
    gcum_all = []
    for b in range(nseq):
        if not prompt:
            xe_ref[b, 0:8, :] = cin_ref[b]
        xe_ref[b, 8:8 + c, 0:w] = blk(q_ref, b)[...]
        xe_ref[b, 8:8 + c, w:2 * w] = blk(k_ref, b)[...]
        xe_ref[b, 8:8 + c, 2 * w:3 * w] = blk(v_ref, b)[...]
        for cb in range(3 * w // 512):
            cs = slice(cb * 512, (cb + 1) * 512)
            acc = xe_ref[b, 8:8 + c, cs] * cw_ref[CONV_W - 1:CONV_W, cs]
            for j in range(CONV_W - 1):
                acc = acc + xe_ref[b, 8 - (CONV_W - 1) + j:8 - (CONV_W - 1) + j + c, cs] * cw_ref[j:j + 1, cs]
            y = _silu(acc)
            if l_valid < c:
                y = jnp.where(rowc < l_valid, y, 0.0)
            cb_ref[b, :, cs] = y
        new_conv = xe_ref[b, l_valid:l_valid + 8, :]
        blk(cout_ref, b)[...] = new_conv
        if prompt:
            xe_ref[b, 0:8, :] = new_conv

        ba = blk(ba_ref, b)[...]
        beta = _sigmoid(ba)
        xg = ba + hp_ref[1:2, :]
        softplus = jnp.maximum(xg, 0.0) + jnp.log1p(jnp.exp(-jnp.abs(xg)))
        g_all = -jnp.exp(hp_ref[0:1, :]) * softplus
        if l_valid < c:
            beta = jnp.where(row < l_valid, beta, 0.0)
            g_all = jnp.where(row < l_valid, g_all, 0.0)
        gcum = g_all
        s = 1
        while s < c:
            gcum = gcum + jnp.where(row >= s, pltpu.roll(gcum, s, 0), 0.0)
            s *= 2
        beta_all.append(beta)
        gcum_all.append(gcum)
    gn = gn_ref[...]

    def col(h, part=0):
        return slice(part * w + h * GDN_D, part * w + (h + 1) * GDN_D)

    chains = [(b, h) for b in range(nseq) for h in range(GDN_HEADS)]
    for c0 in range(0, len(chains), group):
        ch = chains[c0:c0 + group]
        qh = [cb_ref[b, :, col(h)] for b, h in ch]
        kh = [cb_ref[b, :, col(h, 1)] for b, h in ch]
        vh = [cb_ref[b, :, col(h, 2)] for b, h in ch]
        qn = [x * lax.rsqrt(jnp.sum(x * x, axis=-1, keepdims=True) + EPS) * (GDN_D ** -0.5) for x in qh]
        kn = [x * lax.rsqrt(jnp.sum(x * x, axis=-1, keepdims=True) + EPS) for x in kh]
        bcol = [beta_all[b][:, h:h + 1] for b, h in ch]
        gcol = [gcum_all[b][:, h + GDN_HEADS:h + GDN_HEADS + 1] for b, h in ch]
        s_prev = [s_ref[h] if prompt else sin_ref[b, h] for b, h in ch]
        o, s_new = _gdn_heads(qn, kn, vh, bcol, gcol, s_prev, lvl_ref, c)
        for j, (b, h) in enumerate(ch):
            if prompt:
                s_ref[h] = s_new[j]
            else:
                sout_ref[b, h] = s_new[j]
        zh = [blk(z_ref, b)[:, col(h)] for b, h in ch]
        on = [x * lax.rsqrt(jnp.mean(x * x, axis=-1, keepdims=True) + EPS) * gn for x in o]
        for j, (b, h) in enumerate(ch):
            blk(o_ref, b)[:, col(h)] = (on[j] * _silu(zh[j])).astype(o_ref.dtype)
    if prompt:
        @pl.when(i == n_chunks - 1)
        def _():
            sout_ref[...] = s_ref[...]


def _level_masks(c, l_valid):
    i = jnp.arange(c, dtype=jnp.int32)[:, None]
    j = jnp.arange(LANE, dtype=jnp.int32)[None, :]
    masks = []
    s = 1
    while s < l_valid:
        masks.append((i // (2 * s) == j // (2 * s)) & (i % (2 * s) >= s) & (j % (2 * s) < s))
        s *= 2
    return jnp.stack(masks).astype(F32)


def _gdn_prompt(proj, ba_col, conv_w, hp, gn, c=128):
    t = proj.shape[0]
    n_chunks = t // c
    w = GDN_W
    lvl = _level_masks(c, c)
    kern = functools.partial(_gdn_kernel, c=c, l_valid=c, prompt=True, n_chunks=n_chunks, group=GDN_GROUP_PROMPT)
    return pl.pallas_call(
        kern,
        grid=(n_chunks,),
        in_specs=[pl.BlockSpec((c, w), lambda i: (i, 0)),
                  pl.BlockSpec((c, w), lambda i: (i, 1)),
                  pl.BlockSpec((c, w), lambda i: (i, 2)),
                  pl.BlockSpec((c, w), lambda i: (i, 3)),
                  pl.BlockSpec((c, LANE), lambda i: (i, ba_col)),
                  pl.BlockSpec((CONV_W, 3 * w), lambda i: (0, 0)),
                  pl.BlockSpec((8, LANE), lambda i: (0, 0)),
                  pl.BlockSpec((1, GDN_D), lambda i: (0, 0)),
                  pl.BlockSpec(lvl.shape, lambda i: (0, 0, 0))],
        out_specs=[pl.BlockSpec((c, w), lambda i: (i, 0)),
                   pl.BlockSpec((GDN_HEADS, GDN_D, GDN_D), lambda i: (0, 0, 0)),
                   pl.BlockSpec((8, 3 * w), lambda i: (0, 0))],
        out_shape=[jax.ShapeDtypeStruct((t, w), BF16),
                   jax.ShapeDtypeStruct((GDN_HEADS, GDN_D, GDN_D), F32),
                   jax.ShapeDtypeStruct((8, 3 * w), F32)],
        scratch_shapes=[pltpu.VMEM((1, 8 + c, 3 * w), F32),
                        pltpu.VMEM((1, c, 3 * w), F32),
                        pltpu.VMEM((GDN_HEADS, GDN_D, GDN_D), F32)],
        compiler_params=_cparams(("arbitrary",)),
        name="gdn_prompt",
    )(proj, proj, proj, proj, proj, conv_w, hp, gn, lvl)


def _gdn_sample(proj3, ba_col, conv_w, hp, gn, conv_in, s_in, l):
    b, c, _ = proj3.shape
    w = GDN_W
    lvl = _level_masks(c, l)
    kern = functools.partial(_gdn_kernel, c=c, l_valid=l, prompt=False, n_chunks=1, group=GDN_GROUP_SAMPLE)
    ns = math.gcd(b, GDN_SAMPLE_SEQS)
    state_spec = pl.BlockSpec((ns, GDN_HEADS, GDN_D, GDN_D), lambda i: (i, 0, 0, 0))
    return pl.pallas_call(
        kern,
        grid=(b // ns,),
        in_specs=[pl.BlockSpec((ns, c, w), lambda i: (i, 0, 0)),
                  pl.BlockSpec((ns, c, w), lambda i: (i, 0, 1)),
                  pl.BlockSpec((ns, c, w), lambda i: (i, 0, 2)),
                  pl.BlockSpec((ns, c, w), lambda i: (i, 0, 3)),
                  pl.BlockSpec((ns, c, LANE), lambda i: (i, 0, ba_col)),
                  pl.BlockSpec((CONV_W, 3 * w), lambda i: (0, 0)),
                  pl.BlockSpec((8, LANE), lambda i: (0, 0)),
                  pl.BlockSpec((1, GDN_D), lambda i: (0, 0)),
                  pl.BlockSpec(lvl.shape, lambda i: (0, 0, 0)),
                  pl.BlockSpec((ns, 8, 3 * w), lambda i: (i, 0, 0)),
                  state_spec],
        out_specs=[pl.BlockSpec((ns, c, w), lambda i: (i, 0, 0)),
                   state_spec,
                   pl.BlockSpec((ns, 8, 3 * w), lambda i: (i, 0, 0))],
        out_shape=[jax.ShapeDtypeStruct((b, c, w), F32),
                   jax.ShapeDtypeStruct((b, GDN_HEADS, GDN_D, GDN_D), F32),
                   jax.ShapeDtypeStruct((b, 8, 3 * w), F32)],
        scratch_shapes=[pltpu.VMEM((ns, 8 + c, 3 * w), F32),
                        pltpu.VMEM((ns, c, 3 * w), F32)],
        compiler_params=_cparams(("arbitrary",)),
        name="gdn_sample",
    )(proj3, proj3, proj3, proj3, proj3, conv_w, hp, gn, lvl, conv_in, s_in)


def _t5_bucket(dist):
    d = jnp.maximum(dist, 0)
    max_exact = N_BUCKETS // 2
    large = max_exact + (jnp.log(jnp.maximum(d, 1).astype(F32) / max_exact)
                         / math.log(WINDOW / max_exact) * (N_BUCKETS - max_exact)).astype(jnp.int32)
    large = jnp.minimum(large, N_BUCKETS - 1)
    return jnp.where(d < max_exact, d, large)


def _bias_table(dist, rel_bias):
    valid = (dist >= 0) & (dist < WINDOW)
    onehot = (_t5_bucket(dist)[None] == jnp.arange(N_BUCKETS, dtype=jnp.int32)[:, None, None]).astype(F32)
    b = jnp.einsum('bh,bqs->hqs', rel_bias.astype(F32), onehot, precision=lax.Precision.HIGHEST)
    return jnp.where(valid[None], b, NEG)


def _lo_hi(slab, g):
    lane = lax.broadcasted_iota(jnp.int32, slab.shape, 1)
    if g % 2 == 0:
        lo = jnp.where(lane < SWA_HD, slab, 0.0)
        hi = pltpu.roll(lo, SWA_HD, 1)
    else:
        hi = jnp.where(lane >= SWA_HD, slab, 0.0)
        lo = pltpu.roll(hi, SWA_HD, 1)
    return lo, hi


def _sink_softmax_chains(parts, sinks):
    cs = range(len(parts))
    m = [sinks[i] for i in cs]
    for j in range(len(parts[0])):
        mx = [jnp.max(parts[i][j], axis=-1, keepdims=True) for i in cs]
        m = [jnp.maximum(m[i], mx[i]) for i in cs]
    es = [[jnp.exp(p - m[i]) for p in parts[i]] for i in cs]
    den = [jnp.exp(sinks[i] - m[i]) for i in cs]
    for j in range(len(parts[0])):
        sm = [jnp.sum(es[i][j], axis=-1, keepdims=True) for i in cs]
        den = [den[i] + sm[i] for i in cs]
    inv = [1.0 / d for d in den]
    return [[e * inv[i] for e in es[i]] for i in cs]


SWA_SAMPLE_SEQS = 1
SWA_GROUP_PROMPT = 8


def _swa_prompt_kernel(q_ref, kp_ref, kc_ref, vp_ref, vc_ref, bias_ref, sink_ref, o_ref):
    n = pl.program_id(0)
    wq = WINDOW
    col = lax.broadcasted_iota(jnp.int32, (wq, 2 * wq), 1)
    first = jnp.logical_and(n == 0, col < wq)
    sink_all = sink_ref[...]
    scale = SWA_HD ** -0.5
    for g0 in range(0, SWA_KV, SWA_GROUP_PROMPT):
        gs = range(g0, g0 + SWA_GROUP_PROMPT)
        kv = {}
        for g in gs:
            sl = slice((g // 2) * LANE, (g // 2 + 1) * LANE)
            kslab = jnp.concatenate([kp_ref[:, sl], kc_ref[:, sl]], axis=0)
            vslab = jnp.concatenate([vp_ref[:, sl], vc_ref[:, sl]], axis=0)
            kv[g] = ([x.astype(BF16) for x in _lo_hi(kslab, g)], [x.astype(BF16) for x in _lo_hi(vslab, g)])
        chains = [(g, s, half) for g in gs for s in range(2) for half in range(2)]
        qs = {(g, s): q_ref[:, (2 * g + s) * LANE:(2 * g + s + 1) * LANE].astype(BF16) for g in gs for s in range(2)}
        logits = [_dot_nt(qs[g, s], kv[g][0][half]) for g, s, half in chains]
        logits = [jnp.where(first, NEG, logits[i] * scale + bias_ref[4 * g + 2 * s + half])
                  for i, (g, s, half) in enumerate(chains)]
        sinks = [sink_all[:, 4 * g + 2 * s + half:4 * g + 2 * s + half + 1] for g, s, half in chains]
        probs = _sink_softmax_chains([[x] for x in logits], sinks)
        pv = [_dot(probs[i][0].astype(BF16), kv[g][1][half]) for i, (g, s, half) in enumerate(chains)]
        for i, (g, s, half) in enumerate(chains):
            if half == 0:
                o_ref[:, (2 * g + s) * LANE:(2 * g + s + 1) * LANE] = (pv[i] + pv[i + 1]).astype(o_ref.dtype)


def _swa_prompt(proj, qcol, bias, sinks):
    t = proj.shape[0]
    nb = t // WINDOW
    kb = qcol * (SWA_W // SWA_KVW) + SWA_W // SWA_KVW
    return pl.pallas_call(
        _swa_prompt_kernel,
        grid=(nb,),
        in_specs=[pl.BlockSpec((WINDOW, SWA_W), lambda n: (n, qcol)),
                  pl.BlockSpec((WINDOW, SWA_KVW), lambda n: (jnp.maximum(n - 1, 0), kb)),
                  pl.BlockSpec((WINDOW, SWA_KVW), lambda n: (n, kb)),
                  pl.BlockSpec((WINDOW, SWA_KVW), lambda n: (jnp.maximum(n - 1, 0), kb + 1)),
                  pl.BlockSpec((WINDOW, SWA_KVW), lambda n: (n, kb + 1)),
                  pl.BlockSpec((SWA_HEADS, WINDOW, 2 * WINDOW), lambda n: (0, 0, 0)),
                  pl.BlockSpec((1, LANE), lambda n: (0, 0))],
        out_specs=pl.BlockSpec((WINDOW, SWA_W), lambda n: (n, 0)),
        out_shape=jax.ShapeDtypeStruct((t, SWA_W), BF16),
        compiler_params=_cparams(("arbitrary",)),
        name="swa_prompt",
    )(proj, proj, proj, proj, proj, bias, sinks)


def _swa_sample_kernel(q_ref, kn_ref, vn_ref, kc_ref, vc_ref, bc_ref, bn_ref, sink_ref, o_ref, ko_ref, vo_ref, *, l):
    r = 8
    wb = WINDOW
    nseq = q_ref.shape[0]
    seqs = range(nseq)
    knew = [kn_ref[b] for b in seqs]
    vnew = [vn_ref[b] for b in seqs]
    row = lax.broadcasted_iota(jnp.int32, (LANE, wb), 0)
    col = lax.broadcasted_iota(jnp.int32, (LANE, wb), 1)
    place = jnp.where(jnp.logical_and(row < l, col == row + (wb - l)), 1.0, 0.0).astype(BF16)
    lane_w = lax.broadcasted_iota(jnp.int32, (SWA_KVW, wb), 1)
    for b in seqs:
        for c_ref, new, out_ref in ((kc_ref, knew[b], ko_ref), (vc_ref, vnew[b], vo_ref)):
            hi = new.astype(BF16)
            r1 = new - hi.astype(F32)
            mid = r1.astype(BF16)
            lo = (r1 - mid.astype(F32)).astype(BF16)
            moved = ((_dot_tn(_rowpad(hi, LANE), place) + _dot_tn(_rowpad(mid, LANE), place))
                     + _dot_tn(_rowpad(lo, LANE), place))
            out_ref[b] = jnp.where(lane_w >= wb - l, moved, pltpu.roll(c_ref[b], wb - l, 1))
    sink_all = sink_ref[...]
    scale = SWA_HD ** -0.5
    gs = range(SWA_KV)
    chains = [(b, g, half) for b in seqs for g in gs for half in range(2)]
    sls = [slice((g // 2) * LANE, (g // 2 + 1) * LANE) for g in gs]
    zero = jnp.zeros((SWA_HD, wb), BF16)

    def lo_hi_t(ref, b, g):
        t = ref[b, g * SWA_HD:(g + 1) * SWA_HD, :].astype(BF16)
        return [jnp.concatenate([t, zero], axis=0), jnp.concatenate([zero, t], axis=0)]
    kc = {(b, g): lo_hi_t(kc_ref, b, g) for b in seqs for g in gs}
    vc = {(b, g): lo_hi_t(vc_ref, b, g) for b in seqs for g in gs}
    kn = {(b, g): [_rowpad(x, LANE).astype(BF16) for x in _lo_hi(knew[b][:, sls[g]], g)] for b in seqs for g in gs}
    vn = {(b, g): [_rowpad(x, LANE).astype(BF16) for x in _lo_hi(vnew[b][:, sls[g]], g)] for b in seqs for g in gs}
    qs = {(b, g): jnp.concatenate([q_ref[b, :, (2 * g) * LANE:(2 * g + 1) * LANE],
                                   q_ref[b, :, (2 * g + 1) * LANE:(2 * g + 2) * LANE]], axis=0).astype(BF16)
          for b in seqs for g in gs}
    lc = [_dot(qs[b, g], kc[b, g][half]) for b, g, half in chains]
    ln = [_dot_nt(qs[b, g], kn[b, g][half]) for b, g, half in chains]
    lc = [lc[i] * scale + bc_ref[2 * g + half] for i, (b, g, half) in enumerate(chains)]
    ln = [ln[i] * scale + bn_ref[2 * g + half] for i, (b, g, half) in enumerate(chains)]
    sinks = [jnp.concatenate([jnp.broadcast_to(sink_all[:, 4 * g + half:4 * g + half + 1], (r, 1)),
                              jnp.broadcast_to(sink_all[:, 4 * g + 2 + half:4 * g + 2 + half + 1], (r, 1))], axis=0)
             for b, g, half in chains]
    probs = _sink_softmax_chains([[lc[i], ln[i]] for i in range(len(chains))], sinks)
    pvc = [_dot_nt(probs[i][0].astype(BF16), vc[b, g][half]) for i, (b, g, half) in enumerate(chains)]
    pvn = [_dot(probs[i][1].astype(BF16), vn[b, g][half]) for i, (b, g, half) in enumerate(chains)]
    for i, (b, g, half) in enumerate(chains):
        if half == 0:
            acc = (pvc[i] + pvn[i]) + (pvc[i + 1] + pvn[i + 1])
            o_ref[b, :, (2 * g) * LANE:(2 * g + 1) * LANE] = acc[:r]
            o_ref[b, :, (2 * g + 1) * LANE:(2 * g + 2) * LANE] = acc[r:]


def _swa_sample(proj3, qcol, k_cache, v_cache, bias_c, bias_n, sinks, l):
    b, r, _ = proj3.shape
    kb = qcol * (SWA_W // SWA_KVW) + SWA_W // SWA_KVW
    ns = math.gcd(b, SWA_SAMPLE_SEQS)
    cache_spec = pl.BlockSpec((ns, SWA_KVW, WINDOW), lambda i: (i, 0, 0))
    return pl.pallas_call(
        functools.partial(_swa_sample_kernel, l=l),
        grid=(b // ns,),
        in_specs=[pl.BlockSpec((ns, r, SWA_W), lambda i: (i, 0, qcol)),
                  pl.BlockSpec((ns, r, SWA_KVW), lambda i: (i, 0, kb)),
                  pl.BlockSpec((ns, r, SWA_KVW), lambda i: (i, 0, kb + 1)),
                  cache_spec, cache_spec,
                  pl.BlockSpec((2 * SWA_KV, 2 * r, LANE), lambda i: (0, 0, 0)),
                  pl.BlockSpec((2 * SWA_KV, 2 * r, LANE), lambda i: (0, 0, 0)),
                  pl.BlockSpec((1, LANE), lambda i: (0, 0))],
        out_specs=[pl.BlockSpec((ns, r, SWA_W), lambda i: (i, 0, 0)), cache_spec, cache_spec],
        out_shape=[jax.ShapeDtypeStruct((b, r, SWA_W), F32),
                   jax.ShapeDtypeStruct(k_cache.shape, F32),
                   jax.ShapeDtypeStruct(v_cache.shape, F32)],
        compiler_params=_cparams(("arbitrary",)),
        name="swa_sample",
    )(proj3, proj3, proj3, k_cache, v_cache, bias_c, bias_n, sinks)


MOE_WCHUNK = 1024
MOE_WSLOTS = 6
MOE_LOOKAHEAD = 5
MOE_ROWS_SMALL = 256
MOE_ROW_UNROLL = 8


def _moe_kernel(ord_ref, be_ref, bs_ref, bn_ref, h_hbm, wg_hbm, wu_hbm, wd_hbm, y_hbm,
                xbuf, x16, hgu, hid16, acc, ybuf, wbuf, gsem, ssem, wsem, *, n_blocks, t_all):
    k = pl.program_id(0)
    n = bn_ref[k]
    slot = lax.rem(k, 2)
    d = x16.shape[1]
    half = d // 2
    kc = MOE_WCHUNK // 2
    n_gu = d // kc
    n_dn = d // MOE_WCHUNK
    n_ch = n_gu + n_dn
    assert n_ch % MOE_WSLOTS == 0 and wg_hbm.shape[2] == MOE_WCHUNK

    def gather_copy(tok, r):
        return pltpu.make_async_copy(h_hbm.at[pl.ds(tok, 1)], xbuf.at[pl.ds(r, 1)], gsem)

    def scatter_copy(sl, r, a):
        return pltpu.make_async_copy(ybuf.at[sl, pl.ds(r, 1)], y_hbm.at[pl.ds(a, 1)], ssem.at[sl])

    def w_copies(e, j):
        s = j % MOE_WSLOTS
        if j < n_gu:
            rows = pl.ds(j * kc, kc)
            return [pltpu.make_async_copy(wg_hbm.at[e, rows, :], wbuf.at[s, pl.ds(0, kc), :], wsem.at[s]),
                    pltpu.make_async_copy(wu_hbm.at[e, rows, :], wbuf.at[s, pl.ds(kc, kc), :], wsem.at[s])]
        cols = pl.ds((j - n_gu) * MOE_WCHUNK, MOE_WCHUNK)
        return [pltpu.make_async_copy(wd_hbm.at[e, :, cols], wbuf.at[s], wsem.at[s])]

    def for_rows(cnt, fn):
        groups = lax.shift_right_logical(cnt, MOE_ROW_UNROLL.bit_length() - 1)

        def group(q, carry):
            for u in range(MOE_ROW_UNROLL):
                fn(q * MOE_ROW_UNROLL + u)
            return carry

        def single(r, carry):
            fn(r)
            return carry
        lax.fori_loop(0, groups, group, 0)
        lax.fori_loop(groups * MOE_ROW_UNROLL, cnt, single, 0)

    def gather_start(kk):
        st = bs_ref[kk]

        def one(r):
            a = ord_ref[st + r]
            gather_copy(jnp.where(a >= t_all, a - t_all, a), r).start()
        for_rows(bn_ref[kk], one)

    def gather_wait(kk):
        for_rows(bn_ref[kk], lambda r: gather_copy(0, 0).wait())

    def scatter_start(kk, sl):
        st = bs_ref[kk]
        for_rows(bn_ref[kk], lambda r: scatter_copy(sl, r, ord_ref[st + r]).start())

    def scatter_wait(kk, sl):
        for_rows(bn_ref[kk], lambda r: scatter_copy(sl, 0, 0).wait())

    @pl.when(n > 0)
    def _():
        e = be_ref[k]
        k1 = jnp.minimum(k + 1, n_blocks - 1)
        has_next = jnp.logical_and(k + 1 < n_blocks, bn_ref[k1] > 0)
        e_next = be_ref[k1]

        @pl.when(k == 0)
        def _():
            xbuf[...] = jnp.zeros(xbuf.shape, xbuf.dtype)
            gather_start(0)
            for j in range(MOE_LOOKAHEAD):
                for cp in w_copies(e, j):
                    cp.start()
        gather_wait(k)
        words = xbuf[...]
        lo = lax.bitcast_convert_type(lax.shift_left(words, jnp.uint32(16)), F32)
        hi = lax.bitcast_convert_type(words & jnp.uint32(0xFFFF0000), F32)
        x16[:, 0:half] = lo.astype(BF16)
        x16[:, half:d] = hi.astype(BF16)

        @pl.when(has_next)
        def _():
            gather_start(k1)

        def ffn(m):
            for j in range(n_ch):
                jn = j + MOE_LOOKAHEAD
                if jn < n_ch:
                    for cp in w_copies(e, jn):
                        cp.start()
                else:
                    @pl.when(has_next)
                    def _():
                        for cp in w_copies(e_next, jn - n_ch):
                            cp.start()
                for cp in w_copies(e, j):
                    cp.wait()
                s = j % MOE_WSLOTS
                if j < n_gu:
                    xs = x16[0:m, j * kc:(j + 1) * kc]
                    pg = _dot(xs, wbuf[s, 0:kc, :].astype(BF16))
                    pu = _dot(xs, wbuf[s, kc:2 * kc, :].astype(BF16))
                    if j == 0:
                        hgu[0, 0:m] = pg
                        hgu[1, 0:m] = pu
                    else:
                        hgu[0, 0:m] += pg
                        hgu[1, 0:m] += pu
                    if j == n_gu - 1:
                        hid16[0:m] = (_silu(hgu[0, 0:m]) * hgu[1, 0:m]).astype(BF16)
                else:
                    cols = slice((j - n_gu) * MOE_WCHUNK, (j - n_gu + 1) * MOE_WCHUNK)
                    acc[0:m, cols] = _dot(hid16[0:m], wbuf[s].astype(BF16))
            bits = lax.bitcast_convert_type(acc[0:m].astype(BF16).astype(F32), jnp.uint32)
            ybuf[slot, 0:m] = (lax.shift_right_logical(bits[:, :half], jnp.uint32(16))
                               | (bits[:, half:] & jnp.uint32(0xFFFF0000)))

        @pl.when(n <= MOE_ROWS_SMALL)
        def _():
            ffn(MOE_ROWS_SMALL)

        @pl.when(n > MOE_ROWS_SMALL)
        def _():
            ffn(MOE_ROWS)
        scatter_start(k, slot)

        @pl.when(k > 0)
        def _():
            scatter_wait(k - 1, 1 - slot)

        @pl.when(jnp.logical_not(has_next))
        def _():
            scatter_wait(k, slot)


def _moe_ffn(order, blk_e, blk_start, blk_n, h_packed, w_gate, w_up, w_down):
    a = order.shape[0]
    t_all = h_packed.shape[0]
    d = 2 * h_packed.shape[1]
    n_blocks = blk_e.shape[0]
    grid_spec = pltpu.PrefetchScalarGridSpec(
        num_scalar_prefetch=4,
        grid=(n_blocks,),
        in_specs=[pl.BlockSpec(memory_space=pl.ANY)] * 4,
        out_specs=pl.BlockSpec(memory_space=pl.ANY),
        scratch_shapes=[pltpu.VMEM((MOE_ROWS, d // 2), jnp.uint32),
                        pltpu.VMEM((MOE_ROWS, d), BF16),
                        pltpu.VMEM((2, MOE_ROWS, D_EXPERT), F32),
                        pltpu.VMEM((MOE_ROWS, D_EXPERT), BF16),
                        pltpu.VMEM((MOE_ROWS, d), F32),
                        pltpu.VMEM((2, MOE_ROWS, d // 2), jnp.uint32),
                        pltpu.VMEM((MOE_WSLOTS, MOE_WCHUNK, MOE_WCHUNK), F32),
                        pltpu.SemaphoreType.DMA(()),
                        pltpu.SemaphoreType.DMA((2,)),
                        pltpu.SemaphoreType.DMA((MOE_WSLOTS,))],
    )
    return pl.pallas_call(
        functools.partial(_moe_kernel, n_blocks=n_blocks, t_all=t_all),
        grid_spec=grid_spec,
        out_shape=jax.ShapeDtypeStruct((a, d // 2), jnp.uint32),
        compiler_params=_cparams(("arbitrary",)),
        name="moe_ffn",
    )(order, blk_e, blk_start, blk_n, h_packed, w_gate, w_up, w_down)


def _unpack_pairs(words):
    lo = lax.bitcast_convert_type(lax.shift_left(words, jnp.uint32(16)), F32)
    hi = lax.bitcast_convert_type(words & jnp.uint32(0xFFFF0000), F32)
    return jnp.concatenate([lo, hi], axis=1)


def _final_kernel(x_ref, y0_ref, y1_ref, p_ref, g_ref, nf_ref, o_ref):
    p = p_ref[...]
    moe = _unpack_pairs(y0_ref[...]) * p[:, 0:1] + _unpack_pairs(y1_ref[...]) * p[:, 1:2]
    x = x_ref[...] + g_ref[...] * moe
    o_ref[...] = x * lax.rsqrt(jnp.mean(x * x, axis=-1, keepdims=True) + EPS) * nf_ref[...]


def _final(x, y, row0, gates, gt2, nf, tm=256):
    t, d = x.shape
    t_all = gates.shape[0]
    tm = math.gcd(math.gcd(tm, t), math.gcd(row0, t_all))
    b0 = row0 // tm
    b1 = t_all // tm + b0
    return pl.pallas_call(
        _final_kernel,
        grid=(t // tm,),
        in_specs=[pl.BlockSpec((tm, d), lambda i: (i, 0)),
                  pl.BlockSpec((tm, d // 2), lambda i: (b0 + i, 0)),
                  pl.BlockSpec((tm, d // 2), lambda i: (b1 + i, 0)),
                  pl.BlockSpec((tm, LANE), lambda i: (b0 + i, 0)),
                  _mod_spec(gt2, tm, d),
                  pl.BlockSpec((1, d), lambda i: (0, 0))],
        out_specs=pl.BlockSpec((tm, d), lambda i: (i, 0)),
        out_shape=jax.ShapeDtypeStruct((t, d), F32),
        compiler_params=_cparams(("arbitrary",)),
        name="final",
    )(x, y, y, gates, gt2.arr, nf.reshape(1, d))


def _dispatch(expert_idx):
    t, k = expert_idx.shape
    a = t * k
    n_blocks = a // MOE_ROWS + N_EXPERTS
    flat_e = expert_idx.T.reshape(a)
    order = jnp.argsort(flat_e, stable=True).astype(jnp.int32)
    counts = jnp.zeros((N_EXPERTS,), jnp.int32).at[flat_e].add(1)
    seg_start = jnp.cumsum(counts) - counts
    nblk = (counts + MOE_ROWS - 1) // MOE_ROWS
    blk_end = jnp.cumsum(nblk)
    n_active = blk_end[-1]
    kk = jnp.minimum(jnp.arange(n_blocks, dtype=jnp.int32), n_active - 1)
    blk_e = jnp.minimum(jnp.searchsorted(blk_end, kk, side='right'), N_EXPERTS - 1).astype(jnp.int32)
    j = kk - (blk_end[blk_e] - nblk[blk_e])
    blk_start = (seg_start[blk_e] + j * MOE_ROWS).astype(jnp.int32)
    blk_n = jnp.clip(counts[blk_e] - j * MOE_ROWS, 0, MOE_ROWS)
    blk_n = jnp.where(jnp.arange(n_blocks) < n_active, blk_n, 0).astype(jnp.int32)
    return order, blk_e, blk_start, blk_n


def kernel(x_prompt, x_sample, state_gdn, state_conv, cache_swa_k, cache_swa_v, c_prompt, c_sample, w_ada, b_ada, norm_mix, w_in, conv_w, a_log, dt_bias, gdn_norm, swa_sinks, rel_bias, w_out, norm_moe, router_group, router_group_bias, router_expert, router_expert_bias, w_gate, w_up, w_down, norm_final):
    depth = w_ada.shape[0]
    assert depth == 1
    bp, seq, d = x_prompt.shape
    assert bp == 1 and seq % WINDOW == 0
    bs, ls, _ = x_sample.shape
    tp = bp * seq
    ts = bs * ls
    n_main = 2 * GDN_W + 2 * GDN_W
    n_ba = 2 * GDN_HEADS

    c_all = jnp.concatenate([c_prompt, c_sample], axis=0)
    m_pad = -(-c_all.shape[0] // 16) * 16
    c_all = jnp.pad(c_all, ((0, m_pad - c_all.shape[0]), (0, 0)))
    mod = _adaln(c_all, w_ada[0], b_ada[0])
    mod_rows_s = jnp.repeat(mod[1:1 + bs], ls, axis=0)
    mod_p = [Mod(mod[0:1], j) for j in range(6)]
    mod_s = [Mod(mod_rows_s, j) for j in range(6)]

    n_gdn = n_main + 512
    w_in_t = w_in[0].T
    w_swa_t = w_in_t[n_main + n_ba:]
    w_out16 = w_out[0].astype(BF16)
    ba_col = n_main // LANE
    kcol = SWA_W
    hp = jnp.zeros((8, LANE), F32)
    hp = hp.at[0, GDN_HEADS:2 * GDN_HEADS].set(a_log[0]).at[1, GDN_HEADS:2 * GDN_HEADS].set(dt_bias[0])
    gn = gdn_norm[0].reshape(1, GDN_D)
    sinks = jnp.pad(swa_sinks[0], (0, LANE - SWA_HEADS)).reshape(1, LANE)

    xp = x_prompt.reshape(tp, d)
    h_p = _norm_mod(xp, norm_mix[0], mod_p[1], mod_p[0])
    proj_p = _matmul_f32wt(h_p, w_in_t, n_gdn, 1024, 512, name="inproj_p")
    pswa_p = _matmul_f32wt(h_p, w_swa_t, w_swa_t.shape[0], 1024, 512, name="inproj_swa_p")
    og_p, s_p, conv_p = _gdn_prompt(proj_p, ba_col, conv_w[0], hp, gn)
    conv_p = conv_p[8 - (CONV_W - 1):]
    qi = jnp.arange(WINDOW, dtype=jnp.int32)
    sj = jnp.arange(2 * WINDOW, dtype=jnp.int32)
    bias_p = _bias_table((WINDOW + qi)[:, None] - sj[None, :], rel_bias)
    os_p = _swa_prompt(pswa_p, 0, bias_p, sinks)
    k_p = pswa_p[tp - WINDOW:, kcol:kcol + SWA_KVW]
    v_p = pswa_p[tp - WINDOW:, kcol + SWA_KVW:kcol + 2 * SWA_KVW]
    x1_p = _outproj(og_p, os_p, w_out16, xp, mod_p[2])

    xs = x_sample.reshape(ts, d)
    h_s = _norm_mod(xs, norm_mix[0], mod_s[1], mod_s[0])
    proj_s = _matmul_f32wt(h_s, w_in_t, n_gdn, 512, 512, name="inproj_s")
    pswa_s = _matmul_f32wt(h_s, w_swa_t, w_swa_t.shape[0], 512, 512, name="inproj_swa_s")
    rs = 8
    assert CONV_W - 1 <= ls <= rs
    proj_s8 = jnp.pad(proj_s.reshape(bs, ls, n_gdn), ((0, 0), (0, rs - ls), (0, 0)))
    pswa_s8 = jnp.pad(pswa_s.reshape(bs, ls, pswa_s.shape[1]), ((0, 0), (0, rs - ls), (0, 0)))
    conv_in8 = jnp.pad(state_conv[0], ((0, 0), (8 - (CONV_W - 1), 0), (0, 0)))
    og_s, s_s, conv_s = _gdn_sample(proj_s8, ba_col, conv_w[0], hp, gn, conv_in8, state_gdn[0], ls)
    conv_s = conv_s[:, 8 - (CONV_W - 1):]
    wb = cache_swa_k.shape[2]
    assert wb == WINDOW == LANE
    dist_s = (wb + jnp.arange(ls, dtype=jnp.int32))[:, None] - jnp.arange(wb + ls, dtype=jnp.int32)[None, :]
    bias_s = _bias_table(dist_s, rel_bias)
    bias_s = jnp.pad(bias_s, ((0, 0), (0, rs - ls), (0, 2 * LANE - wb - ls)), constant_values=NEG)
    bias_s = bias_s.reshape(SWA_KV, 2, 2, rs, 2 * LANE)
    bias_s = jnp.transpose(bias_s, (0, 2, 1, 3, 4)).reshape(2 * SWA_KV, 2 * rs, 2 * LANE)
    k_cache = jnp.transpose(cache_swa_k[0], (0, 2, 3, 1)).reshape(bs, SWA_KVW, wb)
    v_cache = jnp.transpose(cache_swa_v[0], (0, 2, 3, 1)).reshape(bs, SWA_KVW, wb)
    os_s, k_s, v_s = _swa_sample(pswa_s8, 0, k_cache, v_cache, bias_s[:, :, :LANE], bias_s[:, :, LANE:], sinks, ls)
    x1_s = _outproj(og_s[:, :ls].reshape(ts, GDN_W).astype(BF16), os_s[:, :ls].reshape(ts, SWA_W).astype(BF16),
                    w_out16, xs, mod_s[2], tm=512)

    wr = jnp.concatenate([router_group[0], router_expert[0]], axis=1)
    wr2 = jnp.pad(wr, ((0, 0), (0, LANE - wr.shape[1]))).astype(BF16)
    br = jnp.pad(jnp.concatenate([router_group_bias[0], router_expert_bias[0]]),
                 (0, LANE - N_GROUPS - N_EXPERTS)).reshape(1, LANE)
    t_all = tp + ts
    h2, eidx, gates = _norm_router(x1_p, x1_s, norm_moe[0], mod_p[4], mod_p[3], mod_s[4], mod_s[3], wr2, br)
    order, blk_e, blk_start, blk_n = _dispatch(eidx[:, :2])
    y = _moe_ffn(order, blk_e, blk_start, blk_n, h2, w_gate[0], w_up[0], w_down[0])
    y_p = _final(x1_p, y, 0, gates, mod_p[5], norm_final)
    y_s = _final(x1_s, y, tp, gates, mod_s[5], norm_final)

    sdt = state_gdn.dtype
    return (y_p.reshape(bp, seq, d), y_s.reshape(bs, ls, d),
            s_p.reshape(1, bp, GDN_HEADS, GDN_D, GDN_D).astype(sdt), s_s[None].astype(sdt),
            conv_p.reshape(1, bp, CONV_W - 1, 3 * GDN_W).astype(state_conv.dtype), conv_s[None].astype(state_conv.dtype),
            k_p.reshape(1, bp, WINDOW, SWA_KV, SWA_HD).astype(cache_swa_k.dtype),
            jnp.transpose(k_s.reshape(bs, SWA_KV, SWA_HD, wb), (0, 3, 1, 2))[None].astype(cache_swa_k.dtype),
            v_p.reshape(1, bp, WINDOW, SWA_KV, SWA_HD).astype(cache_swa_v.dtype),
            jnp.transpose(v_s.reshape(bs, SWA_KV, SWA_HD, wb), (0, 3, 1, 2))[None].astype(cache_swa_v.dtype))
```

```python
import functools
import math
from typing import NamedTuple

import jax
import jax.numpy as jnp
from jax import lax
from jax.experimental import pallas as pl
from jax.experimental.pallas import tpu as pltpu

F32 = jnp.float32
BF16 = jnp.bfloat16
EPS = 1e-6
NEG = -1e30

LANE = 128
GDN_HEADS = 16
GDN_D = 128
GDN_W = GDN_HEADS * GDN_D
CONV_W = 4
SWA_HEADS = 32
SWA_KV = 8
SWA_HD = 64
SWA_W = SWA_HEADS * SWA_HD
SWA_KVW = SWA_KV * SWA_HD
WINDOW = 128
N_BUCKETS = 32
N_GROUPS = 8
EPG = 8
N_EXPERTS = 64
D_EXPERT = 1024
MOE_ROWS = 448
VMEM_LIMIT = 56 * 1024 * 1024


def _cparams(sem):
    return pltpu.CompilerParams(dimension_semantics=sem, vmem_limit_bytes=VMEM_LIMIT)


def _sigmoid(x):
    return 1.0 / (1.0 + jnp.exp(-x))


def _silu(x):
    return x * _sigmoid(x)


def _dot(a, b):
    return jnp.dot(a, b, preferred_element_type=F32)


def _dot_nt(a, b):
    return lax.dot_general(a, b, (((1,), (1,)), ((), ())), preferred_element_type=F32)


def _dot_tn(a, b):
    return lax.dot_general(a, b, (((0,), (0,)), ((), ())), preferred_element_type=F32)


def _ada_kernel(c_ref, w_ref, b_ref, o_ref):
    a = _silu(c_ref[...]).astype(BF16)
    o_ref[...] = _dot(a, w_ref[...].astype(BF16)) + b_ref[...]


def _adaln(c, w, b, tn=512):
    m, d = c.shape
    n = w.shape[1]
    return pl.pallas_call(
        _ada_kernel,
        grid=(n // tn,),
        in_specs=[pl.BlockSpec((m, d), lambda j: (0, 0)),
                  pl.BlockSpec((d, tn), lambda j: (0, j)),
                  pl.BlockSpec((1, tn), lambda j: (0, j))],
        out_specs=pl.BlockSpec((m, tn), lambda j: (0, j)),
        out_shape=jax.ShapeDtypeStruct((m, n), F32),
        compiler_params=_cparams(("arbitrary",)),
        name="adaln",
    )(c, w, b.reshape(1, n))


def _norm_kernel(x_ref, g_ref, sc_ref, sh_ref, o_ref):
    x = x_ref[...]
    y = x * lax.rsqrt(jnp.mean(x * x, axis=-1, keepdims=True) + EPS) * g_ref[...]
    o_ref[...] = (y * (1.0 + sc_ref[...]) + sh_ref[...]).astype(o_ref.dtype)


class Mod(NamedTuple):
    arr: jax.Array
    col: int


def _mod_spec(mod, tm, d):
    if mod.arr.shape[0] == 1:
        return pl.BlockSpec((1, d), lambda i: (0, mod.col))
    return pl.BlockSpec((tm, d), lambda i: (i, mod.col))


def _norm_mod(x, g, sc, sh, tm=256):
    t, d = x.shape
    tm = min(tm, t)
    return pl.pallas_call(
        _norm_kernel,
        grid=(t // tm,),
        in_specs=[pl.BlockSpec((tm, d), lambda i: (i, 0)),
                  pl.BlockSpec((1, d), lambda i: (0, 0)),
                  _mod_spec(sc, tm, d), _mod_spec(sh, tm, d)],
        out_specs=pl.BlockSpec((tm, d), lambda i: (i, 0)),
        out_shape=jax.ShapeDtypeStruct((t, d), BF16),
        compiler_params=_cparams(("arbitrary",)),
        name="norm_mod",
    )(x, g.reshape(1, d), sc.arr, sh.arr)


def _norm_router_kernel(*refs, nba):
    (xa_ref, xb_ref, g_ref, sca_ref, sha_ref, scb_ref, shb_ref, wr_ref, br_ref,
     o_ref, e_ref, p_ref) = refs
    first = pl.program_id(0) < nba
    x = jnp.where(first, xa_ref[...], xb_ref[...])
    sc = jnp.where(first, sca_ref[...], scb_ref[...])
    sh = jnp.where(first, sha_ref[...], shb_ref[...])
    y = x * lax.rsqrt(jnp.mean(x * x, axis=-1, keepdims=True) + EPS) * g_ref[...]
    t = y * (1.0 + sc) + sh
    t16 = t.astype(BF16)
    half = t.shape[1] // 2
    bits = lax.bitcast_convert_type(t16.astype(F32), jnp.uint32)
    o_ref[...] = lax.shift_right_logical(bits[:, :half], jnp.uint32(16)) | (bits[:, half:] & jnp.uint32(0xFFFF0000))
    lg = _dot(t16, wr_ref[...]) + br_ref[...]
    lane = lax.broadcasted_iota(jnp.int32, lg.shape, 1)
    big = jnp.int32(4 * LANE)
    lgrp = jnp.where(lane < N_GROUPS, lg, NEG)
    mg = jnp.max(lgrp, axis=-1, keepdims=True)
    grp = jnp.min(jnp.where(lgrp == mg, lane, big), axis=-1, keepdims=True)
    p_grp = 1.0 / jnp.sum(jnp.exp(lgrp - mg), axis=-1, keepdims=True)
    lo = N_GROUPS + grp * EPG
    emask = jnp.logical_and(lane >= lo, lane < lo + EPG)
    le = jnp.where(emask, lg, NEG)
    m1 = jnp.max(le, axis=-1, keepdims=True)
    i1 = jnp.min(jnp.where(le == m1, lane, big), axis=-1, keepdims=True)
    le2 = jnp.where(lane == i1, NEG, le)
    m2 = jnp.max(le2, axis=-1, keepdims=True)
    i2 = jnp.min(jnp.where(le2 == m2, lane, big), axis=-1, keepdims=True)
    e2 = jnp.exp(m2 - m1)
    w1 = p_grp / (1.0 + e2)
    w2 = p_grp * e2 / (1.0 + e2)
    e_ref[...] = jnp.where(lane == 0, i1 - N_GROUPS, jnp.where(lane == 1, i2 - N_GROUPS, 0))
    p_ref[...] = jnp.where(lane == 0, w1, jnp.where(lane == 1, w2, 0.0))


def _norm_router(xa, xb, g, sca, sha, scb, shb, wr, br, tm=256):
    ta, d = xa.shape
    tb = xb.shape[0]
    tm = min(tm, math.gcd(ta, tb))
    nba = ta // tm
    nbb = tb // tm
    t_all = ta + tb
    spec_a = pl.BlockSpec((tm, d), lambda i: (jnp.minimum(i, nba - 1), 0))
    spec_b = pl.BlockSpec((tm, d), lambda i: (jnp.maximum(i - nba, 0), 0))
    row = pl.BlockSpec((1, d), lambda i: (0, 0))

    def mod_a(m):
        return pl.BlockSpec((1, d), lambda i: (0, m.col))

    def mod_b(m):
        return pl.BlockSpec((tm, d), lambda i: (jnp.maximum(i - nba, 0), m.col))
    return pl.pallas_call(
        functools.partial(_norm_router_kernel, nba=nba),
        grid=(nba + nbb,),
        in_specs=[spec_a, spec_b, row, mod_a(sca), mod_a(sha), mod_b(scb), mod_b(shb),
                  pl.BlockSpec((d, LANE), lambda i: (0, 0)),
                  pl.BlockSpec((1, LANE), lambda i: (0, 0))],
        out_specs=[pl.BlockSpec((tm, d // 2), lambda i: (i, 0)),
                   pl.BlockSpec((tm, LANE), lambda i: (i, 0)),
                   pl.BlockSpec((tm, LANE), lambda i: (i, 0))],
        out_shape=[jax.ShapeDtypeStruct((t_all, d // 2), jnp.uint32),
                   jax.ShapeDtypeStruct((t_all, LANE), jnp.int32),
                   jax.ShapeDtypeStruct((t_all, LANE), F32)],
        compiler_params=_cparams(("arbitrary",)),
        name="norm_router",
    )(xa, xb, g.reshape(1, d), sca.arr, sha.arr, scb.arr, shb.arr, wr, br)


def _mm_kernel(x_ref, w_ref, o_ref):
    o_ref[...] = _dot(x_ref[...], w_ref[...]).astype(o_ref.dtype)


def _matmul(x, w, tm, tn, out_dtype=F32, name="matmul"):
    m, k = x.shape
    n = w.shape[1]
    tm = min(tm, m)
    tn = min(tn, n)
    return pl.pallas_call(
        _mm_kernel,
        grid=(m // tm, n // tn),
        in_specs=[pl.BlockSpec((tm, k), lambda i, j: (i, 0)),
                  pl.BlockSpec((k, tn), lambda i, j: (0, j))],
        out_specs=pl.BlockSpec((tm, tn), lambda i, j: (i, j)),
        out_shape=jax.ShapeDtypeStruct((m, n), out_dtype),
        compiler_params=_cparams(("arbitrary", "arbitrary")),
        name=name,
    )(x, w)


def _mm_f32wt_kernel(x_ref, wt_ref, o_ref):
    o_ref[...] = _dot_nt(x_ref[...], wt_ref[...].astype(BF16))


def _matmul_f32wt(x, wt, n, tm, tn, name):
    m, k = x.shape
    tm = min(tm, m)
    return pl.pallas_call(
        _mm_f32wt_kernel,
        grid=(m // tm, n // tn),
        in_specs=[pl.BlockSpec((tm, k), lambda i, j: (i, 0)),
                  pl.BlockSpec((tn, k), lambda i, j: (j, 0))],
        out_specs=pl.BlockSpec((tm, tn), lambda i, j: (i, j)),
        out_shape=jax.ShapeDtypeStruct((m, n), F32),
        compiler_params=_cparams(("arbitrary", "arbitrary")),
        name=name,
    )(x, wt)


def _outproj_kernel(a_ref, b_ref, wa_ref, wb_ref, x_ref, g_ref, o_ref):
    mix = _dot(a_ref[...], wa_ref[...].astype(BF16)) + _dot(b_ref[...], wb_ref[...].astype(BF16))
    o_ref[...] = x_ref[...] + g_ref[...] * mix


def _outproj(a, b, w, x, gate, tm=1024, tn=512):
    m, ka = a.shape
    kb = b.shape[1]
    n = w.shape[1]
    tm = min(tm, m)
    g0 = gate.col * (n // tn)
    if gate.arr.shape[0] == 1:
        gspec = pl.BlockSpec((1, tn), lambda i, j: (0, g0 + j))
    else:
        gspec = pl.BlockSpec((tm, tn), lambda i, j: (i, g0 + j))
    return pl.pallas_call(
        _outproj_kernel,
        grid=(m // tm, n // tn),
        in_specs=[pl.BlockSpec((tm, ka), lambda i, j: (i, 0)),
                  pl.BlockSpec((tm, kb), lambda i, j: (i, 0)),
                  pl.BlockSpec((ka, tn), lambda i, j: (0, j)),
                  pl.BlockSpec((kb, tn), lambda i, j: (1, j)),
                  pl.BlockSpec((tm, tn), lambda i, j: (i, j)),
                  gspec],
        out_specs=pl.BlockSpec((tm, tn), lambda i, j: (i, j)),
        out_shape=jax.ShapeDtypeStruct((m, n), F32),
        compiler_params=_cparams(("arbitrary", "arbitrary")),
        name="outproj",
    )(a, b, w, w, x, gate.arr)


GDN_GROUP_PROMPT = 8
GDN_SAMPLE_SEQS = 8
GDN_GROUP_SAMPLE = 32


def _rowpad(a, rows):
    if a.shape[0] == rows:
        return a
    return jnp.concatenate([a, jnp.zeros((rows - a.shape[0], a.shape[1]), a.dtype)], axis=0)


def _gdn_heads(qn, kn, vh, bcol, gcol, grow, s_prev, lvl_ref, c):
    cp = LANE
    n = len(qn)
    hs = range(n)
    lane = lax.broadcasted_iota(jnp.int32, (c, cp), 1)
    row = lax.broadcasted_iota(jnp.int32, (c, cp), 0)
    tri = lane <= row

    def b16(x):
        return x.astype(BF16)

    kn16 = [b16(k) for k in kn]
    qkn16 = [jnp.concatenate([b16(qn[h]), kn16[h]], axis=0) for h in hs]
    knp16 = [_rowpad(k, cp) for k in kn16]
    qk_kk = [_dot_nt(qkn16[h], knp16[h]) for h in hs]
    decay = [jnp.where(tri, jnp.exp(jnp.minimum(gcol[h] - grow[h], 0.0)), 0.0) for h in hs]
    eg = [jnp.exp(g) for g in gcol]
    lmat = [(bcol[h] * qk_kk[h][c:]) * decay[h] for h in hs]
    qkd16 = [b16(qk_kk[h][:c] * decay[h]) for h in hs]
    n_mat = [-(lm * lvl_ref[0]) for lm in lmat]
    for lv in range(1, lvl_ref.shape[0]):
        bm = [lm * lvl_ref[lv] for lm in lmat]
        n16 = [b16(x) for x in n_mat]
        w_mat = [bm[h] + _dot(n16[h], _rowpad(b16(bm[h]), cp)) for h in hs]
        n_mat = [n_mat[h] - w_mat[h] - _dot(b16(w_mat[h]), _rowpad(n16[h], cp)) for h in hs]
    rhs = [jnp.concatenate([vh[h] * bcol[h], kn[h] * (bcol[h] * eg[h])], axis=1) for h in hs]
    sol = [rhs[h] + _dot(b16(n_mat[h]), _rowpad(b16(rhs[h]), cp)) for h in hs]
    kq16 = [jnp.concatenate([b16(sol[h][:, GDN_D:]), b16(qn[h] * eg[h])], axis=0) for h in hs]
    kq_s = [_dot(kq16[h], b16(s_prev[h])) for h in hs]
    u = [sol[h][:, :GDN_D] - kq_s[h][:c] for h in hs]
    u16 = [_rowpad(b16(x), cp) for x in u]
    g_last = [g[c - 1:c, :] for g in gcol]
    kd16 = [_rowpad(b16(kn[h] * jnp.exp(g_last[h] - gcol[h])), cp) for h in hs]
    o = [kq_s[h][c:] + _dot(qkd16[h], u16[h]) for h in hs]
    s_new = [s_prev[h] * jnp.exp(g_last[h]) + _dot_tn(kd16[h], u16[h]) for h in hs]
    return o, s_new


def _gdn_kernel(*refs, c, l_valid, prompt, n_chunks, group):
    if prompt:
        (q_ref, k_ref, v_ref, z_ref, ba_ref, cw_ref, hp_ref, gn_ref, lvl_ref,
         o_ref, sout_ref, cout_ref, xe_ref, cb_ref, s_ref) = refs
        nseq = 1
    else:
        (q_ref, k_ref, v_ref, z_ref, ba_ref, cw_ref, hp_ref, gn_ref, lvl_ref, cin_ref, sin_ref,
         o_ref, sout_ref, cout_ref, xe_ref, cb_ref) = refs
        nseq = q_ref.shape[0]
    i = pl.program_id(0)
    w = GDN_W

    def blk(ref, b):
        return ref if prompt else ref.at[b]

    if prompt:
        @pl.when(i == 0)
        def _():
            xe_ref[0, 0:8, :] = jnp.zeros((8, 3 * w), F32)
            s_ref[...] = jnp.zeros(s_ref.shape, F32)
    rowc = lax.broadcasted_iota(jnp.int32, (c, 512), 0)
    row = lax.broadcasted_iota(jnp.int32, (c, LANE), 0)
    beta_all = []
    gcum_all = []
    gcum_t_all = []
    for b in range(nseq):
        if not prompt:
            xe_ref[b, 0:8, :] = cin_ref[b]
        xe_ref[b, 8:8 + c, 0:w] = blk(q_ref, b)[...]
        xe_ref[b, 8:8 + c, w:2 * w] = blk(k_ref, b)[...]
        xe_ref[b, 8:8 + c, 2 * w:3 * w] = blk(v_ref, b)[...]
        for cb in range(3 * w // 512):
            cs = slice(cb * 512, (cb + 1) * 512)
            acc = xe_ref[b, 8:8 + c, cs] * cw_ref[CONV_W - 1:CONV_W, cs]
            for j in range(CONV_W - 1):
                acc = acc + xe_ref[b, 8 - (CONV_W - 1) + j:8 - (CONV_W - 1) + j + c, cs] * cw_ref[j:j + 1, cs]
            y = _silu(acc)
            if l_valid < c:
                y = jnp.where(rowc < l_valid, y, 0.0)
            cb_ref[b, :, cs] = y
        new_conv = xe_ref[b, l_valid:l_valid + 8, :]
        blk(cout_ref, b)[...] = new_conv
        if prompt:
            xe_ref[b, 0:8, :] = new_conv

        ba = blk(ba_ref, b)[...]
        beta = _sigmoid(ba)
        xg = ba + hp_ref[1:2, :]
        softplus = jnp.maximum(xg, 0.0) + jnp.log1p(jnp.exp(-jnp.abs(xg)))
        g_all = -jnp.exp(hp_ref[0:1, :]) * softplus
        if l_valid < c:
            beta = jnp.where(row < l_valid, beta, 0.0)
            g_all = jnp.where(row < l_valid, g_all, 0.0)
        gcum = g_all
        s = 1
        while s < c:
            gcum = gcum + jnp.where(row >= s, pltpu.roll(gcum, s, 0), 0.0)
            s *= 2
        beta_all.append(beta)
        gcum_all.append(gcum)
        gcum_t_all.append(jnp.transpose(_rowpad(gcum, LANE)))
    gn = gn_ref[...]

    def col(h, part=0):
        return slice(part * w + h * GDN_D, part * w + (h + 1) * GDN_D)

    chains = [(b, h) for b in range(nseq) for h in range(GDN_HEADS)]
    for c0 in range(0, len(chains), group):
        ch = chains[c0:c0 + group]
        qh = [cb_ref[b, :, col(h)] for b, h in ch]
        kh = [cb_ref[b, :, col(h, 1)] for b, h in ch]
        vh = [cb_ref[b, :, col(h, 2)] for b, h in ch]
        qn = [x * lax.rsqrt(jnp.sum(x * x, axis=-1, keepdims=True) + EPS) * (GDN_D ** -0.5) for x in qh]
        kn = [x * lax.rsqrt(jnp.sum(x * x, axis=-1, keepdims=True) + EPS) for x in kh]
        bcol = [beta_all[b][:, h:h + 1] for b, h in ch]
        gcol = [gcum_all[b][:, h + GDN_HEADS:h + GDN_HEADS + 1] for b, h in ch]
        grow = [gcum_t_all[b][h + GDN_HEADS:h + GDN_HEADS + 1, :] for b, h in ch]
        s_prev = [s_ref[h] if prompt else sin_ref[b, h] for b, h in ch]
        o, s_new = _gdn_heads(qn, kn, vh, bcol, gcol, grow, s_prev, lvl_ref, c)
        for j, (b, h) in enumerate(ch):
            if prompt:
                s_ref[h] = s_new[j]
            else:
                sout_ref[b, h] = s_new[j]
        zh = [blk(z_ref, b)[:, col(h)] for b, h in ch]
        on = [x * lax.rsqrt(jnp.mean(x * x, axis=-1, keepdims=True) + EPS) * gn for x in o]
        for j, (b, h) in enumerate(ch):
            blk(o_ref, b)[:, col(h)] = (on[j] * _silu(zh[j])).astype(o_ref.dtype)
    if prompt:
        @pl.when(i == n_chunks - 1)
        def _():
            sout_ref[...] = s_ref[...]


def _level_masks(c, l_valid):
    i = jnp.arange(c, dtype=jnp.int32)[:, None]
    j = jnp.arange(LANE, dtype=jnp.int32)[None, :]
    masks = []
    s = 1
    while s < l_valid:
        masks.append((i // (2 * s) == j // (2 * s)) & (i % (2 * s) >= s) & (j % (2 * s) < s))
        s *= 2
    return jnp.stack(masks).astype(F32)


def _gdn_prompt(proj, ba_col, conv_w, hp, gn, c=128):
    t = proj.shape[0]
    n_chunks = t // c
    w = GDN_W
    lvl = _level_masks(c, c)
    kern = functools.partial(_gdn_kernel, c=c, l_valid=c, prompt=True, n_chunks=n_chunks, group=GDN_GROUP_PROMPT)
    return pl.pallas_call(
        kern,
        grid=(n_chunks,),
        in_specs=[pl.BlockSpec((c, w), lambda i: (i, 0)),
                  pl.BlockSpec((c, w), lambda i: (i, 1)),
                  pl.BlockSpec((c, w), lambda i: (i, 2)),
                  pl.BlockSpec((c, w), lambda i: (i, 3)),
                  pl.BlockSpec((c, LANE), lambda i: (i, ba_col)),
                  pl.BlockSpec((CONV_W, 3 * w), lambda i: (0, 0)),
                  pl.BlockSpec((8, LANE), lambda i: (0, 0)),
                  pl.BlockSpec((1, GDN_D), lambda i: (0, 0)),
                  pl.BlockSpec(lvl.shape, lambda i: (0, 0, 0))],
        out_specs=[pl.BlockSpec((c, w), lambda i: (i, 0)),
                   pl.BlockSpec((GDN_HEADS, GDN_D, GDN_D), lambda i: (0, 0, 0)),
                   pl.BlockSpec((8, 3 * w), lambda i: (0, 0))],
        out_shape=[jax.ShapeDtypeStruct((t, w), BF16),
                   jax.ShapeDtypeStruct((GDN_HEADS, GDN_D, GDN_D), F32),
                   jax.ShapeDtypeStruct((8, 3 * w), F32)],
        scratch_shapes=[pltpu.VMEM((1, 8 + c, 3 * w), F32),
                        pltpu.VMEM((1, c, 3 * w), F32),
                        pltpu.VMEM((GDN_HEADS, GDN_D, GDN_D), F32)],
        compiler_params=_cparams(("arbitrary",)),
        name="gdn_prompt",
    )(proj, proj, proj, proj, proj, conv_w, hp, gn, lvl)


def _gdn_sample(proj3, ba_col, conv_w, hp, gn, conv_in, s_in, l):
    b, c, _ = proj3.shape
    w = GDN_W
    lvl = _level_masks(c, l)
    kern = functools.partial(_gdn_kernel, c=c, l_valid=l, prompt=False, n_chunks=1, group=GDN_GROUP_SAMPLE)
    ns = math.gcd(b, GDN_SAMPLE_SEQS)
    state_spec = pl.BlockSpec((ns, GDN_HEADS, GDN_D, GDN_D), lambda i: (i, 0, 0, 0))
    return pl.pallas_call(
        kern,
        grid=(b // ns,),
        in_specs=[pl.BlockSpec((ns, c, w), lambda i: (i, 0, 0)),
                  pl.BlockSpec((ns, c, w), lambda i: (i, 0, 1)),
                  pl.BlockSpec((ns, c, w), lambda i: (i, 0, 2)),
                  pl.BlockSpec((ns, c, w), lambda i: (i, 0, 3)),
                  pl.BlockSpec((ns, c, LANE), lambda i: (i, 0, ba_col)),
                  pl.BlockSpec((CONV_W, 3 * w), lambda i: (0, 0)),
                  pl.BlockSpec((8, LANE), lambda i: (0, 0)),
                  pl.BlockSpec((1, GDN_D), lambda i: (0, 0)),
                  pl.BlockSpec(lvl.shape, lambda i: (0, 0, 0)),
                  pl.BlockSpec((ns, 8, 3 * w), lambda i: (i, 0, 0)),
                  state_spec],
        out_specs=[pl.BlockSpec((ns, c, w), lambda i: (i, 0, 0)),
                   state_spec,
                   pl.BlockSpec((ns, 8, 3 * w), lambda i: (i, 0, 0))],
        out_shape=[jax.ShapeDtypeStruct((b, c, w), F32),
                   jax.ShapeDtypeStruct((b, GDN_HEADS, GDN_D, GDN_D), F32),
                   jax.ShapeDtypeStruct((b, 8, 3 * w), F32)],
        scratch_shapes=[pltpu.VMEM((ns, 8 + c, 3 * w), F32),
                        pltpu.VMEM((ns, c, 3 * w), F32)],
        compiler_params=_cparams(("arbitrary",)),
        name="gdn_sample",
    )(proj3, proj3, proj3, proj3, proj3, conv_w, hp, gn, lvl, conv_in, s_in)


def _t5_bucket(dist):
    d = jnp.maximum(dist, 0)
    max_exact = N_BUCKETS // 2
    large = max_exact + (jnp.log(jnp.maximum(d, 1).astype(F32) / max_exact)
                         / math.log(WINDOW / max_exact) * (N_BUCKETS - max_exact)).astype(jnp.int32)
    large = jnp.minimum(large, N_BUCKETS - 1)
    return jnp.where(d < max_exact, d, large)


def _bias_table(dist, rel_bias):
    valid = (dist >= 0) & (dist < WINDOW)
    onehot = (_t5_bucket(dist)[None] == jnp.arange(N_BUCKETS, dtype=jnp.int32)[:, None, None]).astype(F32)
    b = jnp.einsum('bh,bqs->hqs', rel_bias.astype(F32), onehot, precision=lax.Precision.HIGHEST)
    return jnp.where(valid[None], b, NEG)


def _lo_hi(slab, g):
    lane = lax.broadcasted_iota(jnp.int32, slab.shape, 1)
    if g % 2 == 0:
        lo = jnp.where(lane < SWA_HD, slab, 0.0)
        hi = pltpu.roll(lo, SWA_HD, 1)
    else:
        hi = jnp.where(lane >= SWA_HD, slab, 0.0)
        lo = pltpu.roll(hi, SWA_HD, 1)
    return lo, hi


def _sink_softmax_chains(parts, sinks):
    cs = range(len(parts))
    m = [sinks[i] for i in cs]
    for j in range(len(parts[0])):
        mx = [jnp.max(parts[i][j], axis=-1, keepdims=True) for i in cs]
        m = [jnp.maximum(m[i], mx[i]) for i in cs]
    es = [[jnp.exp(p - m[i]) for p in parts[i]] for i in cs]
    den = [jnp.exp(sinks[i] - m[i]) for i in cs]
    for j in range(len(parts[0])):
        sm = [jnp.sum(es[i][j], axis=-1, keepdims=True) for i in cs]
        den = [den[i] + sm[i] for i in cs]
    inv = [1.0 / d for d in den]
    return [[e * inv[i] for e in es[i]] for i in cs]


SWA_SAMPLE_SEQS = 1
SWA_GROUP_PROMPT = 8


def _swa_prompt_kernel(q_ref, kp_ref, kc_ref, vp_ref, vc_ref, bias_ref, sink_ref, o_ref):
    n = pl.program_id(0)
    wq = WINDOW
    col = lax.broadcasted_iota(jnp.int32, (wq, 2 * wq), 1)
    first = jnp.logical_and(n == 0, col < wq)
    sink_all = sink_ref[...]
    scale = SWA_HD ** -0.5
    for g0 in range(0, SWA_KV, SWA_GROUP_PROMPT):
        gs = range(g0, g0 + SWA_GROUP_PROMPT)
        kv = {}
        for g in gs:
            sl = slice((g // 2) * LANE, (g // 2 + 1) * LANE)
            kslab = jnp.concatenate([kp_ref[:, sl], kc_ref[:, sl]], axis=0)
            vslab = jnp.concatenate([vp_ref[:, sl], vc_ref[:, sl]], axis=0)
            kv[g] = ([x.astype(BF16) for x in _lo_hi(kslab, g)], [x.astype(BF16) for x in _lo_hi(vslab, g)])
        chains = [(g, s, half) for g in gs for s in range(2) for half in range(2)]
        qs = {(g, s): q_ref[:, (2 * g + s) * LANE:(2 * g + s + 1) * LANE].astype(BF16) for g in gs for s in range(2)}
        logits = [_dot_nt(qs[g, s], kv[g][0][half]) for g, s, half in chains]
        logits = [jnp.where(first, NEG, logits[i] * scale + bias_ref[4 * g + 2 * s + half])
                  for i, (g, s, half) in enumerate(chains)]
        sinks = [sink_all[:, 4 * g + 2 * s + half:4 * g + 2 * s + half + 1] for g, s, half in chains]
        probs = _sink_softmax_chains([[x] for x in logits], sinks)
        pv = [_dot(probs[i][0].astype(BF16), kv[g][1][half]) for i, (g, s, half) in enumerate(chains)]
        for i, (g, s, half) in enumerate(chains):
            if half == 0:
                o_ref[:, (2 * g + s) * LANE:(2 * g + s + 1) * LANE] = (pv[i] + pv[i + 1]).astype(o_ref.dtype)


def _swa_prompt(proj, qcol, bias, sinks):
    t = proj.shape[0]
    nb = t // WINDOW
    kb = qcol * (SWA_W // SWA_KVW) + SWA_W // SWA_KVW
    return pl.pallas_call(
        _swa_prompt_kernel,
        grid=(nb,),
        in_specs=[pl.BlockSpec((WINDOW, SWA_W), lambda n: (n, qcol)),
                  pl.BlockSpec((WINDOW, SWA_KVW), lambda n: (jnp.maximum(n - 1, 0), kb)),
                  pl.BlockSpec((WINDOW, SWA_KVW), lambda n: (n, kb)),
                  pl.BlockSpec((WINDOW, SWA_KVW), lambda n: (jnp.maximum(n - 1, 0), kb + 1)),
                  pl.BlockSpec((WINDOW, SWA_KVW), lambda n: (n, kb + 1)),
                  pl.BlockSpec((SWA_HEADS, WINDOW, 2 * WINDOW), lambda n: (0, 0, 0)),
                  pl.BlockSpec((1, LANE), lambda n: (0, 0))],
        out_specs=pl.BlockSpec((WINDOW, SWA_W), lambda n: (n, 0)),
        out_shape=jax.ShapeDtypeStruct((t, SWA_W), BF16),
        compiler_params=_cparams(("arbitrary",)),
        name="swa_prompt",
    )(proj, proj, proj, proj, proj, bias, sinks)


def _swa_sample_kernel(q_ref, kn_ref, vn_ref, kc_ref, vc_ref, bc_ref, bn_ref, sink_ref, o_ref, ko_ref, vo_ref, *, l):
    r = 8
    wb = WINDOW
    nseq = q_ref.shape[0]
    seqs = range(nseq)
    knew = [kn_ref[b] for b in seqs]
    vnew = [vn_ref[b] for b in seqs]
    row = lax.broadcasted_iota(jnp.int32, (LANE, wb), 0)
    col = lax.broadcasted_iota(jnp.int32, (LANE, wb), 1)
    place = jnp.where(jnp.logical_and(row < l, col == row + (wb - l)), 1.0, 0.0).astype(BF16)
    lane_w = lax.broadcasted_iota(jnp.int32, (SWA_KVW, wb), 1)
    for b in seqs:
        for c_ref, new, out_ref in ((kc_ref, knew[b], ko_ref), (vc_ref, vnew[b], vo_ref)):
            hi = new.astype(BF16)
            r1 = new - hi.astype(F32)
            mid = r1.astype(BF16)
            lo = (r1 - mid.astype(F32)).astype(BF16)
            moved = ((_dot_tn(_rowpad(hi, LANE), place) + _dot_tn(_rowpad(mid, LANE), place))
                     + _dot_tn(_rowpad(lo, LANE), place))
            out_ref[b] = jnp.where(lane_w >= wb - l, moved, pltpu.roll(c_ref[b], wb - l, 1))
    sink_all = sink_ref[...]
    scale = SWA_HD ** -0.5
    gs = range(SWA_KV)
    chains = [(b, g, half) for b in seqs for g in gs for half in range(2)]
    sls = [slice((g // 2) * LANE, (g // 2 + 1) * LANE) for g in gs]
    zero = jnp.zeros((SWA_HD, wb), BF16)

    def lo_hi_t(ref, b, g):
        t = ref[b, g * SWA_HD:(g + 1) * SWA_HD, :].astype(BF16)
        return [jnp.concatenate([t, zero], axis=0), jnp.concatenate([zero, t], axis=0)]
    kc = {(b, g): lo_hi_t(kc_ref, b, g) for b in seqs for g in gs}
    vc = {(b, g): lo_hi_t(vc_ref, b, g) for b in seqs for g in gs}
    kn = {(b, g): [_rowpad(x, LANE).astype(BF16) for x in _lo_hi(knew[b][:, sls[g]], g)] for b in seqs for g in gs}
    vn = {(b, g): [_rowpad(x, LANE).astype(BF16) for x in _lo_hi(vnew[b][:, sls[g]], g)] for b in seqs for g in gs}
    qs = {(b, g): jnp.concatenate([q_ref[b, :, (2 * g) * LANE:(2 * g + 1) * LANE],
                                   q_ref[b, :, (2 * g + 1) * LANE:(2 * g + 2) * LANE]], axis=0).astype(BF16)
          for b in seqs for g in gs}
    lc = [_dot(qs[b, g], kc[b, g][half]) for b, g, half in chains]
    ln = [_dot_nt(qs[b, g], kn[b, g][half]) for b, g, half in chains]
    lc = [lc[i] * scale + bc_ref[2 * g + half] for i, (b, g, half) in enumerate(chains)]
    ln = [ln[i] * scale + bn_ref[2 * g + half] for i, (b, g, half) in enumerate(chains)]
    sinks = [jnp.concatenate([jnp.broadcast_to(sink_all[:, 4 * g + half:4 * g + half + 1], (r, 1)),
                              jnp.broadcast_to(sink_all[:, 4 * g + 2 + half:4 * g + 2 + half + 1], (r, 1))], axis=0)
             for b, g, half in chains]
    probs = _sink_softmax_chains([[lc[i], ln[i]] for i in range(len(chains))], sinks)
    pvc = [_dot_nt(probs[i][0].astype(BF16), vc[b, g][half]) for i, (b, g, half) in enumerate(chains)]
    pvn = [_dot(probs[i][1].astype(BF16), vn[b, g][half]) for i, (b, g, half) in enumerate(chains)]
    for i, (b, g, half) in enumerate(chains):
        if half == 0:
            acc = (pvc[i] + pvn[i]) + (pvc[i + 1] + pvn[i + 1])
            o_ref[b, :, (2 * g) * LANE:(2 * g + 1) * LANE] = acc[:r]
            o_ref[b, :, (2 * g + 1) * LANE:(2 * g + 2) * LANE] = acc[r:]


def _swa_sample(proj3, qcol, k_cache, v_cache, bias_c, bias_n, sinks, l):
    b, r, _ = proj3.shape
    kb = qcol * (SWA_W // SWA_KVW) + SWA_W // SWA_KVW
    ns = math.gcd(b, SWA_SAMPLE_SEQS)
    cache_spec = pl.BlockSpec((ns, SWA_KVW, WINDOW), lambda i: (i, 0, 0))
    return pl.pallas_call(
        functools.partial(_swa_sample_kernel, l=l),
        grid=(b // ns,),
        in_specs=[pl.BlockSpec((ns, r, SWA_W), lambda i: (i, 0, qcol)),
                  pl.BlockSpec((ns, r, SWA_KVW), lambda i: (i, 0, kb)),
                  pl.BlockSpec((ns, r, SWA_KVW), lambda i: (i, 0, kb + 1)),
                  cache_spec, cache_spec,
                  pl.BlockSpec((2 * SWA_KV, 2 * r, LANE), lambda i: (0, 0, 0)),
                  pl.BlockSpec((2 * SWA_KV, 2 * r, LANE), lambda i: (0, 0, 0)),
                  pl.BlockSpec((1, LANE), lambda i: (0, 0))],
        out_specs=[pl.BlockSpec((ns, r, SWA_W), lambda i: (i, 0, 0)), cache_spec, cache_spec],
        out_shape=[jax.ShapeDtypeStruct((b, r, SWA_W), F32),
                   jax.ShapeDtypeStruct(k_cache.shape, F32),
                   jax.ShapeDtypeStruct(v_cache.shape, F32)],
        compiler_params=_cparams(("arbitrary",)),
        name="swa_sample",
    )(proj3, proj3, proj3, k_cache, v_cache, bias_c, bias_n, sinks)


MOE_WCHUNK = 1024
MOE_WSLOTS = 6
MOE_LOOKAHEAD = 5
MOE_ROWS_SMALL = 256
MOE_ROW_UNROLL = 8


def _moe_kernel(ord_ref, be_ref, bs_ref, bn_ref, h_hbm, wg_hbm, wu_hbm, wd_hbm, y_hbm,
                xbuf, x16, hgu, hid16, acc, ybuf, wbuf, gsem, ssem, wsem, *, n_blocks, t_all):
    k = pl.program_id(0)
    n = bn_ref[k]
    slot = lax.rem(k, 2)
    d = x16.shape[1]
    half = d // 2
    kc = MOE_WCHUNK // 2
    n_gu = d // kc
    n_dn = d // MOE_WCHUNK
    n_ch = n_gu + n_dn
    assert n_ch % MOE_WSLOTS == 0 and wg_hbm.shape[2] == MOE_WCHUNK

    def gather_copy(tok, r):
        return pltpu.make_async_copy(h_hbm.at[pl.ds(tok, 1)], xbuf.at[pl.ds(r, 1)], gsem)

    def scatter_copy(sl, r, a):
        return pltpu.make_async_copy(ybuf.at[sl, pl.ds(r, 1)], y_hbm.at[pl.ds(a, 1)], ssem.at[sl])

    def w_copies(e, j):
        s = j % MOE_WSLOTS
        if j < n_gu:
            rows = pl.ds(j * kc, kc)
            return [pltpu.make_async_copy(wg_hbm.at[e, rows, :], wbuf.at[s, pl.ds(0, kc), :], wsem.at[s]),
                    pltpu.make_async_copy(wu_hbm.at[e, rows, :], wbuf.at[s, pl.ds(kc, kc), :], wsem.at[s])]
        cols = pl.ds((j - n_gu) * MOE_WCHUNK, MOE_WCHUNK)
        return [pltpu.make_async_copy(wd_hbm.at[e, :, cols], wbuf.at[s], wsem.at[s])]

    def for_rows(cnt, fn):
        groups = lax.shift_right_logical(cnt, MOE_ROW_UNROLL.bit_length() - 1)

        def group(q, carry):
            for u in range(MOE_ROW_UNROLL):
                fn(q * MOE_ROW_UNROLL + u)
            return carry

        def single(r, carry):
            fn(r)
            return carry
        lax.fori_loop(0, groups, group, 0)
        lax.fori_loop(groups * MOE_ROW_UNROLL, cnt, single, 0)

    def gather_start(kk):
        st = bs_ref[kk]

        def one(r):
            a = ord_ref[st + r]
            gather_copy(jnp.where(a >= t_all, a - t_all, a), r).start()
        for_rows(bn_ref[kk], one)

    def gather_wait(kk):
        for_rows(bn_ref[kk], lambda r: gather_copy(0, 0).wait())

    def scatter_start(kk, sl):
        st = bs_ref[kk]
        for_rows(bn_ref[kk], lambda r: scatter_copy(sl, r, ord_ref[st + r]).start())

    def scatter_wait(kk, sl):
        for_rows(bn_ref[kk], lambda r: scatter_copy(sl, 0, 0).wait())

    @pl.when(n > 0)
    def _():
        e = be_ref[k]
        k1 = jnp.minimum(k + 1, n_blocks - 1)
        has_next = jnp.logical_and(k + 1 < n_blocks, bn_ref[k1] > 0)
        e_next = be_ref[k1]

        @pl.when(k == 0)
        def _():
            xbuf[...] = jnp.zeros(xbuf.shape, xbuf.dtype)
            gather_start(0)
            for j in range(MOE_LOOKAHEAD):
                for cp in w_copies(e, j):
                    cp.start()
        gather_wait(k)
        words = xbuf[...]
        lo = lax.bitcast_convert_type(lax.shift_left(words, jnp.uint32(16)), F32)
        hi = lax.bitcast_convert_type(words & jnp.uint32(0xFFFF0000), F32)
        x16[:, 0:half] = lo.astype(BF16)
        x16[:, half:d] = hi.astype(BF16)

        @pl.when(has_next)
        def _():
            gather_start(k1)

        def ffn(m):
            for j in range(n_ch):
                jn = j + MOE_LOOKAHEAD
                if jn < n_ch:
                    for cp in w_copies(e, jn):
                        cp.start()
                else:
                    @pl.when(has_next)
                    def _():
                        for cp in w_copies(e_next, jn - n_ch):
                            cp.start()
                for cp in w_copies(e, j):
                    cp.wait()
                s = j % MOE_WSLOTS
                if j < n_gu:
                    xs = x16[0:m, j * kc:(j + 1) * kc]
                    pg = _dot(xs, wbuf[s, 0:kc, :].astype(BF16))
                    pu = _dot(xs, wbuf[s, kc:2 * kc, :].astype(BF16))
                    if j == 0:
                        hgu[0, 0:m] = pg
                        hgu[1, 0:m] = pu
                    else:
                        hgu[0, 0:m] += pg
                        hgu[1, 0:m] += pu
                    if j == n_gu - 1:
                        hid16[0:m] = (_silu(hgu[0, 0:m]) * hgu[1, 0:m]).astype(BF16)
                else:
                    cols = slice((j - n_gu) * MOE_WCHUNK, (j - n_gu + 1) * MOE_WCHUNK)
                    acc[0:m, cols] = _dot(hid16[0:m], wbuf[s].astype(BF16))
            bits = lax.bitcast_convert_type(acc[0:m].astype(BF16).astype(F32), jnp.uint32)
            ybuf[slot, 0:m] = (lax.shift_right_logical(bits[:, :half], jnp.uint32(16))
                               | (bits[:, half:] & jnp.uint32(0xFFFF0000)))

        @pl.when(n <= MOE_ROWS_SMALL)
        def _():
            ffn(MOE_ROWS_SMALL)

        @pl.when(n > MOE_ROWS_SMALL)
        def _():
            ffn(MOE_ROWS)
        scatter_start(k, slot)

        @pl.when(k > 0)
        def _():
            scatter_wait(k - 1, 1 - slot)

        @pl.when(jnp.logical_not(has_next))
        def _():
            scatter_wait(k, slot)


def _moe_ffn(order, blk_e, blk_start, blk_n, h_packed, w_gate, w_up, w_down):
    a = order.shape[0]
    t_all = h_packed.shape[0]
    d = 2 * h_packed.shape[1]
    n_blocks = blk_e.shape[0]
    grid_spec = pltpu.PrefetchScalarGridSpec(
        num_scalar_prefetch=4,
        grid=(n_blocks,),
        in_specs=[pl.BlockSpec(memory_space=pl.ANY)] * 4,
        out_specs=pl.BlockSpec(memory_space=pl.ANY),
        scratch_shapes=[pltpu.VMEM((MOE_ROWS, d // 2), jnp.uint32),
                        pltpu.VMEM((MOE_ROWS, d), BF16),
                        pltpu.VMEM((2, MOE_ROWS, D_EXPERT), F32),
                        pltpu.VMEM((MOE_ROWS, D_EXPERT), BF16),
                        pltpu.VMEM((MOE_ROWS, d), F32),
                        pltpu.VMEM((2, MOE_ROWS, d // 2), jnp.uint32),
                        pltpu.VMEM((MOE_WSLOTS, MOE_WCHUNK, MOE_WCHUNK), F32),
                        pltpu.SemaphoreType.DMA(()),
                        pltpu.SemaphoreType.DMA((2,)),
                        pltpu.SemaphoreType.DMA((MOE_WSLOTS,))],
    )
    return pl.pallas_call(
        functools.partial(_moe_kernel, n_blocks=n_blocks, t_all=t_all),
        grid_spec=grid_spec,
        out_shape=jax.ShapeDtypeStruct((a, d // 2), jnp.uint32),
        compiler_params=_cparams(("arbitrary",)),
        name="moe_ffn",
    )(order, blk_e, blk_start, blk_n, h_packed, w_gate, w_up, w_down)


def _unpack_pairs(words):
    lo = lax.bitcast_convert_type(lax.shift_left(words, jnp.uint32(16)), F32)
    hi = lax.bitcast_convert_type(words & jnp.uint32(0xFFFF0000), F32)
    return jnp.concatenate([lo, hi], axis=1)


def _final_kernel(x_ref, y0_ref, y1_ref, p_ref, g_ref, nf_ref, o_ref):
    p = p_ref[...]
    moe = _unpack_pairs(y0_ref[...]) * p[:, 0:1] + _unpack_pairs(y1_ref[...]) * p[:, 1:2]
    x = x_ref[...] + g_ref[...] * moe
    o_ref[...] = x * lax.rsqrt(jnp.mean(x * x, axis=-1, keepdims=True) + EPS) * nf_ref[...]


def _final(x, y, row0, gates, gt2, nf, tm=256):
    t, d = x.shape
    t_all = gates.shape[0]
    tm = math.gcd(math.gcd(tm, t), math.gcd(row0, t_all))
    b0 = row0 // tm
    b1 = t_all // tm + b0
    return pl.pallas_call(
        _final_kernel,
        grid=(t // tm,),
        in_specs=[pl.BlockSpec((tm, d), lambda i: (i, 0)),
                  pl.BlockSpec((tm, d // 2), lambda i: (b0 + i, 0)),
                  pl.BlockSpec((tm, d // 2), lambda i: (b1 + i, 0)),
                  pl.BlockSpec((tm, LANE), lambda i: (b0 + i, 0)),
                  _mod_spec(gt2, tm, d),
                  pl.BlockSpec((1, d), lambda i: (0, 0))],
        out_specs=pl.BlockSpec((tm, d), lambda i: (i, 0)),
        out_shape=jax.ShapeDtypeStruct((t, d), F32),
        compiler_params=_cparams(("arbitrary",)),
        name="final",
    )(x, y, y, gates, gt2.arr, nf.reshape(1, d))


def _dispatch(expert_idx):
    t, k = expert_idx.shape
    a = t * k
    n_blocks = a // MOE_ROWS + N_EXPERTS
    flat_e = expert_idx.T.reshape(a)
    order = jnp.argsort(flat_e, stable=True).astype(jnp.int32)
    counts = jnp.zeros((N_EXPERTS,), jnp.int32).at[flat_e].add(1)
    seg_start = jnp.cumsum(counts) - counts
    nblk = (counts + MOE_ROWS - 1) // MOE_ROWS
    blk_end = jnp.cumsum(nblk)
    n_active = blk_end[-1]
    kk = jnp.minimum(jnp.arange(n_blocks, dtype=jnp.int32), n_active - 1)
    blk_e = jnp.minimum(jnp.searchsorted(blk_end, kk, side='right'), N_EXPERTS - 1).astype(jnp.int32)
    j = kk - (blk_end[blk_e] - nblk[blk_e])
    blk_start = (seg_start[blk_e] + j * MOE_ROWS).astype(jnp.int32)
    blk_n = jnp.clip(counts[blk_e] - j * MOE_ROWS, 0, MOE_ROWS)
    blk_n = jnp.where(jnp.arange(n_blocks) < n_active, blk_n, 0).astype(jnp.int32)
    return order, blk_e, blk_start, blk_n


def kernel(x_prompt, x_sample, state_gdn, state_conv, cache_swa_k, cache_swa_v, c_prompt, c_sample, w_ada, b_ada, norm_mix, w_in, conv_w, a_log, dt_bias, gdn_norm, swa_sinks, rel_bias, w_out, norm_moe, router_group, router_group_bias, router_expert, router_expert_bias, w_gate, w_up, w_down, norm_final):
    depth = w_ada.shape[0]
    assert depth == 1
    bp, seq, d = x_prompt.shape
    assert bp == 1 and seq % WINDOW == 0
    bs, ls, _ = x_sample.shape
    tp = bp * seq
    ts = bs * ls
    n_main = 2 * GDN_W + 2 * GDN_W
    n_ba = 2 * GDN_HEADS

    c_all = jnp.concatenate([c_prompt, c_sample], axis=0)
    m_pad = -(-c_all.shape[0] // 16) * 16
    c_all = jnp.pad(c_all, ((0, m_pad - c_all.shape[0]), (0, 0)))
    mod = _adaln(c_all, w_ada[0], b_ada[0])
    mod_rows_s = jnp.repeat(mod[1:1 + bs], ls, axis=0)
    mod_p = [Mod(mod[0:1], j) for j in range(6)]
    mod_s = [Mod(mod_rows_s, j) for j in range(6)]

    n_gdn = n_main + 512
    w_in_t = w_in[0].T
    w_swa_t = w_in_t[n_main + n_ba:]
    w_out16 = w_out[0]
    ba_col = n_main // LANE
    kcol = SWA_W
    hp = jnp.zeros((8, LANE), F32)
    hp = hp.at[0, GDN_HEADS:2 * GDN_HEADS].set(a_log[0]).at[1, GDN_HEADS:2 * GDN_HEADS].set(dt_bias[0])
    gn = gdn_norm[0].reshape(1, GDN_D)
    sinks = jnp.pad(swa_sinks[0], (0, LANE - SWA_HEADS)).reshape(1, LANE)

    xp = x_prompt.reshape(tp, d)
    h_p = _norm_mod(xp, norm_mix[0], mod_p[1], mod_p[0])
    proj_p = _matmul_f32wt(h_p, w_in_t, n_gdn, 1024, 512, name="inproj_p")
    pswa_p = _matmul_f32wt(h_p, w_swa_t, w_swa_t.shape[0], 1024, 512, name="inproj_swa_p")
    og_p, s_p, conv_p = _gdn_prompt(proj_p, ba_col, conv_w[0], hp, gn)
    conv_p = conv_p[8 - (CONV_W - 1):]
    qi = jnp.arange(WINDOW, dtype=jnp.int32)
    sj = jnp.arange(2 * WINDOW, dtype=jnp.int32)
    bias_p = _bias_table((WINDOW + qi)[:, None] - sj[None, :], rel_bias)
    os_p = _swa_prompt(pswa_p, 0, bias_p, sinks)
    k_p = pswa_p[tp - WINDOW:, kcol:kcol + SWA_KVW]
    v_p = pswa_p[tp - WINDOW:, kcol + SWA_KVW:kcol + 2 * SWA_KVW]
    x1_p = _outproj(og_p, os_p, w_out16, xp, mod_p[2])

    xs = x_sample.reshape(ts, d)
    h_s = _norm_mod(xs, norm_mix[0], mod_s[1], mod_s[0])
    proj_s = _matmul_f32wt(h_s, w_in_t, n_gdn, 512, 512, name="inproj_s")
    pswa_s = _matmul_f32wt(h_s, w_swa_t, w_swa_t.shape[0], 512, 512, name="inproj_swa_s")
    rs = 8
    assert CONV_W - 1 <= ls <= rs
    proj_s8 = jnp.pad(proj_s.reshape(bs, ls, n_gdn), ((0, 0), (0, rs - ls), (0, 0)))
    pswa_s8 = jnp.pad(pswa_s.reshape(bs, ls, pswa_s.shape[1]), ((0, 0), (0, rs - ls), (0, 0)))
    conv_in8 = jnp.pad(state_conv[0], ((0, 0), (8 - (CONV_W - 1), 0), (0, 0)))
    og_s, s_s, conv_s = _gdn_sample(proj_s8, ba_col, conv_w[0], hp, gn, conv_in8, state_gdn[0], ls)
    conv_s = conv_s[:, 8 - (CONV_W - 1):]
    wb = cache_swa_k.shape[2]
    assert wb == WINDOW == LANE
    dist_s = (wb + jnp.arange(ls, dtype=jnp.int32))[:, None] - jnp.arange(wb + ls, dtype=jnp.int32)[None, :]
    bias_s = _bias_table(dist_s, rel_bias)
    bias_s = jnp.pad(bias_s, ((0, 0), (0, rs - ls), (0, 2 * LANE - wb - ls)), constant_values=NEG)
    bias_s = bias_s.reshape(SWA_KV, 2, 2, rs, 2 * LANE)
    bias_s = jnp.transpose(bias_s, (0, 2, 1, 3, 4)).reshape(2 * SWA_KV, 2 * rs, 2 * LANE)
    k_cache = jnp.transpose(cache_swa_k[0], (0, 2, 3, 1)).reshape(bs, SWA_KVW, wb)
    v_cache = jnp.transpose(cache_swa_v[0], (0, 2, 3, 1)).reshape(bs, SWA_KVW, wb)
    os_s, k_s, v_s = _swa_sample(pswa_s8, 0, k_cache, v_cache, bias_s[:, :, :LANE], bias_s[:, :, LANE:], sinks, ls)
    x1_s = _outproj(og_s[:, :ls].reshape(ts, GDN_W).astype(BF16), os_s[:, :ls].reshape(ts, SWA_W).astype(BF16),
                    w_out16, xs, mod_s[2], tm=512)

    wr = jnp.concatenate([router_group[0], router_expert[0]], axis=1)
    wr2 = jnp.pad(wr, ((0, 0), (0, LANE - wr.shape[1]))).astype(BF16)
    br = jnp.pad(jnp.concatenate([router_group_bias[0], router_expert_bias[0]]),
                 (0, LANE - N_GROUPS - N_EXPERTS)).reshape(1, LANE)
    t_all = tp + ts
    h2, eidx, gates = _norm_router(x1_p, x1_s, norm_moe[0], mod_p[4], mod_p[3], mod_s[4], mod_s[3], wr2, br)
    order, blk_e, blk_start, blk_n = _dispatch(eidx[:, :2])
    y = _moe_ffn(order, blk_e, blk_start, blk_n, h2, w_gate[0], w_up[0], w_down[0])
    y_p = _final(x1_p, y, 0, gates, mod_p[5], norm_final)
    y_s = _final(x1_s, y, tp, gates, mod_s[5], norm_final)

    sdt = state_gdn.dtype
    return (y_p.reshape(bp, seq, d), y_s.reshape(bs, ls, d),
            s_p.reshape(1, bp, GDN_HEADS, GDN_D, GDN_D).astype(sdt), s_s[None].astype(sdt),
            conv_p.reshape(1, bp, CONV_W - 1, 3 * GDN_W).astype(state_conv.dtype), conv_s[None].astype(state_conv.dtype),
            k_p.reshape(1, bp, WINDOW, SWA_KV, SWA_HD).astype(cache_swa_k.dtype),
            jnp.transpose(k_s.reshape(bs, SWA_KV, SWA_HD, wb), (0, 3, 1, 2))[None].astype(cache_swa_k.dtype),
            v_p.reshape(1, bp, WINDOW, SWA_KV, SWA_HD).astype(cache_swa_v.dtype),
            jnp.transpose(v_s.reshape(bs, SWA_KV, SWA_HD, wb), (0, 3, 1, 2))[None].astype(cache_swa_v.dtype))
```

```python
import functools
import math
from typing import NamedTuple, Optional

import jax
import jax.numpy as jnp
from jax import lax
from jax.experimental import pallas as pl
from jax.experimental.pallas import tpu as pltpu

F32 = jnp.float32
BF16 = jnp.bfloat16
EPS = 1e-6
NEG = -1e30

LANE = 128
GDN_HEADS = 16
GDN_D = 128
GDN_W = GDN_HEADS * GDN_D
CONV_W = 4
SWA_HEADS = 32
SWA_KV = 8
SWA_HD = 64
SWA_W = SWA_HEADS * SWA_HD
SWA_KVW = SWA_KV * SWA_HD
WINDOW = 128
N_BUCKETS = 32
N_GROUPS = 8
EPG = 8
N_EXPERTS = 64
D_EXPERT = 1024
MOE_ROWS = 448
VMEM_LIMIT = 56 * 1024 * 1024


def _cparams(sem):
    return pltpu.CompilerParams(dimension_semantics=sem, vmem_limit_bytes=VMEM_LIMIT)


def _sigmoid(x):
    return 1.0 / (1.0 + jnp.exp(-x))


def _silu(x):
    return x * _sigmoid(x)


def _dot(a, b):
    return jnp.dot(a, b, preferred_element_type=F32)


def _dot_nt(a, b):
    return lax.dot_general(a, b, (((1,), (1,)), ((), ())), preferred_element_type=F32)


def _dot_tn(a, b):
    return lax.dot_general(a, b, (((0,), (0,)), ((), ())), preferred_element_type=F32)


def _ada_kernel(c_ref, w_ref, b_ref, o_ref, a_ref):
    @pl.when(pl.program_id(0) == 0)
    def _():
        a_ref[...] = _silu(c_ref[...]).astype(BF16)
    o_ref[...] = _dot(a_ref[...], w_ref[...].astype(BF16)) + b_ref[...]


def _adaln(c, w, b, tn=512):
    m, d = c.shape
    n = w.shape[1]
    return pl.pallas_call(
        _ada_kernel,
        grid=(n // tn,),
        in_specs=[pl.BlockSpec((m, d), lambda j: (0, 0)),
                  pl.BlockSpec((d, tn), lambda j: (0, j)),
                  pl.BlockSpec((1, tn), lambda j: (0, j))],
        out_specs=pl.BlockSpec((m, tn), lambda j: (0, j)),
        out_shape=jax.ShapeDtypeStruct((m, n), F32),
        scratch_shapes=[pltpu.VMEM((m, d), BF16)],
        compiler_params=_cparams(("arbitrary",)),
        name="adaln",
    )(c, w, b.reshape(1, n))


class Mod(NamedTuple):
    arr: jax.Array
    col: int
    row: Optional[int]


def _mod_spec(mod, tm, d):
    if mod.row is not None:
        return pl.BlockSpec((8, d), lambda i: (mod.row // 8, mod.col))
    return pl.BlockSpec((tm, d), lambda i: (i, mod.col))


def _mod_rows(ref, n):
    assert n != 8
    return ref[...] if ref.shape[0] == n else ref[0:1, :]


def _norm_kernel(x_ref, g_ref, sc_ref, sh_ref, o_ref):
    x = x_ref[...]
    n = x.shape[0]
    y = x * lax.rsqrt(jnp.mean(x * x, axis=-1, keepdims=True) + EPS) * g_ref[...]
    o_ref[...] = (y * (1.0 + _mod_rows(sc_ref, n)) + _mod_rows(sh_ref, n)).astype(o_ref.dtype)


def _norm_mod(x, g, sc, sh, tm=256):
    t, d = x.shape
    tm = min(tm, t)
    return pl.pallas_call(
        _norm_kernel,
        grid=(t // tm,),
        in_specs=[pl.BlockSpec((tm, d), lambda i: (i, 0)),
                  pl.BlockSpec((1, d), lambda i: (0, 0)),
                  _mod_spec(sc, tm, d), _mod_spec(sh, tm, d)],
        out_specs=pl.BlockSpec((tm, d), lambda i: (i, 0)),
        out_shape=jax.ShapeDtypeStruct((t, d), BF16),
        compiler_params=_cparams(("arbitrary",)),
        name="norm_mod",
    )(x, g.reshape(1, d), sc.arr, sh.arr)


def _norm_router_kernel(*refs, nba):
    (xa_ref, xb_ref, g_ref, sca_ref, sha_ref, scb_ref, shb_ref, wr_ref, br_ref,
     o_ref, e_ref, p_ref) = refs
    first = pl.program_id(0) < nba
    x = jnp.where(first, xa_ref[...], xb_ref[...])
    sc = jnp.where(first, sca_ref[0:1, :], scb_ref[...])
    sh = jnp.where(first, sha_ref[0:1, :], shb_ref[...])
    y = x * lax.rsqrt(jnp.mean(x * x, axis=-1, keepdims=True) + EPS) * g_ref[...]
    t = y * (1.0 + sc) + sh
    t16 = t.astype(BF16)
    half = t.shape[1] // 2
    bits = lax.bitcast_convert_type(t16.astype(F32), jnp.uint32)
    o_ref[...] = lax.shift_right_logical(bits[:, :half], jnp.uint32(16)) | (bits[:, half:] & jnp.uint32(0xFFFF0000))
    lg = _dot(t16, wr_ref[...]) + br_ref[...]
    lane = lax.broadcasted_iota(jnp.int32, lg.shape, 1)
    big = jnp.int32(4 * LANE)
    lgrp = jnp.where(lane < N_GROUPS, lg, NEG)
    mg = jnp.max(lgrp, axis=-1, keepdims=True)
    grp = jnp.min(jnp.where(lgrp == mg, lane, big), axis=-1, keepdims=True)
    p_grp = 1.0 / jnp.sum(jnp.exp(lgrp - mg), axis=-1, keepdims=True)
    lo = N_GROUPS + grp * EPG
    emask = jnp.logical_and(lane >= lo, lane < lo + EPG)
    le = jnp.where(emask, lg, NEG)
    m1 = jnp.max(le, axis=-1, keepdims=True)
    i1 = jnp.min(jnp.where(le == m1, lane, big), axis=-1, keepdims=True)
    le2 = jnp.where(lane == i1, NEG, le)
    m2 = jnp.max(le2, axis=-1, keepdims=True)
    i2 = jnp.min(jnp.where(le2 == m2, lane, big), axis=-1, keepdims=True)
    e2 = jnp.exp(m2 - m1)
    w1 = p_grp / (1.0 + e2)
    w2 = p_grp * e2 / (1.0 + e2)
    e_ref[...] = jnp.where(lane == 0, i1 - N_GROUPS, jnp.where(lane == 1, i2 - N_GROUPS, 0))
    p_ref[...] = jnp.where(lane == 0, w1, jnp.where(lane == 1, w2, 0.0))


def _norm_router(xa, xb, g, sca, sha, scb, shb, wr, br, tm=256):
    ta, d = xa.shape
    tb = xb.shape[0]
    tm = min(tm, math.gcd(ta, tb))
    nba = ta // tm
    nbb = tb // tm
    t_all = ta + tb
    spec_a = pl.BlockSpec((tm, d), lambda i: (jnp.minimum(i, nba - 1), 0))
    spec_b = pl.BlockSpec((tm, d), lambda i: (jnp.maximum(i - nba, 0), 0))
    row = pl.BlockSpec((1, d), lambda i: (0, 0))

    def mod_a(m):
        return pl.BlockSpec((8, d), lambda i: (m.row // 8, m.col))

    def mod_b(m):
        return pl.BlockSpec((tm, d), lambda i: (jnp.maximum(i - nba, 0), m.col))
    return pl.pallas_call(
        functools.partial(_norm_router_kernel, nba=nba),
        grid=(nba + nbb,),
        in_specs=[spec_a, spec_b, row, mod_a(sca), mod_a(sha), mod_b(scb), mod_b(shb),
                  pl.BlockSpec((d, LANE), lambda i: (0, 0)),
                  pl.BlockSpec((1, LANE), lambda i: (0, 0))],
        out_specs=[pl.BlockSpec((tm, d // 2), lambda i: (i, 0)),
                   pl.BlockSpec((tm, LANE), lambda i: (i, 0)),
                   pl.BlockSpec((tm, LANE), lambda i: (i, 0))],
        out_shape=[jax.ShapeDtypeStruct((t_all, d // 2), jnp.uint32),
                   jax.ShapeDtypeStruct((t_all, LANE), jnp.int32),
                   jax.ShapeDtypeStruct((t_all, LANE), F32)],
        compiler_params=_cparams(("arbitrary",)),
        name="norm_router",
    )(xa, xb, g.reshape(1, d), sca.arr, sha.arr, scb.arr, shb.arr, wr, br)


def _mm_kernel(x_ref, w_ref, o_ref):
    o_ref[...] = _dot(x_ref[...], w_ref[...]).astype(o_ref.dtype)


def _matmul(x, w, tm, tn, out_dtype=F32, name="matmul"):
    m, k = x.shape
    n = w.shape[1]
    tm = min(tm, m)
    tn = min(tn, n)
    return pl.pallas_call(
        _mm_kernel,
        grid=(m // tm, n // tn),
        in_specs=[pl.BlockSpec((tm, k), lambda i, j: (i, 0)),
                  pl.BlockSpec((k, tn), lambda i, j: (0, j))],
        out_specs=pl.BlockSpec((tm, tn), lambda i, j: (i, j)),
        out_shape=jax.ShapeDtypeStruct((m, n), out_dtype),
        compiler_params=_cparams(("arbitrary", "arbitrary")),
        name=name,
    )(x, w)


def _mm_f32wt_kernel(x_ref, wt_ref, o_ref):
    o_ref[...] = _dot_nt(x_ref[...], wt_ref[...].astype(BF16))


def _matmul_f32wt(x, wt, n, tm, tn, name):
    m, k = x.shape
    tm = min(tm, m)
    return pl.pallas_call(
        _mm_f32wt_kernel,
        grid=(m // tm, n // tn),
        in_specs=[pl.BlockSpec((tm, k), lambda i, j: (i, 0)),
                  pl.BlockSpec((tn, k), lambda i, j: (j, 0))],
        out_specs=pl.BlockSpec((tm, tn), lambda i, j: (i, j)),
        out_shape=jax.ShapeDtypeStruct((m, n), F32),
        compiler_params=_cparams(("arbitrary", "arbitrary")),
        name=name,
    )(x, wt)


def _outproj_kernel(a_ref, b_ref, wa_ref, wb_ref, x_ref, g_ref, o_ref):
    mix = _dot(a_ref[...], wa_ref[...].astype(BF16)) + _dot(b_ref[...], wb_ref[...].astype(BF16))
    o_ref[...] = x_ref[...] + _mod_rows(g_ref, x_ref.shape[0]) * mix


def _outproj(a, b, w, x, gate, tm=1024, tn=512):
    m, ka = a.shape
    kb = b.shape[1]
    n = w.shape[1]
    tm = min(tm, m)
    g0 = gate.col * (n // tn)
    if gate.row is not None:
        gspec = pl.BlockSpec((8, tn), lambda i, j: (gate.row // 8, g0 + j))
    else:
        gspec = pl.BlockSpec((tm, tn), lambda i, j: (i, g0 + j))
    return pl.pallas_call(
        _outproj_kernel,
        grid=(m // tm, n // tn),
        in_specs=[pl.BlockSpec((tm, ka), lambda i, j: (i, 0)),
                  pl.BlockSpec((tm, kb), lambda i, j: (i, 0)),
                  pl.BlockSpec((ka, tn), lambda i, j: (0, j)),
                  pl.BlockSpec((kb, tn), lambda i, j: (1, j)),
                  pl.BlockSpec((tm, tn), lambda i, j: (i, j)),
                  gspec],
        out_specs=pl.BlockSpec((tm, tn), lambda i, j: (i, j)),
        out_shape=jax.ShapeDtypeStruct((m, n), F32),
        compiler_params=_cparams(("arbitrary", "arbitrary")),
        name="outproj",
    )(a, b, w, w, x, gate.arr)


GDN_GROUP_PROMPT = 8
GDN_SAMPLE_SEQS = 8
GDN_GROUP_SAMPLE = 32


def _rowpad(a, rows):
    if a.shape[0] == rows:
        return a
    return jnp.concatenate([a, jnp.zeros((rows - a.shape[0], a.shape[1]), a.dtype)], axis=0)


def _gdn_heads(qn, kn, vh, bcol, gcol, grow, s_prev, lvl_ref, c):
    cp = LANE
    n = len(qn)
    hs = range(n)
    lane = lax.broadcasted_iota(jnp.int32, (c, cp), 1)
    row = lax.broadcasted_iota(jnp.int32, (c, cp), 0)
    tri = lane <= row

    def b16(x):
        return x.astype(BF16)

    kn16 = [b16(k) for k in kn]
    qkn16 = [jnp.concatenate([b16(qn[h]), kn16[h]], axis=0) for h in hs]
    knp16 = [_rowpad(k, cp) for k in kn16]
    qk_kk = [_dot_nt(qkn16[h], knp16[h]) for h in hs]
    decay = [jnp.where(tri, jnp.exp(jnp.minimum(gcol[h] - grow[h], 0.0)), 0.0) for h in hs]
    eg = [jnp.exp(g) for g in gcol]
    lmat = [(bcol[h] * qk_kk[h][c:]) * decay[h] for h in hs]
    qkd16 = [b16(qk_kk[h][:c] * decay[h]) for h in hs]
    n_mat = [-(lm * lvl_ref[0]) for lm in lmat]
    for lv in range(1, lvl_ref.shape[0]):
        bm = [lm * lvl_ref[lv] for lm in lmat]
        n16 = [b16(x) for x in n_mat]
        w_mat = [bm[h] + _dot(n16[h], _rowpad(b16(bm[h]), cp)) for h in hs]
        n_mat = [n_mat[h] - w_mat[h] - _dot(b16(w_mat[h]), _rowpad(n16[h], cp)) for h in hs]
    rhs = [jnp.concatenate([vh[h] * bcol[h], kn[h] * (bcol[h] * eg[h])], axis=1) for h in hs]
    sol = [rhs[h] + _dot(b16(n_mat[h]), _rowpad(b16(rhs[h]), cp)) for h in hs]
    kq16 = [jnp.concatenate([b16(sol[h][:, GDN_D:]), b16(qn[h] * eg[h])], axis=0) for h in hs]
    kq_s = [_dot(kq16[h], b16(s_prev[h])) for h in hs]
    u = [sol[h][:, :GDN_D] - kq_s[h][:c] for h in hs]
    u16 = [_rowpad(b16(x), cp) for x in u]
    g_last = [g[c - 1:c, :] for g in gcol]
    kd16 = [_rowpad(b16(kn[h] * jnp.exp(g_last[h] - gcol[h])), cp) for h in hs]
    o = [kq_s[h][c:] + _dot(qkd16[h], u16[h]) for h in hs]
    s_new = [s_prev[h] * jnp.exp(g_last[h]) + _dot_tn(kd16[h], u16[h]) for h in hs]
    return o, s_new


def _gdn_kernel(*refs, c, l_valid, prompt, n_chunks, group):
    if prompt:
        (q_ref, k_ref, v_ref, z_ref, ba_ref, cw_ref, hp_ref, gn_ref, lvl_ref,
         o_ref, sout_ref, cout_ref, xe_ref, cb_ref, s_ref) = refs
        nseq = 1
    else:
        (q_ref, k_ref, v_ref, z_ref, ba_ref, cw_ref, hp_ref, gn_ref, lvl_ref, cin_ref, sin_ref,
         o_ref, sout_ref, cout_ref, xe_ref, cb_ref) = refs
        nseq = q_ref.shape[0]
    i = pl.program_id(0)
    w = GDN_W

    def blk(ref, b):
        return ref if prompt else ref.at[b]

    if prompt:
        @pl.when(i == 0)
        def _():
            xe_ref[0, 0:8, :] = jnp.zeros((8, 3 * w), F32)
            s_ref[...] = jnp.zeros(s_ref.shape, F32)
    rowc = lax.broadcasted_iota(jnp.int32, (c, 512), 0)
    row = lax.broadcasted_iota(jnp.int32, (c, LANE), 0)
    beta_all = []
    gcum_all = []
    gcum_t_all = []
    for b in range(nseq):
        if not prompt:
            xe_ref[b, 0:8, :] = cin_ref[b]
        xe_ref[b, 8:8 + c, 0:w] = blk(q_ref, b)[...]
        xe_ref[b, 8:8 + c, w:2 * w] = blk(k_ref, b)[...]
        xe_ref[b, 8:8 + c, 2 * w:3 * w] = blk(v_ref, b)[...]
        for cb in range(3 * w // 512):
            cs = slice(cb * 512, (cb + 1) * 512)
            acc = xe_ref[b, 8:8 + c, cs] * cw_ref[CONV_W - 1:CONV_W, cs]
            for j in range(CONV_W - 1):
                acc = acc + xe_ref[b, 8 - (CONV_W - 1) + j:8 - (CONV_W - 1) + j + c, cs] * cw_ref[j:j + 1, cs]
            y = _silu(acc)
            if l_valid < c:
                y = jnp.where(rowc < l_valid, y, 0.0)
            cb_ref[b, :, cs] = y
        new_conv = xe_ref[b, l_valid:l_valid + 8, :]
        blk(cout_ref, b)[...] = new_conv
        if prompt:
            xe_ref[b, 0:8, :] = new_conv

        ba = blk(ba_ref, b)[...]
        beta = _sigmoid(ba)
        xg = ba + hp_ref[1:2, :]
        softplus = jnp.maximum(xg, 0.0) + jnp.log1p(jnp.exp(-jnp.abs(xg)))
        g_all = -jnp.exp(hp_ref[0:1, :]) * softplus
        if l_valid < c:
            beta = jnp.where(row < l_valid, beta, 0.0)
            g_all = jnp.where(row < l_valid, g_all, 0.0)
        gcum = g_all
        s = 1
        while s < c:
            gcum = gcum + jnp.where(row >= s, pltpu.roll(gcum, s, 0), 0.0)
            s *= 2
        beta_all.append(beta)
        gcum_all.append(gcum)
        gcum_t_all.append(jnp.transpose(_rowpad(gcum, LANE)))
    gn = gn_ref[...]

    def col(h, part=0):
        return slice(part * w + h * GDN_D, part * w + (h + 1) * GDN_D)

    chains = [(b, h) for b in range(nseq) for h in range(GDN_HEADS)]
    for c0 in range(0, len(chains), group):
        ch = chains[c0:c0 + group]
        qh = [cb_ref[b, :, col(h)] for b, h in ch]
        kh = [cb_ref[b, :, col(h, 1)] for b, h in ch]
        vh = [cb_ref[b, :, col(h, 2)] for b, h in ch]
        qn = [x * lax.rsqrt(jnp.sum(x * x, axis=-1, keepdims=True) + EPS) * (GDN_D ** -0.5) for x in qh]
        kn = [x * lax.rsqrt(jnp.sum(x * x, axis=-1, keepdims=True) + EPS) for x in kh]
        bcol = [beta_all[b][:, h:h + 1] for b, h in ch]
        gcol = [gcum_all[b][:, h + GDN_HEADS:h + GDN_HEADS + 1] for b, h in ch]
        grow = [gcum_t_all[b][h + GDN_HEADS:h + GDN_HEADS + 1, :] for b, h in ch]
        s_prev = [s_ref[h] if prompt else sin_ref[b, h] for b, h in ch]
        o, s_new = _gdn_heads(qn, kn, vh, bcol, gcol, grow, s_prev, lvl_ref, c)
        for j, (b, h) in enumerate(ch):
            if prompt:
                s_ref[h] = s_new[j]
            else:
                sout_ref[b, h] = s_new[j]
        zh = [blk(z_ref, b)[:, col(h)] for b, h in ch]
        on = [x * lax.rsqrt(jnp.mean(x * x, axis=-1, keepdims=True) + EPS) * gn for x in o]
        for j, (b, h) in enumerate(ch):
            blk(o_ref, b)[:, col(h)] = (on[j] * _silu(zh[j])).astype(o_ref.dtype)
    if prompt:
        @pl.when(i == n_chunks - 1)
        def _():
            sout_ref[...] = s_ref[...]


def _level_masks(c, l_valid):
    i = jnp.arange(c, dtype=jnp.int32)[:, None]
    j = jnp.arange(LANE, dtype=jnp.int32)[None, :]
    masks = []
    s = 1
    while s < l_valid:
        masks.append((i // (2 * s) == j // (2 * s)) & (i % (2 * s) >= s) & (j % (2 * s) < s))
        s *= 2
    return jnp.stack(masks).astype(F32)


def _gdn_prompt(proj, ba_col, conv_w, hp, gn, c=128):
    t = proj.shape[0]
    n_chunks = t // c
    w = GDN_W
    lvl = _level_masks(c, c)
    kern = functools.partial(_gdn_kernel, c=c, l_valid=c, prompt=True, n_chunks=n_chunks, group=GDN_GROUP_PROMPT)
    return pl.pallas_call(
        kern,
        grid=(n_chunks,),
        in_specs=[pl.BlockSpec((c, w), lambda i: (i, 0)),
                  pl.BlockSpec((c, w), lambda i: (i, 1)),
                  pl.BlockSpec((c, w), lambda i: (i, 2)),
                  pl.BlockSpec((c, w), lambda i: (i, 3)),
                  pl.BlockSpec((c, LANE), lambda i: (i, ba_col)),
                  pl.BlockSpec((CONV_W, 3 * w), lambda i: (0, 0)),
                  pl.BlockSpec((8, LANE), lambda i: (0, 0)),
                  pl.BlockSpec((1, GDN_D), lambda i: (0, 0)),
                  pl.BlockSpec(lvl.shape, lambda i: (0, 0, 0))],
        out_specs=[pl.BlockSpec((c, w), lambda i: (i, 0)),
                   pl.BlockSpec((GDN_HEADS, GDN_D, GDN_D), lambda i: (0, 0, 0)),
                   pl.BlockSpec((8, 3 * w), lambda i: (0, 0))],
        out_shape=[jax.ShapeDtypeStruct((t, w), BF16),
                   jax.ShapeDtypeStruct((GDN_HEADS, GDN_D, GDN_D), F32),
                   jax.ShapeDtypeStruct((8, 3 * w), F32)],
        scratch_shapes=[pltpu.VMEM((1, 8 + c, 3 * w), F32),
                        pltpu.VMEM((1, c, 3 * w), F32),
                        pltpu.VMEM((GDN_HEADS, GDN_D, GDN_D), F32)],
        compiler_params=_cparams(("arbitrary",)),
        name="gdn_prompt",
    )(proj, proj, proj, proj, proj, conv_w, hp, gn, lvl)


def _gdn_sample(proj3, ba_col, conv_w, hp, gn, conv_in, s_in, l):
    b, c, _ = proj3.shape
    w = GDN_W
    lvl = _level_masks(c, l)
    kern = functools.partial(_gdn_kernel, c=c, l_valid=l, prompt=False, n_chunks=1, group=GDN_GROUP_SAMPLE)
    ns = math.gcd(b, GDN_SAMPLE_SEQS)
    state_spec = pl.BlockSpec((ns, GDN_HEADS, GDN_D, GDN_D), lambda i: (i, 0, 0, 0))
    return pl.pallas_call(
        kern,
        grid=(b // ns,),
        in_specs=[pl.BlockSpec((ns, c, w), lambda i: (i, 0, 0)),
                  pl.BlockSpec((ns, c, w), lambda i: (i, 0, 1)),
                  pl.BlockSpec((ns, c, w), lambda i: (i, 0, 2)),
                  pl.BlockSpec((ns, c, w), lambda i: (i, 0, 3)),
                  pl.BlockSpec((ns, c, LANE), lambda i: (i, 0, ba_col)),
                  pl.BlockSpec((CONV_W, 3 * w), lambda i: (0, 0)),
                  pl.BlockSpec((8, LANE), lambda i: (0, 0)),
                  pl.BlockSpec((1, GDN_D), lambda i: (0, 0)),
                  pl.BlockSpec(lvl.shape, lambda i: (0, 0, 0)),
                  pl.BlockSpec((ns, 8, 3 * w), lambda i: (i, 0, 0)),
                  state_spec],
        out_specs=[pl.BlockSpec((ns, c, w), lambda i: (i, 0, 0)),
                   state_spec,
                   pl.BlockSpec((ns, 8, 3 * w), lambda i: (i, 0, 0))],
        out_shape=[jax.ShapeDtypeStruct((b, c, w), F32),
                   jax.ShapeDtypeStruct((b, GDN_HEADS, GDN_D, GDN_D), F32),
                   jax.ShapeDtypeStruct((b, 8, 3 * w), F32)],
        scratch_shapes=[pltpu.VMEM((ns, 8 + c, 3 * w), F32),
                        pltpu.VMEM((ns, c, 3 * w), F32)],
        compiler_params=_cparams(("arbitrary",)),
        name="gdn_sample",
    )(proj3, proj3, proj3, proj3, proj3, conv_w, hp, gn, lvl, conv_in, s_in)


def _t5_bucket(dist):
    d = jnp.maximum(dist, 0)
    max_exact = N_BUCKETS // 2
    large = max_exact + (jnp.log(jnp.maximum(d, 1).astype(F32) / max_exact)
                         / math.log(WINDOW / max_exact) * (N_BUCKETS - max_exact)).astype(jnp.int32)
    large = jnp.minimum(large, N_BUCKETS - 1)
    return jnp.where(d < max_exact, d, large)


def _bias_table(dist, rel_bias):
    valid = (dist >= 0) & (dist < WINDOW)
    onehot = (_t5_bucket(dist)[None] == jnp.arange(N_BUCKETS, dtype=jnp.int32)[:, None, None]).astype(F32)
    b = jnp.einsum('bh,bqs->hqs', rel_bias.astype(F32), onehot, precision=lax.Precision.HIGHEST)
    return jnp.where(valid[None], b, NEG)


def _lo_hi(slab, g):
    lane = lax.broadcasted_iota(jnp.int32, slab.shape, 1)
    if g % 2 == 0:
        lo = jnp.where(lane < SWA_HD, slab, 0.0)
        hi = pltpu.roll(lo, SWA_HD, 1)
    else:
        hi = jnp.where(lane >= SWA_HD, slab, 0.0)
        lo = pltpu.roll(hi, SWA_HD, 1)
    return lo, hi


def _sink_softmax_chains(parts, sinks):
    cs = range(len(parts))
    m = [sinks[i] for i in cs]
    for j in range(len(parts[0])):
        mx = [jnp.max(parts[i][j], axis=-1, keepdims=True) for i in cs]
        m = [jnp.maximum(m[i], mx[i]) for i in cs]
    es = [[jnp.exp(p - m[i]) for p in parts[i]] for i in cs]
    den = [jnp.exp(sinks[i] - m[i]) for i in cs]
    for j in range(len(parts[0])):
        sm = [jnp.sum(es[i][j], axis=-1, keepdims=True) for i in cs]
        den = [den[i] + sm[i] for i in cs]
    inv = [1.0 / d for d in den]
    return [[e * inv[i] for e in es[i]] for i in cs]


SWA_SAMPLE_SEQS = 1
SWA_GROUP_PROMPT = 8


def _swa_prompt_kernel(q_ref, kp_ref, kc_ref, vp_ref, vc_ref, bias_ref, sink_ref, o_ref):
    n = pl.program_id(0)
    wq = WINDOW
    col = lax.broadcasted_iota(jnp.int32, (wq, 2 * wq), 1)
    first = jnp.logical_and(n == 0, col < wq)
    sink_all = sink_ref[...]
    scale = SWA_HD ** -0.5
    for g0 in range(0, SWA_KV, SWA_GROUP_PROMPT):
        gs = range(g0, g0 + SWA_GROUP_PROMPT)
        kv = {}
        for g in gs:
            sl = slice((g // 2) * LANE, (g // 2 + 1) * LANE)
            kslab = jnp.concatenate([kp_ref[:, sl], kc_ref[:, sl]], axis=0)
            vslab = jnp.concatenate([vp_ref[:, sl], vc_ref[:, sl]], axis=0)
            kv[g] = ([x.astype(BF16) for x in _lo_hi(kslab, g)], [x.astype(BF16) for x in _lo_hi(vslab, g)])
        chains = [(g, s, half) for g in gs for s in range(2) for half in range(2)]
        qs = {(g, s): q_ref[:, (2 * g + s) * LANE:(2 * g + s + 1) * LANE].astype(BF16) for g in gs for s in range(2)}
        logits = [_dot_nt(qs[g, s], kv[g][0][half]) for g, s, half in chains]
        logits = [jnp.where(first, NEG, logits[i] * scale + bias_ref[4 * g + 2 * s + half])
                  for i, (g, s, half) in enumerate(chains)]
        sinks = [sink_all[:, 4 * g + 2 * s + half:4 * g + 2 * s + half + 1] for g, s, half in chains]
        probs = _sink_softmax_chains([[x] for x in logits], sinks)
        pv = [_dot(probs[i][0].astype(BF16), kv[g][1][half]) for i, (g, s, half) in enumerate(chains)]
        for i, (g, s, half) in enumerate(chains):
            if half == 0:
                o_ref[:, (2 * g + s) * LANE:(2 * g + s + 1) * LANE] = (pv[i] + pv[i + 1]).astype(o_ref.dtype)


def _swa_prompt(proj, qcol, bias, sinks):
    t = proj.shape[0]
    nb = t // WINDOW
    kb = qcol * (SWA_W // SWA_KVW) + SWA_W // SWA_KVW
    return pl.pallas_call(
        _swa_prompt_kernel,
        grid=(nb,),
        in_specs=[pl.BlockSpec((WINDOW, SWA_W), lambda n: (n, qcol)),
                  pl.BlockSpec((WINDOW, SWA_KVW), lambda n: (jnp.maximum(n - 1, 0), kb)),
                  pl.BlockSpec((WINDOW, SWA_KVW), lambda n: (n, kb)),
                  pl.BlockSpec((WINDOW, SWA_KVW), lambda n: (jnp.maximum(n - 1, 0), kb + 1)),
                  pl.BlockSpec((WINDOW, SWA_KVW), lambda n: (n, kb + 1)),
                  pl.BlockSpec((SWA_HEADS, WINDOW, 2 * WINDOW), lambda n: (0, 0, 0)),
                  pl.BlockSpec((1, LANE), lambda n: (0, 0))],
        out_specs=pl.BlockSpec((WINDOW, SWA_W), lambda n: (n, 0)),
        out_shape=jax.ShapeDtypeStruct((t, SWA_W), BF16),
        compiler_params=_cparams(("arbitrary",)),
        name="swa_prompt",
    )(proj, proj, proj, proj, proj, bias, sinks)


def _swa_sample_kernel(q_ref, kn_ref, vn_ref, kc_ref, vc_ref, bc_ref, bn_ref, sink_ref, o_ref, ko_ref, vo_ref, *, l):
    r = 8
    wb = WINDOW
    nseq = q_ref.shape[0]
    seqs = range(nseq)
    knew = [kn_ref[b] for b in seqs]
    vnew = [vn_ref[b] for b in seqs]
    row = lax.broadcasted_iota(jnp.int32, (LANE, wb), 0)
    col = lax.broadcasted_iota(jnp.int32, (LANE, wb), 1)
    place = jnp.where(jnp.logical_and(row < l, col == row + (wb - l)), 1.0, 0.0).astype(BF16)
    lane_w = lax.broadcasted_iota(jnp.int32, (SWA_KVW, wb), 1)
    for b in seqs:
        for c_ref, new, out_ref in ((kc_ref, knew[b], ko_ref), (vc_ref, vnew[b], vo_ref)):
            hi = new.astype(BF16)
            r1 = new - hi.astype(F32)
            mid = r1.astype(BF16)
            lo = (r1 - mid.astype(F32)).astype(BF16)
            moved = ((_dot_tn(_rowpad(hi, LANE), place) + _dot_tn(_rowpad(mid, LANE), place))
                     + _dot_tn(_rowpad(lo, LANE), place))
            out_ref[b] = jnp.where(lane_w >= wb - l, moved, pltpu.roll(c_ref[b], wb - l, 1))
    sink_all = sink_ref[...]
    scale = SWA_HD ** -0.5
    gs = range(SWA_KV)
    chains = [(b, g, half) for b in seqs for g in gs for half in range(2)]
    sls = [slice((g // 2) * LANE, (g // 2 + 1) * LANE) for g in gs]
    zero = jnp.zeros((SWA_HD, wb), BF16)

    def lo_hi_t(ref, b, g):
        t = ref[b, g * SWA_HD:(g + 1) * SWA_HD, :].astype(BF16)
        return [jnp.concatenate([t, zero], axis=0), jnp.concatenate([zero, t], axis=0)]
    kc = {(b, g): lo_hi_t(kc_ref, b, g) for b in seqs for g in gs}
    vc = {(b, g): lo_hi_t(vc_ref, b, g) for b in seqs for g in gs}
    kn = {(b, g): [_rowpad(x, LANE).astype(BF16) for x in _lo_hi(knew[b][:, sls[g]], g)] for b in seqs for g in gs}
    vn = {(b, g): [_rowpad(x, LANE).astype(BF16) for x in _lo_hi(vnew[b][:, sls[g]], g)] for b in seqs for g in gs}
    qs = {(b, g): jnp.concatenate([q_ref[b, :, (2 * g) * LANE:(2 * g + 1) * LANE],
                                   q_ref[b, :, (2 * g + 1) * LANE:(2 * g + 2) * LANE]], axis=0).astype(BF16)
          for b in seqs for g in gs}
    lc = [_dot(qs[b, g], kc[b, g][half]) for b, g, half in chains]
    ln = [_dot_nt(qs[b, g], kn[b, g][half]) for b, g, half in chains]
    lc = [lc[i] * scale + bc_ref[2 * g + half] for i, (b, g, half) in enumerate(chains)]
    ln = [ln[i] * scale + bn_ref[2 * g + half] for i, (b, g, half) in enumerate(chains)]
    sinks = [jnp.concatenate([jnp.broadcast_to(sink_all[:, 4 * g + half:4 * g + half + 1], (r, 1)),
                              jnp.broadcast_to(sink_all[:, 4 * g + 2 + half:4 * g + 2 + half + 1], (r, 1))], axis=0)
             for b, g, half in chains]
    probs = _sink_softmax_chains([[lc[i], ln[i]] for i in range(len(chains))], sinks)
    pvc = [_dot_nt(probs[i][0].astype(BF16), vc[b, g][half]) for i, (b, g, half) in enumerate(chains)]
    pvn = [_dot(probs[i][1].astype(BF16), vn[b, g][half]) for i, (b, g, half) in enumerate(chains)]
    for i, (b, g, half) in enumerate(chains):
        if half == 0:
            acc = (pvc[i] + pvn[i]) + (pvc[i + 1] + pvn[i + 1])
            o_ref[b, :, (2 * g) * LANE:(2 * g + 1) * LANE] = acc[:r]
            o_ref[b, :, (2 * g + 1) * LANE:(2 * g + 2) * LANE] = acc[r:]


def _swa_sample(proj3, qcol, k_cache, v_cache, bias_c, bias_n, sinks, l):
    b, r, _ = proj3.shape
    kb = qcol * (SWA_W // SWA_KVW) + SWA_W // SWA_KVW
    ns = math.gcd(b, SWA_SAMPLE_SEQS)
    cache_spec = pl.BlockSpec((ns, SWA_KVW, WINDOW), lambda i: (i, 0, 0))
    return pl.pallas_call(
        functools.partial(_swa_sample_kernel, l=l),
        grid=(b // ns,),
        in_specs=[pl.BlockSpec((ns, r, SWA_W), lambda i: (i, 0, qcol)),
                  pl.BlockSpec((ns, r, SWA_KVW), lambda i: (i, 0, kb)),
                  pl.BlockSpec((ns, r, SWA_KVW), lambda i: (i, 0, kb + 1)),
                  cache_spec, cache_spec,
                  pl.BlockSpec((2 * SWA_KV, 2 * r, LANE), lambda i: (0, 0, 0)),
                  pl.BlockSpec((2 * SWA_KV, 2 * r, LANE), lambda i: (0, 0, 0)),
                  pl.BlockSpec((1, LANE), lambda i: (0, 0))],
        out_specs=[pl.BlockSpec((ns, r, SWA_W), lambda i: (i, 0, 0)), cache_spec, cache_spec],
        out_shape=[jax.ShapeDtypeStruct((b, r, SWA_W), F32),
                   jax.ShapeDtypeStruct(k_cache.shape, F32),
                   jax.ShapeDtypeStruct(v_cache.shape, F32)],
        compiler_params=_cparams(("arbitrary",)),
        name="swa_sample",
    )(proj3, proj3, proj3, k_cache, v_cache, bias_c, bias_n, sinks)


MOE_WCHUNK = 1024
MOE_WSLOTS = 6
MOE_LOOKAHEAD = 5
MOE_ROWS_SMALL = 256
MOE_ROW_UNROLL = 8


def _moe_kernel(ord_ref, be_ref, bs_ref, bn_ref, h_hbm, wg_hbm, wu_hbm, wd_hbm, y_hbm,
                xbuf, x16, hgu, hid16, acc, ybuf, wbuf, gsem, ssem, wsem, *, n_blocks, t_all):
    k = pl.program_id(0)
    n = bn_ref[k]
    slot = lax.rem(k, 2)
    d = x16.shape[1]
    half = d // 2
    kc = MOE_WCHUNK // 2
    n_gu = d // kc
    n_dn = d // MOE_WCHUNK
    n_ch = n_gu + n_dn
    assert n_ch % MOE_WSLOTS == 0 and wg_hbm.shape[2] == MOE_WCHUNK

    def gather_copy(tok, r):
        return pltpu.make_async_copy(h_hbm.at[pl.ds(tok, 1)], xbuf.at[pl.ds(r, 1)], gsem)

    def scatter_copy(sl, r, a):
        return pltpu.make_async_copy(ybuf.at[sl, pl.ds(r, 1)], y_hbm.at[pl.ds(a, 1)], ssem.at[sl])

    def w_copies(e, j):
        s = j % MOE_WSLOTS
        if j < n_gu:
            rows = pl.ds(j * kc, kc)
            return [pltpu.make_async_copy(wg_hbm.at[e, rows, :], wbuf.at[s, pl.ds(0, kc), :], wsem.at[s]),
                    pltpu.make_async_copy(wu_hbm.at[e, rows, :], wbuf.at[s, pl.ds(kc, kc), :], wsem.at[s])]
        cols = pl.ds((j - n_gu) * MOE_WCHUNK, MOE_WCHUNK)
        return [pltpu.make_async_copy(wd_hbm.at[e, :, cols], wbuf.at[s], wsem.at[s])]

    def for_rows(cnt, fn):
        groups = lax.shift_right_logical(cnt, MOE_ROW_UNROLL.bit_length() - 1)

        def group(q, carry):
            for u in range(MOE_ROW_UNROLL):
                fn(q * MOE_ROW_UNROLL + u)
            return carry

        def single(r, carry):
            fn(r)
            return carry
        lax.fori_loop(0, groups, group, 0)
        lax.fori_loop(groups * MOE_ROW_UNROLL, cnt, single, 0)

    def gather_start(kk):
        st = bs_ref[kk]

        def one(r):
            a = ord_ref[st + r]
            gather_copy(jnp.where(a >= t_all, a - t_all, a), r).start()
        for_rows(bn_ref[kk], one)

    def gather_wait(kk):
        for_rows(bn_ref[kk], lambda r: gather_copy(0, 0).wait())

    def scatter_start(kk, sl):
        st = bs_ref[kk]
        for_rows(bn_ref[kk], lambda r: scatter_copy(sl, r, ord_ref[st + r]).start())

    def scatter_wait(kk, sl):
        for_rows(bn_ref[kk], lambda r: scatter_copy(sl, 0, 0).wait())

    @pl.when(n > 0)
    def _():
        e = be_ref[k]
        k1 = jnp.minimum(k + 1, n_blocks - 1)
        has_next = jnp.logical_and(k + 1 < n_blocks, bn_ref[k1] > 0)
        e_next = be_ref[k1]

        @pl.when(k == 0)
        def _():
            xbuf[...] = jnp.zeros(xbuf.shape, xbuf.dtype)
            gather_start(0)
            for j in range(MOE_LOOKAHEAD):
                for cp in w_copies(e, j):
                    cp.start()
        gather_wait(k)
        words = xbuf[...]
        lo = lax.bitcast_convert_type(lax.shift_left(words, jnp.uint32(16)), F32)
        hi = lax.bitcast_convert_type(words & jnp.uint32(0xFFFF0000), F32)
        x16[:, 0:half] = lo.astype(BF16)
        x16[:, half:d] = hi.astype(BF16)

        @pl.when(has_next)
        def _():
            gather_start(k1)

        def ffn(m):
            for j in range(n_ch):
                jn = j + MOE_LOOKAHEAD
                if jn < n_ch:
                    for cp in w_copies(e, jn):
                        cp.start()
                else:
                    @pl.when(has_next)
                    def _():
                        for cp in w_copies(e_next, jn - n_ch):
                            cp.start()
                for cp in w_copies(e, j):
                    cp.wait()
                s = j % MOE_WSLOTS
                if j < n_gu:
                    xs = x16[0:m, j * kc:(j + 1) * kc]
                    pg = _dot(xs, wbuf[s, 0:kc, :].astype(BF16))
                    pu = _dot(xs, wbuf[s, kc:2 * kc, :].astype(BF16))
                    if j == 0:
                        hgu[0, 0:m] = pg
                        hgu[1, 0:m] = pu
                    else:
                        hgu[0, 0:m] += pg
                        hgu[1, 0:m] += pu
                    if j == n_gu - 1:
                        hid16[0:m] = (_silu(hgu[0, 0:m]) * hgu[1, 0:m]).astype(BF16)
                else:
                    cols = slice((j - n_gu) * MOE_WCHUNK, (j - n_gu + 1) * MOE_WCHUNK)
                    acc[0:m, cols] = _dot(hid16[0:m], wbuf[s].astype(BF16))
            bits = lax.bitcast_convert_type(acc[0:m].astype(BF16).astype(F32), jnp.uint32)
            ybuf[slot, 0:m] = (lax.shift_right_logical(bits[:, :half], jnp.uint32(16))
                               | (bits[:, half:] & jnp.uint32(0xFFFF0000)))

        @pl.when(n <= MOE_ROWS_SMALL)
        def _():
            ffn(MOE_ROWS_SMALL)

        @pl.when(n > MOE_ROWS_SMALL)
        def _():
            ffn(MOE_ROWS)
        scatter_start(k, slot)

        @pl.when(k > 0)
        def _():
            scatter_wait(k - 1, 1 - slot)

        @pl.when(jnp.logical_not(has_next))
        def _():
            scatter_wait(k, slot)


def _moe_ffn(order, blk_e, blk_start, blk_n, h_packed, w_gate, w_up, w_down):
    a = order.shape[0]
    t_all = h_packed.shape[0]
    d = 2 * h_packed.shape[1]
    n_blocks = blk_e.shape[0]
    grid_spec = pltpu.PrefetchScalarGridSpec(
        num_scalar_prefetch=4,
        grid=(n_blocks,),
        in_specs=[pl.BlockSpec(memory_space=pl.ANY)] * 4,
        out_specs=pl.BlockSpec(memory_space=pl.ANY),
        scratch_shapes=[pltpu.VMEM((MOE_ROWS, d // 2), jnp.uint32),
                        pltpu.VMEM((MOE_ROWS, d), BF16),
                        pltpu.VMEM((2, MOE_ROWS, D_EXPERT), F32),
                        pltpu.VMEM((MOE_ROWS, D_EXPERT), BF16),
                        pltpu.VMEM((MOE_ROWS, d), F32),
                        pltpu.VMEM((2, MOE_ROWS, d // 2), jnp.uint32),
                        pltpu.VMEM((MOE_WSLOTS, MOE_WCHUNK, MOE_WCHUNK), F32),
                        pltpu.SemaphoreType.DMA(()),
                        pltpu.SemaphoreType.DMA((2,)),
                        pltpu.SemaphoreType.DMA((MOE_WSLOTS,))],
    )
    return pl.pallas_call(
        functools.partial(_moe_kernel, n_blocks=n_blocks, t_all=t_all),
        grid_spec=grid_spec,
        out_shape=jax.ShapeDtypeStruct((a, d // 2), jnp.uint32),
        compiler_params=_cparams(("arbitrary",)),
        name="moe_ffn",
    )(order, blk_e, blk_start, blk_n, h_packed, w_gate, w_up, w_down)


def _unpack_pairs(words):
    lo = lax.bitcast_convert_type(lax.shift_left(words, jnp.uint32(16)), F32)
    hi = lax.bitcast_convert_type(words & jnp.uint32(0xFFFF0000), F32)
    return jnp.concatenate([lo, hi], axis=1)


def _final_kernel(x_ref, y0_ref, y1_ref, p_ref, g_ref, nf_ref, o_ref):
    p = p_ref[...]
    moe = _unpack_pairs(y0_ref[...]) * p[:, 0:1] + _unpack_pairs(y1_ref[...]) * p[:, 1:2]
    x = x_ref[...] + _mod_rows(g_ref, x_ref.shape[0]) * moe
    o_ref[...] = x * lax.rsqrt(jnp.mean(x * x, axis=-1, keepdims=True) + EPS) * nf_ref[...]


def _final(x, y, row0, gates, gt2, nf, tm=256):
    t, d = x.shape
    t_all = gates.shape[0]
    tm = math.gcd(math.gcd(tm, t), math.gcd(row0, t_all))
    b0 = row0 // tm
    b1 = t_all // tm + b0
    return pl.pallas_call(
        _final_kernel,
        grid=(t // tm,),
        in_specs=[pl.BlockSpec((tm, d), lambda i: (i, 0)),
                  pl.BlockSpec((tm, d // 2), lambda i: (b0 + i, 0)),
                  pl.BlockSpec((tm, d // 2), lambda i: (b1 + i, 0)),
                  pl.BlockSpec((tm, LANE), lambda i: (b0 + i, 0)),
                  _mod_spec(gt2, tm, d),
                  pl.BlockSpec((1, d), lambda i: (0, 0))],
        out_specs=pl.BlockSpec((tm, d), lambda i: (i, 0)),
        out_shape=jax.ShapeDtypeStruct((t, d), F32),
        compiler_params=_cparams(("arbitrary",)),
        name="final",
    )(x, y, y, gates, gt2.arr, nf.reshape(1, d))


def _dispatch(expert_idx):
    t, k = expert_idx.shape
    a = t * k
    n_blocks = a // MOE_ROWS + N_EXPERTS
    flat_e = expert_idx.T.reshape(a)
    order = jnp.argsort(flat_e, stable=True).astype(jnp.int32)
    counts = jnp.zeros((N_EXPERTS,), jnp.int32).at[flat_e].add(1)
    seg_start = jnp.cumsum(counts) - counts
    nblk = (counts + MOE_ROWS - 1) // MOE_ROWS
    blk_end = jnp.cumsum(nblk)
    n_active = blk_end[-1]
    kk = jnp.minimum(jnp.arange(n_blocks, dtype=jnp.int32), n_active - 1)
    blk_e = jnp.minimum(jnp.searchsorted(blk_end, kk, side='right'), N_EXPERTS - 1).astype(jnp.int32)
    j = kk - (blk_end[blk_e] - nblk[blk_e])
    blk_start = (seg_start[blk_e] + j * MOE_ROWS).astype(jnp.int32)
    blk_n = jnp.clip(counts[blk_e] - j * MOE_ROWS, 0, MOE_ROWS)
    blk_n = jnp.where(jnp.arange(n_blocks) < n_active, blk_n, 0).astype(jnp.int32)
    return order, blk_e, blk_start, blk_n


def kernel(x_prompt, x_sample, state_gdn, state_conv, cache_swa_k, cache_swa_v, c_prompt, c_sample, w_ada, b_ada, norm_mix, w_in, conv_w, a_log, dt_bias, gdn_norm, swa_sinks, rel_bias, w_out, norm_moe, router_group, router_group_bias, router_expert, router_expert_bias, w_gate, w_up, w_down, norm_final):
    depth = w_ada.shape[0]
    assert depth == 1
    bp, seq, d = x_prompt.shape
    assert bp == 1 and seq % WINDOW == 0
    bs, ls, _ = x_sample.shape
    tp = bp * seq
    ts = bs * ls
    n_main = 2 * GDN_W + 2 * GDN_W
    n_ba = 2 * GDN_HEADS

    assert ts % 8 == 0
    c_all = jnp.concatenate([jnp.repeat(c_sample, ls, axis=0), c_prompt, jnp.zeros((15, d), c_prompt.dtype)], axis=0)
    mod = _adaln(c_all, w_ada[0], b_ada[0])
    mod_p = [Mod(mod, j, ts) for j in range(6)]
    mod_s = [Mod(mod, j, None) for j in range(6)]

    n_gdn = n_main + 512
    w_in_t = w_in[0].T
    w_swa_t = w_in_t[n_main + n_ba:]
    w_out16 = w_out[0]
    ba_col = n_main // LANE
    kcol = SWA_W
    hp = jnp.zeros((8, LANE), F32)
    hp = hp.at[0, GDN_HEADS:2 * GDN_HEADS].set(a_log[0]).at[1, GDN_HEADS:2 * GDN_HEADS].set(dt_bias[0])
    gn = gdn_norm[0].reshape(1, GDN_D)
    sinks = jnp.pad(swa_sinks[0], (0, LANE - SWA_HEADS)).reshape(1, LANE)

    xp = x_prompt.reshape(tp, d)
    h_p = _norm_mod(xp, norm_mix[0], mod_p[1], mod_p[0])
    proj_p = _matmul_f32wt(h_p, w_in_t, n_gdn, 1024, 512, name="inproj_p")
    pswa_p = _matmul_f32wt(h_p, w_swa_t, w_swa_t.shape[0], 1024, 512, name="inproj_swa_p")
    og_p, s_p, conv_p = _gdn_prompt(proj_p, ba_col, conv_w[0], hp, gn)
    conv_p = conv_p[8 - (CONV_W - 1):]
    qi = jnp.arange(WINDOW, dtype=jnp.int32)
    sj = jnp.arange(2 * WINDOW, dtype=jnp.int32)
    bias_p = _bias_table((WINDOW + qi)[:, None] - sj[None, :], rel_bias)
    os_p = _swa_prompt(pswa_p, 0, bias_p, sinks)
    k_p = pswa_p[tp - WINDOW:, kcol:kcol + SWA_KVW]
    v_p = pswa_p[tp - WINDOW:, kcol + SWA_KVW:kcol + 2 * SWA_KVW]
    x1_p = _outproj(og_p, os_p, w_out16, xp, mod_p[2])

    xs = x_sample.reshape(ts, d)
    h_s = _norm_mod(xs, norm_mix[0], mod_s[1], mod_s[0])
    proj_s = _matmul_f32wt(h_s, w_in_t, n_gdn, 512, 512, name="inproj_s")
    pswa_s = _matmul_f32wt(h_s, w_swa_t, w_swa_t.shape[0], 512, 512, name="inproj_swa_s")
    rs = 8
    assert CONV_W - 1 <= ls <= rs
    proj_s8 = jnp.pad(proj_s.reshape(bs, ls, n_gdn), ((0, 0), (0, rs - ls), (0, 0)))
    pswa_s8 = jnp.pad(pswa_s.reshape(bs, ls, pswa_s.shape[1]), ((0, 0), (0, rs - ls), (0, 0)))
    conv_in8 = jnp.pad(state_conv[0], ((0, 0), (8 - (CONV_W - 1), 0), (0, 0)))
    og_s, s_s, conv_s = _gdn_sample(proj_s8, ba_col, conv_w[0], hp, gn, conv_in8, state_gdn[0], ls)
    conv_s = conv_s[:, 8 - (CONV_W - 1):]
    wb = cache_swa_k.shape[2]
    assert wb == WINDOW == LANE
    dist_s = (wb + jnp.arange(ls, dtype=jnp.int32))[:, None] - jnp.arange(wb + ls, dtype=jnp.int32)[None, :]
    bias_s = _bias_table(dist_s, rel_bias)
    bias_s = jnp.pad(bias_s, ((0, 0), (0, rs - ls), (0, 2 * LANE - wb - ls)), constant_values=NEG)
    bias_s = bias_s.reshape(SWA_KV, 2, 2, rs, 2 * LANE)
    bias_s = jnp.transpose(bias_s, (0, 2, 1, 3, 4)).reshape(2 * SWA_KV, 2 * rs, 2 * LANE)
    k_cache = jnp.transpose(cache_swa_k[0], (0, 2, 3, 1)).reshape(bs, SWA_KVW, wb)
    v_cache = jnp.transpose(cache_swa_v[0], (0, 2, 3, 1)).reshape(bs, SWA_KVW, wb)
    os_s, k_s, v_s = _swa_sample(pswa_s8, 0, k_cache, v_cache, bias_s[:, :, :LANE], bias_s[:, :, LANE:], sinks, ls)
    x1_s = _outproj(og_s[:, :ls].reshape(ts, GDN_W).astype(BF16), os_s[:, :ls].reshape(ts, SWA_W).astype(BF16),
                    w_out16, xs, mod_s[2], tm=512)

    wr = jnp.concatenate([router_group[0], router_expert[0]], axis=1)
    wr2 = jnp.pad(wr, ((0, 0), (0, LANE - wr.shape[1]))).astype(BF16)
    br = jnp.pad(jnp.concatenate([router_group_bias[0], router_expert_bias[0]]),
                 (0, LANE - N_GROUPS - N_EXPERTS)).reshape(1, LANE)
    t_all = tp + ts
    h2, eidx, gates = _norm_router(x1_p, x1_s, norm_moe[0], mod_p[4], mod_p[3], mod_s[4], mod_s[3], wr2, br)
    order, blk_e, blk_start, blk_n = _dispatch(eidx[:, :2])
    y = _moe_ffn(order, blk_e, blk_start, blk_n, h2, w_gate[0], w_up[0], w_down[0])
    y_p = _final(x1_p, y, 0, gates, mod_p[5], norm_final)
    y_s = _final(x1_s, y, tp, gates, mod_s[5], norm_final)

    sdt = state_gdn.dtype
    return (y_p.reshape(bp, seq, d), y_s.reshape(bs, ls, d),
            s_p.reshape(1, bp, GDN_HEADS, GDN_D, GDN_D).astype(sdt), s_s[None].astype(sdt),
            conv_p.reshape(1, bp, CONV_W - 1, 3 * GDN_W).astype(state_conv.dtype), conv_s[None].astype(state_conv.dtype),
            k_p.reshape(1, bp, WINDOW, SWA_KV, SWA_HD).astype(cache_swa_k.dtype),
            jnp.transpose(k_s.reshape(bs, SWA_KV, SWA_HD, wb), (0, 3, 1, 2))[None].astype(cache_swa_k.dtype),
            v_p.reshape(1, bp, WINDOW, SWA_KV, SWA_HD).astype(cache_swa_v.dtype),
            jnp.transpose(v_s.reshape(bs, SWA_KV, SWA_HD, wb), (0, 3, 1, 2))[None].astype(cache_swa_v.dtype))
```

```python
import functools
import math
from typing import NamedTuple, Optional

import jax
import jax.numpy as jnp
from jax import lax
from jax.experimental import pallas as pl
from jax.experimental.pallas import tpu as pltpu

F32 = jnp.float32
BF16 = jnp.bfloat16
EPS = 1e-6
NEG = -1e30

LANE = 128
GDN_HEADS = 16
GDN_D = 128
GDN_W = GDN_HEADS * GDN_D
CONV_W = 4
SWA_HEADS = 32
SWA_KV = 8
SWA_HD = 64
SWA_W = SWA_HEADS * SWA_HD
SWA_KVW = SWA_KV * SWA_HD
WINDOW = 128
N_BUCKETS = 32
N_GROUPS = 8
EPG = 8
N_EXPERTS = 64
D_EXPERT = 1024
MOE_ROWS = 448
VMEM_LIMIT = 56 * 1024 * 1024


def _cparams(sem):
    return pltpu.CompilerParams(dimension_semantics=sem, vmem_limit_bytes=VMEM_LIMIT)


def _sigmoid(x):
    return 1.0 / (1.0 + jnp.exp(-x))


def _silu(x):
    return x * _sigmoid(x)


def _dot(a, b):
    return jnp.dot(a, b, preferred_element_type=F32)


def _dot_nt(a, b):
    return lax.dot_general(a, b, (((1,), (1,)), ((), ())), preferred_element_type=F32)


def _dot_tn(a, b):
    return lax.dot_general(a, b, (((0,), (0,)), ((), ())), preferred_element_type=F32)


def _ada_kernel(c_ref, w_ref, b_ref, o_ref, a_ref):
    @pl.when(pl.program_id(0) == 0)
    def _():
        a_ref[...] = _silu(c_ref[...]).astype(BF16)
    o_ref[...] = _dot(a_ref[...], w_ref[...].astype(BF16)) + b_ref[...]


def _adaln(c, w, b, tn=512):
    m, d = c.shape
    n = w.shape[1]
    return pl.pallas_call(
        _ada_kernel,
        grid=(n // tn,),
        in_specs=[pl.BlockSpec((m, d), lambda j: (0, 0)),
                  pl.BlockSpec((d, tn), lambda j: (0, j)),
                  pl.BlockSpec((1, tn), lambda j: (0, j))],
        out_specs=pl.BlockSpec((m, tn), lambda j: (0, j)),
        out_shape=jax.ShapeDtypeStruct((m, n), F32),
        scratch_shapes=[pltpu.VMEM((m, d), BF16)],
        compiler_params=_cparams(("arbitrary",)),
        name="adaln",
    )(c, w, b.reshape(1, n))


class Mod(NamedTuple):
    arr: jax.Array
    col: int
    row: Optional[int]


def _mod_spec(mod, tm, d):
    if mod.row is not None:
        return pl.BlockSpec((8, d), lambda i: (mod.row // 8, mod.col))
    return pl.BlockSpec((tm, d), lambda i: (i, mod.col))


def _mod_rows(ref, n):
    assert n != 8
    return ref[...] if ref.shape[0] == n else ref[0:1, :]


def _norm_kernel(x_ref, g_ref, sc_ref, sh_ref, o_ref):
    x = x_ref[...]
    n = x.shape[0]
    y = x * lax.rsqrt(jnp.mean(x * x, axis=-1, keepdims=True) + EPS) * g_ref[...]
    o_ref[...] = (y * (1.0 + _mod_rows(sc_ref, n)) + _mod_rows(sh_ref, n)).astype(o_ref.dtype)


def _norm_mod(x, g, sc, sh, tm=256):
    t, d = x.shape
    tm = min(tm, t)
    return pl.pallas_call(
        _norm_kernel,
        grid=(t // tm,),
        in_specs=[pl.BlockSpec((tm, d), lambda i: (i, 0)),
                  pl.BlockSpec((1, d), lambda i: (0, 0)),
                  _mod_spec(sc, tm, d), _mod_spec(sh, tm, d)],
        out_specs=pl.BlockSpec((tm, d), lambda i: (i, 0)),
        out_shape=jax.ShapeDtypeStruct((t, d), BF16),
        compiler_params=_cparams(("arbitrary",)),
        name="norm_mod",
    )(x, g.reshape(1, d), sc.arr, sh.arr)


def _norm_router_kernel(*refs, nba):
    (xa_ref, xb_ref, g_ref, sca_ref, sha_ref, scb_ref, shb_ref, wr_ref, br_ref,
     o_ref, e_ref, p_ref) = refs
    first = pl.program_id(0) < nba
    x = jnp.where(first, xa_ref[...], xb_ref[...])
    sc = jnp.where(first, sca_ref[0:1, :], scb_ref[...])
    sh = jnp.where(first, sha_ref[0:1, :], shb_ref[...])
    y = x * lax.rsqrt(jnp.mean(x * x, axis=-1, keepdims=True) + EPS) * g_ref[...]
    t = y * (1.0 + sc) + sh
    t16 = t.astype(BF16)
    half = t.shape[1] // 2
    bits = lax.bitcast_convert_type(t16.astype(F32), jnp.uint32)
    o_ref[...] = lax.shift_right_logical(bits[:, :half], jnp.uint32(16)) | (bits[:, half:] & jnp.uint32(0xFFFF0000))
    lg = _dot(t16, wr_ref[...]) + br_ref[...]
    lane = lax.broadcasted_iota(jnp.int32, lg.shape, 1)
    big = jnp.int32(4 * LANE)
    lgrp = jnp.where(lane < N_GROUPS, lg, NEG)
    mg = jnp.max(lgrp, axis=-1, keepdims=True)
    grp = jnp.min(jnp.where(lgrp == mg, lane, big), axis=-1, keepdims=True)
    p_grp = 1.0 / jnp.sum(jnp.exp(lgrp - mg), axis=-1, keepdims=True)
    lo = N_GROUPS + grp * EPG
    emask = jnp.logical_and(lane >= lo, lane < lo + EPG)
    le = jnp.where(emask, lg, NEG)
    m1 = jnp.max(le, axis=-1, keepdims=True)
    i1 = jnp.min(jnp.where(le == m1, lane, big), axis=-1, keepdims=True)
    le2 = jnp.where(lane == i1, NEG, le)
    m2 = jnp.max(le2, axis=-1, keepdims=True)
    i2 = jnp.min(jnp.where(le2 == m2, lane, big), axis=-1, keepdims=True)
    e2 = jnp.exp(m2 - m1)
    w1 = p_grp / (1.0 + e2)
    w2 = p_grp * e2 / (1.0 + e2)
    e_ref[...] = jnp.where(lane == 0, i1 - N_GROUPS, jnp.where(lane == 1, i2 - N_GROUPS, 0))
    p_ref[...] = jnp.where(lane == 0, w1, jnp.where(lane == 1, w2, 0.0))


def _norm_router(xa, xb, g, sca, sha, scb, shb, wr, br, tm=256):
    ta, d = xa.shape
    tb = xb.shape[0]
    tm = min(tm, math.gcd(ta, tb))
    nba = ta // tm
    nbb = tb // tm
    t_all = ta + tb
    spec_a = pl.BlockSpec((tm, d), lambda i: (jnp.minimum(i, nba - 1), 0))
    spec_b = pl.BlockSpec((tm, d), lambda i: (jnp.maximum(i - nba, 0), 0))
    row = pl.BlockSpec((1, d), lambda i: (0, 0))

    def mod_a(m):
        return pl.BlockSpec((8, d), lambda i: (m.row // 8, m.col))

    def mod_b(m):
        return pl.BlockSpec((tm, d), lambda i: (jnp.maximum(i - nba, 0), m.col))
    return pl.pallas_call(
        functools.partial(_norm_router_kernel, nba=nba),
        grid=(nba + nbb,),
        in_specs=[spec_a, spec_b, row, mod_a(sca), mod_a(sha), mod_b(scb), mod_b(shb),
                  pl.BlockSpec((d, LANE), lambda i: (0, 0)),
                  pl.BlockSpec((1, LANE), lambda i: (0, 0))],
        out_specs=[pl.BlockSpec((tm, d // 2), lambda i: (i, 0)),
                   pl.BlockSpec((tm, LANE), lambda i: (i, 0)),
                   pl.BlockSpec((tm, LANE), lambda i: (i, 0))],
        out_shape=[jax.ShapeDtypeStruct((t_all, d // 2), jnp.uint32),
                   jax.ShapeDtypeStruct((t_all, LANE), jnp.int32),
                   jax.ShapeDtypeStruct((t_all, LANE), F32)],
        compiler_params=_cparams(("arbitrary",)),
        name="norm_router",
    )(xa, xb, g.reshape(1, d), sca.arr, sha.arr, scb.arr, shb.arr, wr, br)


def _mm_kernel(x_ref, w_ref, o_ref):
    o_ref[...] = _dot(x_ref[...], w_ref[...]).astype(o_ref.dtype)


def _matmul(x, w, tm, tn, out_dtype=F32, name="matmul"):
    m, k = x.shape
    n = w.shape[1]
    tm = min(tm, m)
    tn = min(tn, n)
    return pl.pallas_call(
        _mm_kernel,
        grid=(m // tm, n // tn),
        in_specs=[pl.BlockSpec((tm, k), lambda i, j: (i, 0)),
                  pl.BlockSpec((k, tn), lambda i, j: (0, j))],
        out_specs=pl.BlockSpec((tm, tn), lambda i, j: (i, j)),
        out_shape=jax.ShapeDtypeStruct((m, n), out_dtype),
        compiler_params=_cparams(("arbitrary", "arbitrary")),
        name=name,
    )(x, w)


def _mm_f32wt_kernel(x_ref, wt_ref, o_ref):
    o_ref[...] = _dot_nt(x_ref[...], wt_ref[...].astype(BF16))


def _matmul_f32wt(x, wt, n, tm, tn, name):
    m, k = x.shape
    tm = min(tm, m)
    return pl.pallas_call(
        _mm_f32wt_kernel,
        grid=(m // tm, n // tn),
        in_specs=[pl.BlockSpec((tm, k), lambda i, j: (i, 0)),
                  pl.BlockSpec((tn, k), lambda i, j: (j, 0))],
        out_specs=pl.BlockSpec((tm, tn), lambda i, j: (i, j)),
        out_shape=jax.ShapeDtypeStruct((m, n), F32),
        compiler_params=_cparams(("arbitrary", "arbitrary")),
        name=name,
    )(x, wt)


def _outproj_kernel(a_ref, b_ref, wa_ref, wb_ref, x_ref, g_ref, o_ref):
    mix = _dot(a_ref[...], wa_ref[...].astype(BF16)) + _dot(b_ref[...], wb_ref[...].astype(BF16))
    o_ref[...] = x_ref[...] + _mod_rows(g_ref, x_ref.shape[0]) * mix


def _outproj(a, b, w, x, gate, tm=1024, tn=512):
    m, ka = a.shape
    kb = b.shape[1]
    n = w.shape[1]
    tm = min(tm, m)
    g0 = gate.col * (n // tn)
    if gate.row is not None:
        gspec = pl.BlockSpec((8, tn), lambda i, j: (gate.row // 8, g0 + j))
    else:
        gspec = pl.BlockSpec((tm, tn), lambda i, j: (i, g0 + j))
    return pl.pallas_call(
        _outproj_kernel,
        grid=(m // tm, n // tn),
        in_specs=[pl.BlockSpec((tm, ka), lambda i, j: (i, 0)),
                  pl.BlockSpec((tm, kb), lambda i, j: (i, 0)),
                  pl.BlockSpec((ka, tn), lambda i, j: (0, j)),
                  pl.BlockSpec((kb, tn), lambda i, j: (1, j)),
                  pl.BlockSpec((tm, tn), lambda i, j: (i, j)),
                  gspec],
        out_specs=pl.BlockSpec((tm, tn), lambda i, j: (i, j)),
        out_shape=jax.ShapeDtypeStruct((m, n), F32),
        compiler_params=_cparams(("arbitrary", "arbitrary")),
        name="outproj",
    )(a, b, w, w, x, gate.arr)


GDN_GROUP_PROMPT = 8
GDN_SAMPLE_SEQS = 8
GDN_GROUP_SAMPLE = 32


def _rowpad(a, rows):
    if a.shape[0] == rows:
        return a
    return jnp.concatenate([a, jnp.zeros((rows - a.shape[0], a.shape[1]), a.dtype)], axis=0)


def _gdn_heads(qn, kn, vh, bcol, gcol, grow, s_prev, lvl_ref, c):
    cp = LANE
    n = len(qn)
    hs = range(n)
    lane = lax.broadcasted_iota(jnp.int32, (c, cp), 1)
    row = lax.broadcasted_iota(jnp.int32, (c, cp), 0)
    tri = lane <= row

    def b16(x):
        return x.astype(BF16)

    kn16 = [b16(k) for k in kn]
    qkn16 = [jnp.concatenate([b16(qn[h]), kn16[h]], axis=0) for h in hs]
    knp16 = [_rowpad(k, cp) for k in kn16]
    qk_kk = [_dot_nt(qkn16[h], knp16[h]) for h in hs]
    decay = [jnp.where(tri, jnp.exp(jnp.minimum(gcol[h] - grow[h], 0.0)), 0.0) for h in hs]
    eg = [jnp.exp(g) for g in gcol]
    lmat = [(bcol[h] * qk_kk[h][c:]) * decay[h] for h in hs]
    qkd16 = [b16(qk_kk[h][:c] * decay[h]) for h in hs]
    n_mat = [-(lm * lvl_ref[0]) for lm in lmat]
    for lv in range(1, lvl_ref.shape[0]):
        bm = [lm * lvl_ref[lv] for lm in lmat]
        n16 = [b16(x) for x in n_mat]
        w_mat = [bm[h] + _dot(n16[h], _rowpad(b16(bm[h]), cp)) for h in hs]
        n_mat = [n_mat[h] - w_mat[h] - _dot(b16(w_mat[h]), _rowpad(n16[h], cp)) for h in hs]
    rhs = [jnp.concatenate([vh[h] * bcol[h], kn[h] * (bcol[h] * eg[h])], axis=1) for h in hs]
    sol = [rhs[h] + _dot(b16(n_mat[h]), _rowpad(b16(rhs[h]), cp)) for h in hs]
    kq16 = [jnp.concatenate([b16(sol[h][:, GDN_D:]), b16(qn[h] * eg[h])], axis=0) for h in hs]
    kq_s = [_dot(kq16[h], b16(s_prev[h])) for h in hs]
    u = [sol[h][:, :GDN_D] - kq_s[h][:c] for h in hs]
    u16 = [_rowpad(b16(x), cp) for x in u]
    g_last = [g[c - 1:c, :] for g in gcol]
    kd16 = [_rowpad(b16(kn[h] * jnp.exp(g_last[h] - gcol[h])), cp) for h in hs]
    o = [kq_s[h][c:] + _dot(qkd16[h], u16[h]) for h in hs]
    s_new = [s_prev[h] * jnp.exp(g_last[h]) + _dot_tn(kd16[h], u16[h]) for h in hs]
    return o, s_new


def _gdn_kernel(*refs, c, l_valid, prompt, n_chunks, group):
    if prompt:
        (q_ref, k_ref, v_ref, z_ref, ba_ref, cw_ref, hp_ref, gn_ref, lvl_ref,
         o_ref, sout_ref, cout_ref, xe_ref, cb_ref, s_ref) = refs
        nseq = 1
    else:
        (q_ref, k_ref, v_ref, z_ref, ba_ref, cw_ref, hp_ref, gn_ref, lvl_ref, cin_ref, sin_ref,
         o_ref, sout_ref, cout_ref, xe_ref, cb_ref) = refs
        nseq = q_ref.shape[0]
    i = pl.program_id(0)
    w = GDN_W

    def blk(ref, b):
        return ref if prompt else ref.at[b]

    if prompt:
        @pl.when(i == 0)
        def _():
            xe_ref[0, 0:8, :] = jnp.zeros((8, 3 * w), F32)
            s_ref[...] = jnp.zeros(s_ref.shape, F32)
    rowc = lax.broadcasted_iota(jnp.int32, (c, 512), 0)
    row = lax.broadcasted_iota(jnp.int32, (c, LANE), 0)
    beta_all = []
    gcum_all = []
    gcum_t_all = []
    for b in range(nseq):
        if not prompt:
            xe_ref[b, 0:8, :] = cin_ref[b]
        xe_ref[b, 8:8 + c, 0:w] = blk(q_ref, b)[...]
        xe_ref[b, 8:8 + c, w:2 * w] = blk(k_ref, b)[...]
        xe_ref[b, 8:8 + c, 2 * w:3 * w] = blk(v_ref, b)[...]
        for cb in range(3 * w // 512):
            cs = slice(cb * 512, (cb + 1) * 512)
            acc = xe_ref[b, 8:8 + c, cs] * cw_ref[CONV_W - 1:CONV_W, cs]
            for j in range(CONV_W - 1):
                acc = acc + xe_ref[b, 8 - (CONV_W - 1) + j:8 - (CONV_W - 1) + j + c, cs] * cw_ref[j:j + 1, cs]
            y = _silu(acc)
            if l_valid < c:
                y = jnp.where(rowc < l_valid, y, 0.0)
            cb_ref[b, :, cs] = y
        new_conv = xe_ref[b, l_valid:l_valid + 8, :]
        blk(cout_ref, b)[...] = new_conv
        if prompt:
            xe_ref[b, 0:8, :] = new_conv

        ba = blk(ba_ref, b)[...]
        beta = _sigmoid(ba)
        xg = ba + hp_ref[1:2, :]
        softplus = jnp.maximum(xg, 0.0) + jnp.log1p(jnp.exp(-jnp.abs(xg)))
        g_all = -jnp.exp(hp_ref[0:1, :]) * softplus
        if l_valid < c:
            beta = jnp.where(row < l_valid, beta, 0.0)
            g_all = jnp.where(row < l_valid, g_all, 0.0)
        gcum = g_all
        s = 1
        while s < c:
            gcum = gcum + jnp.where(row >= s, pltpu.roll(gcum, s, 0), 0.0)
            s *= 2
        beta_all.append(beta)
        gcum_all.append(gcum)
        gcum_t_all.append(jnp.transpose(_rowpad(gcum, LANE)))
    gn = gn_ref[...]

    def col(h, part=0):
        return slice(part * w + h * GDN_D, part * w + (h + 1) * GDN_D)

    chains = [(b, h) for b in range(nseq) for h in range(GDN_HEADS)]
    for c0 in range(0, len(chains), group):
        ch = chains[c0:c0 + group]
        qh = [cb_ref[b, :, col(h)] for b, h in ch]
        kh = [cb_ref[b, :, col(h, 1)] for b, h in ch]
        vh = [cb_ref[b, :, col(h, 2)] for b, h in ch]
        qn = [x * lax.rsqrt(jnp.sum(x * x, axis=-1, keepdims=True) + EPS) * (GDN_D ** -0.5) for x in qh]
        kn = [x * lax.rsqrt(jnp.sum(x * x, axis=-1, keepdims=True) + EPS) for x in kh]
        bcol = [beta_all[b][:, h:h + 1] for b, h in ch]
        gcol = [gcum_all[b][:, h + GDN_HEADS:h + GDN_HEADS + 1] for b, h in ch]
        grow = [gcum_t_all[b][h + GDN_HEADS:h + GDN_HEADS + 1, :] for b, h in ch]
        s_prev = [s_ref[h] if prompt else sin_ref[b, h] for b, h in ch]
        o, s_new = _gdn_heads(qn, kn, vh, bcol, gcol, grow, s_prev, lvl_ref, c)
        for j, (b, h) in enumerate(ch):
            if prompt:
                s_ref[h] = s_new[j]
            else:
                sout_ref[b, h] = s_new[j]
        zh = [blk(z_ref, b)[:, col(h)] for b, h in ch]
        on = [x * lax.rsqrt(jnp.mean(x * x, axis=-1, keepdims=True) + EPS) * gn for x in o]
        for j, (b, h) in enumerate(ch):
            blk(o_ref, b)[:, col(h)] = (on[j] * _silu(zh[j])).astype(o_ref.dtype)
    if prompt:
        @pl.when(i == n_chunks - 1)
        def _():
            sout_ref[...] = s_ref[...]


def _level_masks(c, l_valid):
    i = jnp.arange(c, dtype=jnp.int32)[:, None]
    j = jnp.arange(LANE, dtype=jnp.int32)[None, :]
    masks = []
    s = 1
    while s < l_valid:
        masks.append((i // (2 * s) == j // (2 * s)) & (i % (2 * s) >= s) & (j % (2 * s) < s))
        s *= 2
    return jnp.stack(masks).astype(F32)


def _gdn_prompt(proj, ba_col, conv_w, hp, gn, c=128):
    t = proj.shape[0]
    n_chunks = t // c
    w = GDN_W
    lvl = _level_masks(c, c)
    kern = functools.partial(_gdn_kernel, c=c, l_valid=c, prompt=True, n_chunks=n_chunks, group=GDN_GROUP_PROMPT)
    return pl.pallas_call(
        kern,
        grid=(n_chunks,),
        in_specs=[pl.BlockSpec((c, w), lambda i: (i, 0)),
                  pl.BlockSpec((c, w), lambda i: (i, 1)),
                  pl.BlockSpec((c, w), lambda i: (i, 2)),
                  pl.BlockSpec((c, w), lambda i: (i, 3)),
                  pl.BlockSpec((c, LANE), lambda i: (i, ba_col)),
                  pl.BlockSpec((CONV_W, 3 * w), lambda i: (0, 0)),
                  pl.BlockSpec((8, LANE), lambda i: (0, 0)),
                  pl.BlockSpec((1, GDN_D), lambda i: (0, 0)),
                  pl.BlockSpec(lvl.shape, lambda i: (0, 0, 0))],
        out_specs=[pl.BlockSpec((c, w), lambda i: (i, 0)),
                   pl.BlockSpec((GDN_HEADS, GDN_D, GDN_D), lambda i: (0, 0, 0)),
                   pl.BlockSpec((8, 3 * w), lambda i: (0, 0))],
        out_shape=[jax.ShapeDtypeStruct((t, w), BF16),
                   jax.ShapeDtypeStruct((GDN_HEADS, GDN_D, GDN_D), F32),
                   jax.ShapeDtypeStruct((8, 3 * w), F32)],
        scratch_shapes=[pltpu.VMEM((1, 8 + c, 3 * w), F32),
                        pltpu.VMEM((1, c, 3 * w), F32),
                        pltpu.VMEM((GDN_HEADS, GDN_D, GDN_D), F32)],
        compiler_params=_cparams(("arbitrary",)),
        name="gdn_prompt",
    )(proj, proj, proj, proj, proj, conv_w, hp, gn, lvl)


def _gdn_sample(proj3, ba_col, conv_w, hp, gn, conv_in, s_in, l):
    b, c, _ = proj3.shape
    w = GDN_W
    lvl = _level_masks(c, l)
    kern = functools.partial(_gdn_kernel, c=c, l_valid=l, prompt=False, n_chunks=1, group=GDN_GROUP_SAMPLE)
    ns = math.gcd(b, GDN_SAMPLE_SEQS)
    state_spec = pl.BlockSpec((ns, GDN_HEADS, GDN_D, GDN_D), lambda i: (i, 0, 0, 0))
    return pl.pallas_call(
        kern,
        grid=(b // ns,),
        in_specs=[pl.BlockSpec((ns, c, w), lambda i: (i, 0, 0)),
                  pl.BlockSpec((ns, c, w), lambda i: (i, 0, 1)),
                  pl.BlockSpec((ns, c, w), lambda i: (i, 0, 2)),
                  pl.BlockSpec((ns, c, w), lambda i: (i, 0, 3)),
                  pl.BlockSpec((ns, c, LANE), lambda i: (i, 0, ba_col)),
                  pl.BlockSpec((CONV_W, 3 * w), lambda i: (0, 0)),
                  pl.BlockSpec((8, LANE), lambda i: (0, 0)),
                  pl.BlockSpec((1, GDN_D), lambda i: (0, 0)),
                  pl.BlockSpec(lvl.shape, lambda i: (0, 0, 0)),
                  pl.BlockSpec((ns, 8, 3 * w), lambda i: (i, 0, 0)),
                  state_spec],
        out_specs=[pl.BlockSpec((ns, c, w), lambda i: (i, 0, 0)),
                   state_spec,
                   pl.BlockSpec((ns, 8, 3 * w), lambda i: (i, 0, 0))],
        out_shape=[jax.ShapeDtypeStruct((b, c, w), F32),
                   jax.ShapeDtypeStruct((b, GDN_HEADS, GDN_D, GDN_D), F32),
                   jax.ShapeDtypeStruct((b, 8, 3 * w), F32)],
        scratch_shapes=[pltpu.VMEM((ns, 8 + c, 3 * w), F32),
                        pltpu.VMEM((ns, c, 3 * w), F32)],
        compiler_params=_cparams(("arbitrary",)),
        name="gdn_sample",
    )(proj3, proj3, proj3, proj3, proj3, conv_w, hp, gn, lvl, conv_in, s_in)


def _t5_bucket(dist):
    d = jnp.maximum(dist, 0)
    max_exact = N_BUCKETS // 2
    large = max_exact + (jnp.log(jnp.maximum(d, 1).astype(F32) / max_exact)
                         / math.log(WINDOW / max_exact) * (N_BUCKETS - max_exact)).astype(jnp.int32)
    large = jnp.minimum(large, N_BUCKETS - 1)
    return jnp.where(d < max_exact, d, large)


def _bias_table(dist, rel_bias):
    valid = (dist >= 0) & (dist < WINDOW)
    onehot = (_t5_bucket(dist)[None] == jnp.arange(N_BUCKETS, dtype=jnp.int32)[:, None, None]).astype(F32)
    b = jnp.einsum('bh,bqs->hqs', rel_bias.astype(F32), onehot, precision=lax.Precision.HIGHEST)
    return jnp.where(valid[None], b, NEG)


def _lo_hi(slab, g):
    lane = lax.broadcasted_iota(jnp.int32, slab.shape, 1)
    if g % 2 == 0:
        lo = jnp.where(lane < SWA_HD, slab, 0.0)
        hi = pltpu.roll(lo, SWA_HD, 1)
    else:
        hi = jnp.where(lane >= SWA_HD, slab, 0.0)
        lo = pltpu.roll(hi, SWA_HD, 1)
    return lo, hi


def _sink_softmax_chains(parts, sinks):
    cs = range(len(parts))
    m = [sinks[i] for i in cs]
    for j in range(len(parts[0])):
        mx = [jnp.max(parts[i][j], axis=-1, keepdims=True) for i in cs]
        m = [jnp.maximum(m[i], mx[i]) for i in cs]
    es = [[jnp.exp(p - m[i]) for p in parts[i]] for i in cs]
    den = [jnp.exp(sinks[i] - m[i]) for i in cs]
    for j in range(len(parts[0])):
        sm = [jnp.sum(es[i][j], axis=-1, keepdims=True) for i in cs]
        den = [den[i] + sm[i] for i in cs]
    inv = [1.0 / d for d in den]
    return [[e * inv[i] for e in es[i]] for i in cs]


SWA_SAMPLE_SEQS = 4
SWA_GROUP_PROMPT = 8


def _swa_prompt_kernel(q_ref, kp_ref, kc_ref, vp_ref, vc_ref, bias_ref, sink_ref, o_ref):
    n = pl.program_id(0)
    wq = WINDOW
    col = lax.broadcasted_iota(jnp.int32, (wq, 2 * wq), 1)
    first = jnp.logical_and(n == 0, col < wq)
    sink_all = sink_ref[...]
    scale = SWA_HD ** -0.5
    for g0 in range(0, SWA_KV, SWA_GROUP_PROMPT):
        gs = range(g0, g0 + SWA_GROUP_PROMPT)
        kv = {}
        for g in gs:
            sl = slice((g // 2) * LANE, (g // 2 + 1) * LANE)
            kslab = jnp.concatenate([kp_ref[:, sl], kc_ref[:, sl]], axis=0)
            vslab = jnp.concatenate([vp_ref[:, sl], vc_ref[:, sl]], axis=0)
            kv[g] = ([x.astype(BF16) for x in _lo_hi(kslab, g)], [x.astype(BF16) for x in _lo_hi(vslab, g)])
        chains = [(g, s, half) for g in gs for s in range(2) for half in range(2)]
        qs = {(g, s): q_ref[:, (2 * g + s) * LANE:(2 * g + s + 1) * LANE].astype(BF16) for g in gs for s in range(2)}
        logits = [_dot_nt(qs[g, s], kv[g][0][half]) for g, s, half in chains]
        logits = [jnp.where(first, NEG, logits[i] * scale + bias_ref[4 * g + 2 * s + half])
                  for i, (g, s, half) in enumerate(chains)]
        sinks = [sink_all[:, 4 * g + 2 * s + half:4 * g + 2 * s + half + 1] for g, s, half in chains]
        probs = _sink_softmax_chains([[x] for x in logits], sinks)
        pv = [_dot(probs[i][0].astype(BF16), kv[g][1][half]) for i, (g, s, half) in enumerate(chains)]
        for i, (g, s, half) in enumerate(chains):
            if half == 0:
                o_ref[:, (2 * g + s) * LANE:(2 * g + s + 1) * LANE] = (pv[i] + pv[i + 1]).astype(o_ref.dtype)


def _swa_prompt(proj, qcol, bias, sinks):
    t = proj.shape[0]
    nb = t // WINDOW
    kb = qcol * (SWA_W // SWA_KVW) + SWA_W // SWA_KVW
    return pl.pallas_call(
        _swa_prompt_kernel,
        grid=(nb,),
        in_specs=[pl.BlockSpec((WINDOW, SWA_W), lambda n: (n, qcol)),
                  pl.BlockSpec((WINDOW, SWA_KVW), lambda n: (jnp.maximum(n - 1, 0), kb)),
                  pl.BlockSpec((WINDOW, SWA_KVW), lambda n: (n, kb)),
                  pl.BlockSpec((WINDOW, SWA_KVW), lambda n: (jnp.maximum(n - 1, 0), kb + 1)),
                  pl.BlockSpec((WINDOW, SWA_KVW), lambda n: (n, kb + 1)),
                  pl.BlockSpec((SWA_HEADS, WINDOW, 2 * WINDOW), lambda n: (0, 0, 0)),
                  pl.BlockSpec((1, LANE), lambda n: (0, 0))],
        out_specs=pl.BlockSpec((WINDOW, SWA_W), lambda n: (n, 0)),
        out_shape=jax.ShapeDtypeStruct((t, SWA_W), BF16),
        compiler_params=_cparams(("arbitrary",)),
        name="swa_prompt",
    )(proj, proj, proj, proj, proj, bias, sinks)


def _swa_sample_kernel(q_ref, kn_ref, vn_ref, kc_ref, vc_ref, bc_ref, bn_ref, sink_ref, o_ref, ko_ref, vo_ref, *, l):
    r = 8
    wb = WINDOW
    nseq = q_ref.shape[0]
    seqs = range(nseq)
    knew = [kn_ref[b] for b in seqs]
    vnew = [vn_ref[b] for b in seqs]
    row = lax.broadcasted_iota(jnp.int32, (LANE, wb), 0)
    col = lax.broadcasted_iota(jnp.int32, (LANE, wb), 1)
    place = jnp.where(jnp.logical_and(row < l, col == row + (wb - l)), 1.0, 0.0).astype(BF16)
    lane_w = lax.broadcasted_iota(jnp.int32, (SWA_KVW, wb), 1)
    for b in seqs:
        for c_ref, new, out_ref in ((kc_ref, knew[b], ko_ref), (vc_ref, vnew[b], vo_ref)):
            hi = new.astype(BF16)
            r1 = new - hi.astype(F32)
            mid = r1.astype(BF16)
            lo = (r1 - mid.astype(F32)).astype(BF16)
            moved = ((_dot_tn(_rowpad(hi, LANE), place) + _dot_tn(_rowpad(mid, LANE), place))
                     + _dot_tn(_rowpad(lo, LANE), place))
            out_ref[b] = jnp.where(lane_w >= wb - l, moved, pltpu.roll(c_ref[b], wb - l, 1))
    sink_all = sink_ref[...]
    scale = SWA_HD ** -0.5
    gs = range(SWA_KV)
    all_chains = [(b, g, half) for b in seqs for g in gs for half in range(2)]
    sls =[slice((g // 2) * LANE, (g // 2 + 1) * LANE) for g in gs]
    zero = jnp.zeros((SWA_HD, wb), BF16)

    def lo_hi_t(ref, b, g):
        t = ref[b, g * SWA_HD:(g + 1) * SWA_HD, :].astype(BF16)
        return [jnp.concatenate([t, zero], axis=0), jnp.concatenate([zero, t], axis=0)]
    kc = {(b, g): lo_hi_t(kc_ref, b, g) for b in seqs for g in gs}
    vc = {(b, g): lo_hi_t(vc_ref, b, g) for b in seqs for g in gs}
    kn = {(b, g): [_rowpad(x, LANE).astype(BF16) for x in _lo_hi(knew[b][:, sls[g]], g)] for b in seqs for g in gs}
    vn = {(b, g): [_rowpad(x, LANE).astype(BF16) for x in _lo_hi(vnew[b][:, sls[g]], g)] for b in seqs for g in gs}
    qs = {(b, g): jnp.concatenate([q_ref[b, :, (2 * g) * LANE:(2 * g + 1) * LANE],
                                   q_ref[b, :, (2 * g + 1) * LANE:(2 * g + 2) * LANE]], axis=0).astype(BF16)
          for b in seqs for g in gs}
    for b0 in seqs:
        chains = [(b, g, half) for b, g, half in all_chains if b == b0]
        lc = [_dot(qs[b, g], kc[b, g][half]) for b, g, half in chains]
        ln = [_dot_nt(qs[b, g], kn[b, g][half]) for b, g, half in chains]
        lc = [lc[i] * scale + bc_ref[2 * g + half] for i, (b, g, half) in enumerate(chains)]
        ln = [ln[i] * scale + bn_ref[2 * g + half] for i, (b, g, half) in enumerate(chains)]
        sinks = [jnp.concatenate([jnp.broadcast_to(sink_all[:, 4 * g + half:4 * g + half + 1], (r, 1)),
                                  jnp.broadcast_to(sink_all[:, 4 * g + 2 + half:4 * g + 2 + half + 1], (r, 1))], axis=0)
                 for b, g, half in chains]
        probs = _sink_softmax_chains([[lc[i], ln[i]] for i in range(len(chains))], sinks)
        pvc = [_dot_nt(probs[i][0].astype(BF16), vc[b, g][half]) for i, (b, g, half) in enumerate(chains)]
        pvn = [_dot(probs[i][1].astype(BF16), vn[b, g][half]) for i, (b, g, half) in enumerate(chains)]
        for i, (b, g, half) in enumerate(chains):
            if half == 0:
                acc = (pvc[i] + pvn[i]) + (pvc[i + 1] + pvn[i + 1])
                o_ref[b, :, (2 * g) * LANE:(2 * g + 1) * LANE] = acc[:r]
                o_ref[b, :, (2 * g + 1) * LANE:(2 * g + 2) * LANE] = acc[r:]


def _swa_sample(proj3, qcol, k_cache, v_cache, bias_c, bias_n, sinks, l):
    b, r, _ = proj3.shape
    kb = qcol * (SWA_W // SWA_KVW) + SWA_W // SWA_KVW
    ns = math.gcd(b, SWA_SAMPLE_SEQS)
    cache_spec = pl.BlockSpec((ns, SWA_KVW, WINDOW), lambda i: (i, 0, 0))
    return pl.pallas_call(
        functools.partial(_swa_sample_kernel, l=l),
        grid=(b // ns,),
        in_specs=[pl.BlockSpec((ns, r, SWA_W), lambda i: (i, 0, qcol)),
                  pl.BlockSpec((ns, r, SWA_KVW), lambda i: (i, 0, kb)),
                  pl.BlockSpec((ns, r, SWA_KVW), lambda i: (i, 0, kb + 1)),
                  cache_spec, cache_spec,
                  pl.BlockSpec((2 * SWA_KV, 2 * r, LANE), lambda i: (0, 0, 0)),
                  pl.BlockSpec((2 * SWA_KV, 2 * r, LANE), lambda i: (0, 0, 0)),
                  pl.BlockSpec((1, LANE), lambda i: (0, 0))],
        out_specs=[pl.BlockSpec((ns, r, SWA_W), lambda i: (i, 0, 0)), cache_spec, cache_spec],
        out_shape=[jax.ShapeDtypeStruct((b, r, SWA_W), F32),
                   jax.ShapeDtypeStruct(k_cache.shape, F32),
                   jax.ShapeDtypeStruct(v_cache.shape, F32)],
        compiler_params=_cparams(("arbitrary",)),
        name="swa_sample",
    )(proj3, proj3, proj3, k_cache, v_cache, bias_c, bias_n, sinks)


MOE_WCHUNK = 1024
MOE_WSLOTS = 6
MOE_LOOKAHEAD = 5
MOE_ROWS_SMALL = 256
MOE_ROW_UNROLL = 8


def _moe_kernel(ord_ref, be_ref, bs_ref, bn_ref, h_hbm, wg_hbm, wu_hbm, wd_hbm, y_hbm,
                xbuf, x16, hgu, hid16, acc, ybuf, wbuf, gsem, ssem, wsem, *, n_blocks, t_all):
    k = pl.program_id(0)
    n = bn_ref[k]
    slot = lax.rem(k, 2)
    d = x16.shape[1]
    half = d // 2
    kc = MOE_WCHUNK // 2
    n_gu = d // kc
    n_dn = d // MOE_WCHUNK
    n_ch = n_gu + n_dn
    assert n_ch % MOE_WSLOTS == 0 and wg_hbm.shape[2] == MOE_WCHUNK

    def gather_copy(tok, r):
        return pltpu.make_async_copy(h_hbm.at[pl.ds(tok, 1)], xbuf.at[pl.ds(r, 1)], gsem)

    def scatter_copy(sl, r, a):
        return pltpu.make_async_copy(ybuf.at[sl, pl.ds(r, 1)], y_hbm.at[pl.ds(a, 1)], ssem.at[sl])

    def w_copies(e, j):
        s = j % MOE_WSLOTS
        if j < n_gu:
            rows = pl.ds(j * kc, kc)
            return [pltpu.make_async_copy(wg_hbm.at[e, rows, :], wbuf.at[s, pl.ds(0, kc), :], wsem.at[s]),
                    pltpu.make_async_copy(wu_hbm.at[e, rows, :], wbuf.at[s, pl.ds(kc, kc), :], wsem.at[s])]
        cols = pl.ds((j - n_gu) * MOE_WCHUNK, MOE_WCHUNK)
        return [pltpu.make_async_copy(wd_hbm.at[e, :, cols], wbuf.at[s], wsem.at[s])]

    def for_rows(cnt, fn):
        groups = lax.shift_right_logical(cnt, MOE_ROW_UNROLL.bit_length() - 1)

        def group(q, carry):
            for u in range(MOE_ROW_UNROLL):
                fn(q * MOE_ROW_UNROLL + u)
            return carry

        def single(r, carry):
            fn(r)
            return carry
        lax.fori_loop(0, groups, group, 0)
        lax.fori_loop(groups * MOE_ROW_UNROLL, cnt, single, 0)

    def gather_start(kk):
        st = bs_ref[kk]

        def one(r):
            a = ord_ref[st + r]
            gather_copy(jnp.where(a >= t_all, a - t_all, a), r).start()
        for_rows(bn_ref[kk], one)

    def gather_wait(kk):
        for_rows(bn_ref[kk], lambda r: gather_copy(0, 0).wait())

    def scatter_start(kk, sl):
        st = bs_ref[kk]
        for_rows(bn_ref[kk], lambda r: scatter_copy(sl, r, ord_ref[st + r]).start())

    def scatter_wait(kk, sl):
        for_rows(bn_ref[kk], lambda r: scatter_copy(sl, 0, 0).wait())

    @pl.when(n > 0)
    def _():
        e = be_ref[k]
        k1 = jnp.minimum(k + 1, n_blocks - 1)
        has_next = jnp.logical_and(k + 1 < n_blocks, bn_ref[k1] > 0)
        e_next = be_ref[k1]

        @pl.when(k == 0)
        def _():
            xbuf[...] = jnp.zeros(xbuf.shape, xbuf.dtype)
            gather_start(0)
            for j in range(MOE_LOOKAHEAD):
                for cp in w_copies(e, j):
                    cp.start()
        gather_wait(k)
        words = xbuf[...]
        lo = lax.bitcast_convert_type(lax.shift_left(words, jnp.uint32(16)), F32)
        hi = lax.bitcast_convert_type(words & jnp.uint32(0xFFFF0000), F32)
        x16[:, 0:half] = lo.astype(BF16)
        x16[:, half:d] = hi.astype(BF16)

        @pl.when(has_next)
        def _():
            gather_start(k1)

        def ffn(m):
            for j in range(n_ch):
                jn = j + MOE_LOOKAHEAD
                if jn < n_ch:
                    for cp in w_copies(e, jn):
                        cp.start()
                else:
                    @pl.when(has_next)
                    def _():
                        for cp in w_copies(e_next, jn - n_ch):
                            cp.start()
                for cp in w_copies(e, j):
                    cp.wait()
                s = j % MOE_WSLOTS
                if j < n_gu:
                    xs = x16[0:m, j * kc:(j + 1) * kc]
                    pg = _dot(xs, wbuf[s, 0:kc, :].astype(BF16))
                    pu = _dot(xs, wbuf[s, kc:2 * kc, :].astype(BF16))
                    if j == 0:
                        hgu[0, 0:m] = pg
                        hgu[1, 0:m] = pu
                    else:
                        hgu[0, 0:m] += pg
                        hgu[1, 0:m] += pu
                    if j == n_gu - 1:
                        hid16[0:m] = (_silu(hgu[0, 0:m]) * hgu[1, 0:m]).astype(BF16)
                else:
                    cols = slice((j - n_gu) * MOE_WCHUNK, (j - n_gu + 1) * MOE_WCHUNK)
                    acc[0:m, cols] = _dot(hid16[0:m], wbuf[s].astype(BF16))
            bits = lax.bitcast_convert_type(acc[0:m].astype(BF16).astype(F32), jnp.uint32)
            ybuf[slot, 0:m] = (lax.shift_right_logical(bits[:, :half], jnp.uint32(16))
                               | (bits[:, half:] & jnp.uint32(0xFFFF0000)))

        @pl.when(n <= MOE_ROWS_SMALL)
        def _():
            ffn(MOE_ROWS_SMALL)

        @pl.when(n > MOE_ROWS_SMALL)
        def _():
            ffn(MOE_ROWS)
        scatter_start(k, slot)

        @pl.when(k > 0)
        def _():
            scatter_wait(k - 1, 1 - slot)

        @pl.when(jnp.logical_not(has_next))
        def _():
            scatter_wait(k, slot)


def _moe_ffn(order, blk_e, blk_start, blk_n, h_packed, w_gate, w_up, w_down):
    a = order.shape[0]
    t_all = h_packed.shape[0]
    d = 2 * h_packed.shape[1]
    n_blocks = blk_e.shape[0]
    grid_spec = pltpu.PrefetchScalarGridSpec(
        num_scalar_prefetch=4,
        grid=(n_blocks,),
        in_specs=[pl.BlockSpec(memory_space=pl.ANY)] * 4,
        out_specs=pl.BlockSpec(memory_space=pl.ANY),
        scratch_shapes=[pltpu.VMEM((MOE_ROWS, d // 2), jnp.uint32),
                        pltpu.VMEM((MOE_ROWS, d), BF16),
                        pltpu.VMEM((2, MOE_ROWS, D_EXPERT), F32),
                        pltpu.VMEM((MOE_ROWS, D_EXPERT), BF16),
                        pltpu.VMEM((MOE_ROWS, d), F32),
                        pltpu.VMEM((2, MOE_ROWS, d // 2), jnp.uint32),
                        pltpu.VMEM((MOE_WSLOTS, MOE_WCHUNK, MOE_WCHUNK), F32),
                        pltpu.SemaphoreType.DMA(()),
                        pltpu.SemaphoreType.DMA((2,)),
                        pltpu.SemaphoreType.DMA((MOE_WSLOTS,))],
    )
    return pl.pallas_call(
        functools.partial(_moe_kernel, n_blocks=n_blocks, t_all=t_all),
        grid_spec=grid_spec,
        out_shape=jax.ShapeDtypeStruct((a, d // 2), jnp.uint32),
        compiler_params=_cparams(("arbitrary",)),
        name="moe_ffn",
    )(order, blk_e, blk_start, blk_n, h_packed, w_gate, w_up, w_down)


def _unpack_pairs(words):
    lo = lax.bitcast_convert_type(lax.shift_left(words, jnp.uint32(16)), F32)
    hi = lax.bitcast_convert_type(words & jnp.uint32(0xFFFF0000), F32)
    return jnp.concatenate([lo, hi], axis=1)


def _final_kernel(x_ref, y0_ref, y1_ref, p_ref, g_ref, nf_ref, o_ref):
    p = p_ref[...]
    moe = _unpack_pairs(y0_ref[...]) * p[:, 0:1] + _unpack_pairs(y1_ref[...]) * p[:, 1:2]
    x = x_ref[...] + _mod_rows(g_ref, x_ref.shape[0]) * moe
    o_ref[...] = x * lax.rsqrt(jnp.mean(x * x, axis=-1, keepdims=True) + EPS) * nf_ref[...]


def _final(x, y, row0, gates, gt2, nf, tm=256):
    t, d = x.shape
    t_all = gates.shape[0]
    tm = math.gcd(math.gcd(tm, t), math.gcd(row0, t_all))
    b0 = row0 // tm
    b1 = t_all // tm + b0
    return pl.pallas_call(
        _final_kernel,
        grid=(t // tm,),
        in_specs=[pl.BlockSpec((tm, d), lambda i: (i, 0)),
                  pl.BlockSpec((tm, d // 2), lambda i: (b0 + i, 0)),
                  pl.BlockSpec((tm, d // 2), lambda i: (b1 + i, 0)),
                  pl.BlockSpec((tm, LANE), lambda i: (b0 + i, 0)),
                  _mod_spec(gt2, tm, d),
                  pl.BlockSpec((1, d), lambda i: (0, 0))],
        out_specs=pl.BlockSpec((tm, d), lambda i: (i, 0)),
        out_shape=jax.ShapeDtypeStruct((t, d), F32),
        compiler_params=_cparams(("arbitrary",)),
        name="final",
    )(x, y, y, gates, gt2.arr, nf.reshape(1, d))


def _dispatch(expert_idx):
    t, k = expert_idx.shape
    a = t * k
    n_blocks = a // MOE_ROWS + N_EXPERTS
    flat_e = expert_idx.T.reshape(a)
    order = jnp.argsort(flat_e, stable=True).astype(jnp.int32)
    counts = jnp.zeros((N_EXPERTS,), jnp.int32).at[flat_e].add(1)
    seg_start = jnp.cumsum(counts) - counts
    nblk = (counts + MOE_ROWS - 1) // MOE_ROWS
    blk_end = jnp.cumsum(nblk)
    n_active = blk_end[-1]
    kk = jnp.minimum(jnp.arange(n_blocks, dtype=jnp.int32), n_active - 1)
    blk_e = jnp.minimum(jnp.searchsorted(blk_end, kk, side='right'), N_EXPERTS - 1).astype(jnp.int32)
    j = kk - (blk_end[blk_e] - nblk[blk_e])
    blk_start = (seg_start[blk_e] + j * MOE_ROWS).astype(jnp.int32)
    blk_n = jnp.clip(counts[blk_e] - j * MOE_ROWS, 0, MOE_ROWS)
    blk_n = jnp.where(jnp.arange(n_blocks) < n_active, blk_n, 0).astype(jnp.int32)
    return order, blk_e, blk_start, blk_n


def kernel(x_prompt, x_sample, state_gdn, state_conv, cache_swa_k, cache_swa_v, c_prompt, c_sample, w_ada, b_ada, norm_mix, w_in, conv_w, a_log, dt_bias, gdn_norm, swa_sinks, rel_bias, w_out, norm_moe, router_group, router_group_bias, router_expert, router_expert_bias, w_gate, w_up, w_down, norm_final):
    depth = w_ada.shape[0]
    assert depth == 1
    bp, seq, d = x_prompt.shape
    assert bp == 1 and seq % WINDOW == 0
    bs, ls, _ = x_sample.shape
    tp = bp * seq
    ts = bs * ls
    n_main = 2 * GDN_W + 2 * GDN_W
    n_ba = 2 * GDN_HEADS

    assert ts % 8 == 0
    c_all = jnp.concatenate([jnp.repeat(c_sample, ls, axis=0), c_prompt, jnp.zeros((15, d), c_prompt.dtype)], axis=0)
    mod = _adaln(c_all, w_ada[0], b_ada[0])
    mod_p = [Mod(mod, j, ts) for j in range(6)]
    mod_s = [Mod(mod, j, None) for j in range(6)]

    n_gdn = n_main + 512
    w_in_t = w_in[0].T
    w_swa_t = w_in_t[n_main + n_ba:]
    w_out16 = w_out[0]
    ba_col = n_main // LANE
    kcol = SWA_W
    hp = jnp.zeros((8, LANE), F32)
    hp = hp.at[0, GDN_HEADS:2 * GDN_HEADS].set(a_log[0]).at[1, GDN_HEADS:2 * GDN_HEADS].set(dt_bias[0])
    gn = gdn_norm[0].reshape(1, GDN_D)
    sinks = jnp.pad(swa_sinks[0], (0, LANE - SWA_HEADS)).reshape(1, LANE)

    xp = x_prompt.reshape(tp, d)
    h_p = _norm_mod(xp, norm_mix[0], mod_p[1], mod_p[0])
    proj_p = _matmul_f32wt(h_p, w_in_t, n_gdn, 1024, 512, name="inproj_p")
    pswa_p = _matmul_f32wt(h_p, w_swa_t, w_swa_t.shape[0], 1024, 512, name="inproj_swa_p")
    og_p, s_p, conv_p = _gdn_prompt(proj_p, ba_col, conv_w[0], hp, gn)
    conv_p = conv_p[8 - (CONV_W - 1):]
    qi = jnp.arange(WINDOW, dtype=jnp.int32)
    sj = jnp.arange(2 * WINDOW, dtype=jnp.int32)
    bias_p = _bias_table((WINDOW + qi)[:, None] - sj[None, :], rel_bias)
    os_p = _swa_prompt(pswa_p, 0, bias_p, sinks)
    k_p = pswa_p[tp - WINDOW:, kcol:kcol + SWA_KVW]
    v_p = pswa_p[tp - WINDOW:, kcol + SWA_KVW:kcol + 2 * SWA_KVW]
    x1_p = _outproj(og_p, os_p, w_out16, xp, mod_p[2])

    xs = x_sample.reshape(ts, d)
    h_s = _norm_mod(xs, norm_mix[0], mod_s[1], mod_s[0])
    proj_s = _matmul_f32wt(h_s, w_in_t, n_gdn, 512, 512, name="inproj_s")
    pswa_s = _matmul_f32wt(h_s, w_swa_t, w_swa_t.shape[0], 512, 512, name="inproj_swa_s")
    rs = 8
    assert CONV_W - 1 <= ls <= rs
    proj_s8 = jnp.pad(proj_s.reshape(bs, ls, n_gdn), ((0, 0), (0, rs - ls), (0, 0)))
    pswa_s8 = jnp.pad(pswa_s.reshape(bs, ls, pswa_s.shape[1]), ((0, 0), (0, rs - ls), (0, 0)))
    conv_in8 = jnp.pad(state_conv[0], ((0, 0), (8 - (CONV_W - 1), 0), (0, 0)))
    og_s, s_s, conv_s = _gdn_sample(proj_s8, ba_col, conv_w[0], hp, gn, conv_in8, state_gdn[0], ls)
    conv_s = conv_s[:, 8 - (CONV_W - 1):]
    wb = cache_swa_k.shape[2]
    assert wb == WINDOW == LANE
    dist_s = (wb + jnp.arange(ls, dtype=jnp.int32))[:, None] - jnp.arange(wb + ls, dtype=jnp.int32)[None, :]
    bias_s = _bias_table(dist_s, rel_bias)
    bias_s = jnp.pad(bias_s, ((0, 0), (0, rs - ls), (0, 2 * LANE - wb - ls)), constant_values=NEG)
    bias_s = bias_s.reshape(SWA_KV, 2, 2, rs, 2 * LANE)
    bias_s = jnp.transpose(bias_s, (0, 2, 1, 3, 4)).reshape(2 * SWA_KV, 2 * rs, 2 * LANE)
    k_cache = jnp.transpose(cache_swa_k[0], (0, 2, 3, 1)).reshape(bs, SWA_KVW, wb)
    v_cache = jnp.transpose(cache_swa_v[0], (0, 2, 3, 1)).reshape(bs, SWA_KVW, wb)
    os_s, k_s, v_s = _swa_sample(pswa_s8, 0, k_cache, v_cache, bias_s[:, :, :LANE], bias_s[:, :, LANE:], sinks, ls)
    x1_s = _outproj(og_s[:, :ls].reshape(ts, GDN_W).astype(BF16), os_s[:, :ls].reshape(ts, SWA_W).astype(BF16),
                    w_out16, xs, mod_s[2], tm=512)

    wr = jnp.concatenate([router_group[0], router_expert[0]], axis=1)
    wr2 = jnp.pad(wr, ((0, 0), (0, LANE - wr.shape[1]))).astype(BF16)
    br = jnp.pad(jnp.concatenate([router_group_bias[0], router_expert_bias[0]]),
                 (0, LANE - N_GROUPS - N_EXPERTS)).reshape(1, LANE)
    t_all = tp + ts
    h2, eidx, gates = _norm_router(x1_p, x1_s, norm_moe[0], mod_p[4], mod_p[3], mod_s[4], mod_s[3], wr2, br)
    order, blk_e, blk_start, blk_n = _dispatch(eidx[:, :2])
    y = _moe_ffn(order, blk_e, blk_start, blk_n, h2, w_gate[0], w_up[0], w_down[0])
    y_p = _final(x1_p, y, 0, gates, mod_p[5], norm_final)
    y_s = _final(x1_s, y, tp, gates, mod_s[5], norm_final)

    sdt = state_gdn.dtype
    return (y_p.reshape(bp, seq, d), y_s.reshape(bs, ls, d),
            s_p.reshape(1, bp, GDN_HEADS, GDN_D, GDN_D).astype(sdt), s_s[None].astype(sdt),
            conv_p.reshape(1, bp, CONV_W - 1, 3 * GDN_W).astype(state_conv.dtype), conv_s[None].astype(state_conv.dtype),
            k_p.reshape(1, bp, WINDOW, SWA_KV, SWA_HD).astype(cache_swa_k.dtype),
            jnp.transpose(k_s.reshape(bs, SWA_KV, SWA_HD, wb), (0, 3, 1, 2))[None].astype(cache_swa_k.dtype),
            v_p.reshape(1, bp, WINDOW, SWA_KV, SWA_HD).astype(cache_swa_v.dtype),
            jnp.transpose(v_s.reshape(bs, SWA_KV, SWA_HD, wb), (0, 3, 1, 2))[None].astype(cache_swa_v.dtype))
```
